```python
import math
import jax, jax.numpy as jnp
from jax import lax
import numpy as np

D_MODEL = 1024
BATCH = 1
SEQ = 16384
DEPTH = 4

MEM_LEN = 256
N_MIXERS = 3
EXPAND = 2
BRANCH = EXPAND * D_MODEL
CONV_WIDTH = 3
DIFF_HEAD_DIM = 64
DIFF_HEADS = BRANCH // (2 * DIFF_HEAD_DIM)
SWA_HEAD_DIM = 64
SWA_Q_HEADS = BRANCH // SWA_HEAD_DIM
SWA_KV_HEADS = SWA_Q_HEADS // 8
WINDOW = 128
Q_BLOCK = 128
MEM_HEADS = 4
MEM_HEAD_DIM = 64
MEM_WIDTH = MEM_HEADS * MEM_HEAD_DIM
GATE_WIDTH = BRANCH + MEM_WIDTH
ALIBI_MAX_BIAS = 8.0
EPS = 1e-6
NEG_INF = -1e30

CONV_COLS = (BRANCH, BRANCH, BRANCH, MEM_WIDTH, GATE_WIDTH)
DIFF_COLS = (DIFF_HEADS * 2 * DIFF_HEAD_DIM, DIFF_HEADS * 2 * DIFF_HEAD_DIM,
             DIFF_HEADS * 2 * DIFF_HEAD_DIM, MEM_WIDTH, GATE_WIDTH)
SWA_COLS = (SWA_Q_HEADS * SWA_HEAD_DIM, SWA_KV_HEADS * SWA_HEAD_DIM,
            SWA_KV_HEADS * SWA_HEAD_DIM, MEM_WIDTH, GATE_WIDTH)

kernel_name = "interleaved_conv_diffattn_swa_sink_hybrid"


def rms_norm(x, g):
    xf = x.astype(jnp.float32)
    y = xf * lax.rsqrt(jnp.mean(xf * xf, axis=-1, keepdims=True) + EPS)
    return (y * g.astype(jnp.float32)).astype(x.dtype)


def split_cols(t, widths):
    idx = [int(v) for v in np.cumsum(widths)[:-1]]
    return jnp.split(t, idx, axis=-1)


def alibi_slopes(n_heads):
    return 2.0 ** (-ALIBI_MAX_BIAS * jnp.arange(1, n_heads + 1, dtype=jnp.float32) / n_heads)


def diff_lambda_init(layer_idx):
    return 0.8 - 0.6 * math.exp(-0.3 * layer_idx)


def short_conv_mixer(bg, cg, u, conv_w):
    s = u.shape[1]
    z = cg * u
    zp = jnp.pad(z, ((0, 0), (CONV_WIDTH - 1, 0), (0, 0)))
    conv = sum(conv_w[j] * zp[:, j:j + s] for j in range(CONV_WIDTH))
    return bg * conv


def diff_attention(q, k, v, positions, lam, slopes):
    b, s = q.shape[:2]
    nb = s // Q_BLOCK
    scale = DIFF_HEAD_DIM ** -0.5
    kf = k.astype(jnp.float32)
    vf = v.astype(jnp.float32)
    qb = jnp.moveaxis(q.reshape(b, nb, Q_BLOCK, DIFF_HEADS, 2, DIFF_HEAD_DIM), 1, 0)
    pb = jnp.moveaxis(positions.reshape(b, nb, Q_BLOCK), 1, 0)

    def block(args):
        q_blk, p_blk = args
        logits = jnp.einsum('bqhcd,bshcd->bchqs', q_blk.astype(jnp.float32), kf) * scale
        rel = p_blk[:, :, None] - positions[:, None, :]
        alibi = -slopes[None, :, None, None] * rel[:, None].astype(jnp.float32)
        logits = jnp.where((rel >= 0)[:, None, None], logits + alibi[:, None], NEG_INF)
        probs = jax.nn.softmax(logits, axis=-1)
        weights = probs[:, 0] - lam * probs[:, 1]
        return jnp.einsum('bhqs,bshe->bqhe', weights, vf)

    out = lax.map(block, (qb, pb))
    return jnp.moveaxis(out, 0, 1).reshape(b, s, DIFF_HEADS, 2 * DIFF_HEAD_DIM)


def swa_with_sinks(q, k, v, positions, sinks, slopes):
    b, s = q.shape[:2]
    nb = s // WINDOW
    g = SWA_Q_HEADS // SWA_KV_HEADS
    scale = SWA_HEAD_DIM ** -0.5
    qb = q.reshape(b, nb, WINDOW, SWA_KV_HEADS, g, SWA_HEAD_DIM).astype(jnp.float32)

    def band(t):
        tb = t.reshape(b, nb, WINDOW, SWA_KV_HEADS, SWA_HEAD_DIM).astype(jnp.float32)
        prev = jnp.pad(tb[:, :-1], ((0, 0), (1, 0), (0, 0), (0, 0), (0, 0)))
        return jnp.concatenate([prev, tb], axis=2)

    kb, vb = band(k), band(v)
    pos_b = positions.reshape(b, nb, WINDOW)
    pos_prev = jnp.pad(pos_b[:, :-1], ((0, 0), (1, 0), (0, 0)))
    pos_band = jnp.concatenate([pos_prev, pos_b], axis=2)
    key_valid = jnp.concatenate(
        [jnp.broadcast_to((jnp.arange(nb) > 0)[:, None], (nb, WINDOW)),
         jnp.ones((nb, WINDOW), dtype=bool)], axis=1)
    rel = pos_b[..., :, None] - pos_band[..., None, :]
    mask = key_valid[None, :, None, :] & (rel >= 0) & (rel < WINDOW)
    logits = jnp.einsum('bnqkgd,bnskd->bnkgqs', qb, kb) * scale
    logits = logits - slopes.reshape(SWA_KV_HEADS, g)[None, None, :, :, None, None] \
        * rel[:, :, None, None].astype(jnp.float32)
    logits = jnp.where(mask[:, :, None, None], logits, NEG_INF)
    sink = sinks.astype(jnp.float32).reshape(SWA_KV_HEADS, g)[None, None, :, :, None, None]
    m = jnp.maximum(jnp.max(logits, axis=-1, keepdims=True), sink)
    e = jnp.exp(logits - m)
    probs = e / (jnp.sum(e, axis=-1, keepdims=True) + jnp.exp(sink - m))
    out = jnp.einsum('bnkgqs,bnskd->bnqkgd', probs, vb)
    return out.reshape(b, s, SWA_Q_HEADS * SWA_HEAD_DIM)


def memory_attention(q, mem_n, w_mem_kv):
    b, s = q.shape[:2]
    kv = jnp.einsum('bmd,de->bme', mem_n, w_mem_kv)
    km, vm = jnp.split(kv, 2, axis=-1)
    km = km.reshape(b, -1, MEM_HEADS, MEM_HEAD_DIM).astype(jnp.float32)
    vm = vm.reshape(b, -1, MEM_HEADS, MEM_HEAD_DIM).astype(jnp.float32)
    qh = q.reshape(b, s, MEM_HEADS, MEM_HEAD_DIM).astype(jnp.float32)
    logits = jnp.einsum('bshd,bmhd->bhsm', qh, km) * (MEM_HEAD_DIM ** -0.5)
    probs = jax.nn.softmax(logits, axis=-1)
    return jnp.einsum('bhsm,bmhd->bshd', probs, vm).reshape(b, s, MEM_WIDTH)


def setup_inputs(seed: int = 0) -> dict:
    key = jax.random.key(seed)
    ks = iter(jax.random.split(key, 64))

    def nrm(shape, scale):
        return scale * jax.random.normal(next(ks), shape, jnp.float32)

    def gain(n):
        return 1.0 + nrm((n,), 0.02)

    inp = {}
    inp["x"] = nrm((BATCH, SEQ, D_MODEL), 1.0)
    inp["mem"] = nrm((BATCH, MEM_LEN, D_MODEL), 1.0)
    inp["positions"] = jnp.broadcast_to(jnp.arange(SEQ, dtype=jnp.int32)[None], (BATCH, SEQ))
    for i in range(DEPTH):
        kind = i % N_MIXERS
        cols = (CONV_COLS, DIFF_COLS, SWA_COLS)[kind]
        inp[f"norm_pre_{i}"] = gain(D_MODEL)
        inp[f"norm_post_{i}"] = gain(D_MODEL)
        inp[f"norm_mem_{i}"] = gain(D_MODEL)
        inp[f"w_in_{i}"] = nrm((D_MODEL, sum(cols)), D_MODEL ** -0.5)
        inp[f"w_mem_kv_{i}"] = nrm((D_MODEL, 2 * MEM_WIDTH), D_MODEL ** -0.5)
        if kind == 0:
            inp[f"conv_w_{i}"] = nrm((CONV_WIDTH, BRANCH), CONV_WIDTH ** -0.5)
        elif kind == 1:
            inp[f"lambda_q1_{i}"] = nrm((DIFF_HEAD_DIM,), 0.1)
            inp[f"lambda_k1_{i}"] = nrm((DIFF_HEAD_DIM,), 0.1)
            inp[f"lambda_q2_{i}"] = nrm((DIFF_HEAD_DIM,), 0.1)
            inp[f"lambda_k2_{i}"] = nrm((DIFF_HEAD_DIM,), 0.1)
            inp[f"subln_{i}"] = gain(2 * DIFF_HEAD_DIM)
        else:
            inp[f"sinks_{i}"] = nrm((SWA_Q_HEADS,), 0.5)
        inp[f"w_out_{i}"] = nrm((GATE_WIDTH, D_MODEL), GATE_WIDTH ** -0.5)
    return inp


def reference(x, mem, positions,
              norm_pre_0, norm_post_0, norm_mem_0, w_in_0, w_mem_kv_0, conv_w_0, w_out_0,
              norm_pre_1, norm_post_1, norm_mem_1, w_in_1, w_mem_kv_1,
              lambda_q1_1, lambda_k1_1, lambda_q2_1, lambda_k2_1, subln_1, w_out_1,
              norm_pre_2, norm_post_2, norm_mem_2, w_in_2, w_mem_kv_2, sinks_2, w_out_2,
              norm_pre_3, norm_post_3, norm_mem_3, w_in_3, w_mem_kv_3, conv_w_3, w_out_3):
    layers = [
        dict(pre=norm_pre_0, post=norm_post_0, mem_norm=norm_mem_0, w_in=w_in_0,
             w_mem_kv=w_mem_kv_0, w_out=w_out_0, conv_w=conv_w_0),
        dict(pre=norm_pre_1, post=norm_post_1, mem_norm=norm_mem_1, w_in=w_in_1,
             w_mem_kv=w_mem_kv_1, w_out=w_out_1, lq1=lambda_q1_1, lk1=lambda_k1_1,
             lq2=lambda_q2_1, lk2=lambda_k2_1, subln=subln_1),
        dict(pre=norm_pre_2, post=norm_post_2, mem_norm=norm_mem_2, w_in=w_in_2,
             w_mem_kv=w_mem_kv_2, w_out=w_out_2, sinks=sinks_2),
        dict(pre=norm_pre_3, post=norm_post_3, mem_norm=norm_mem_3, w_in=w_in_3,
             w_mem_kv=w_mem_kv_3, w_out=w_out_3, conv_w=conv_w_3),
    ]
    b, s = x.shape[:2]
    diff_slopes = alibi_slopes(DIFF_HEADS)
    swa_slopes = alibi_slopes(SWA_Q_HEADS)
    for i in range(DEPTH):
        p = layers[i]
        kind = i % N_MIXERS
        h = rms_norm(x, p["pre"])
        proj = jnp.einsum('bsd,de->bse', h, p["w_in"])
        mem_n = rms_norm(mem, p["mem_norm"])
        if kind == 0:
            bg, cg, u, q_mem, gate = split_cols(proj, CONV_COLS)
            mix = short_conv_mixer(bg, cg, u, p["conv_w"])
        elif kind == 1:
            q, k, v, q_mem, gate = split_cols(proj, DIFF_COLS)
            q = q.reshape(b, s, DIFF_HEADS, 2, DIFF_HEAD_DIM)
            k = k.reshape(b, s, DIFF_HEADS, 2, DIFF_HEAD_DIM)
            v = v.reshape(b, s, DIFF_HEADS, 2 * DIFF_HEAD_DIM)
            lam_init = diff_lambda_init(i)
            lam = (jnp.exp(jnp.sum(p["lq1"].astype(jnp.float32) * p["lk1"].astype(jnp.float32)))
                   - jnp.exp(jnp.sum(p["lq2"].astype(jnp.float32) * p["lk2"].astype(jnp.float32)))
                   + lam_init)
            o = diff_attention(q, k, v, positions, lam, diff_slopes)
            mix = (rms_norm(o, p["subln"]) * (1.0 - lam_init)).reshape(b, s, BRANCH)
        else:
            q, k, v, q_mem, gate = split_cols(proj, SWA_COLS)
            mix = swa_with_sinks(q, k, v, positions, p["sinks"], swa_slopes)
        mem_out = memory_attention(q_mem, mem_n, p["w_mem_kv"])
        y = jnp.concatenate([mix.astype(x.dtype), mem_out.astype(x.dtype)], axis=-1) * jax.nn.silu(gate)
        y = jnp.einsum('bse,ed->bsd', y, p["w_out"])
        x = x + rms_norm(y, p["post"])
    return x
```

```python
import functools
import math

import jax
import jax.numpy as jnp
from jax import lax
from jax.experimental import pallas as pl
from jax.experimental.pallas import tpu as pltpu

D_MODEL = 1024
SEQ = 16384
DEPTH = 4
MEM_LEN = 256
BRANCH = 2048
CONV_WIDTH = 3
DIFF_HEAD_DIM = 64
DIFF_HEADS = 16
SWA_HEAD_DIM = 64
SWA_Q_HEADS = 32
SWA_KV_HEADS = 4
SWA_GROUP = SWA_Q_HEADS // SWA_KV_HEADS
WINDOW = 128
MEM_HEADS = 4
MEM_HEAD_DIM = 64
MEM_WIDTH = 256
GATE_WIDTH = BRANCH + MEM_WIDTH
ALIBI_MAX_BIAS = 8.0
EPS = 1e-6
NEG_INF = -1e30

LANES = 128
VMEM_LIMIT_BYTES = 56 * 1024 * 1024

PROJ_TM = 512
OUT_TM = 256
DIFF_TQ = 256
DIFF_TK = 256


def _params(*sem):
    return pltpu.CompilerParams(dimension_semantics=sem, vmem_limit_bytes=VMEM_LIMIT_BYTES)


def _mem_kv_kernel(mem_ref, g_ref, w_ref, kbd_ref, vobd_ref):
    m = mem_ref[...]
    r = lax.rsqrt(jnp.mean(m * m, axis=-1, keepdims=True) + EPS)
    mn = (m * r * g_ref[0]).astype(jnp.bfloat16)
    kv = jnp.dot(mn, w_ref[0], preferred_element_type=jnp.float32)
    km = kv[:, :MEM_WIDTH] * (MEM_HEAD_DIM ** -0.5)
    vm = kv[:, MEM_WIDTH:]
    head_of_lane = lax.broadcasted_iota(jnp.int32, (MEM_LEN, MEM_WIDTH), 1) // MEM_HEAD_DIM
    for h in range(MEM_HEADS):
        sel = head_of_lane == h
        rows = pl.ds(h * MEM_LEN, MEM_LEN)
        kbd_ref[0, rows, :] = jnp.where(sel, km, 0.0).astype(jnp.bfloat16)
        vobd_ref[0, rows, :MEM_WIDTH] = jnp.where(sel, vm, 0.0).astype(jnp.bfloat16)
        vobd_ref[0, rows, MEM_WIDTH:] = jnp.where(sel, 1.0, 0.0).astype(jnp.bfloat16)


def _mem_kv(mem, gains, weights):
    n_layers = gains.shape[0]
    rows = MEM_HEADS * MEM_LEN
    return pl.pallas_call(
        _mem_kv_kernel,
        grid=(n_layers,),
        in_specs=[
            pl.BlockSpec((MEM_LEN, D_MODEL), lambda l: (0, 0)),
            pl.BlockSpec((1, 1, D_MODEL), lambda l: (l, 0, 0)),
            pl.BlockSpec((1, D_MODEL, 2 * MEM_WIDTH), lambda l: (l, 0, 0)),
        ],
        out_specs=[
            pl.BlockSpec((1, rows, MEM_WIDTH), lambda l: (l, 0, 0)),
            pl.BlockSpec((1, rows, 2 * MEM_WIDTH), lambda l: (l, 0, 0)),
        ],
        out_shape=[
            jax.ShapeDtypeStruct((n_layers, rows, MEM_WIDTH), jnp.bfloat16),
            jax.ShapeDtypeStruct((n_layers, rows, 2 * MEM_WIDTH), jnp.bfloat16),
        ],
        compiler_params=_params("arbitrary"),
        name="mem_kv",
    )(mem, gains, weights)


def _proj_kernel(x_ref, g_ref, w_ref, o_ref):
    x = x_ref[...]
    r = lax.rsqrt(jnp.mean(x * x, axis=-1, keepdims=True) + EPS)
    h = (x * r * g_ref[...]).astype(jnp.bfloat16)
    o_ref[...] = jnp.dot(h, w_ref[...], preferred_element_type=jnp.float32).astype(o_ref.dtype)


def _norm_proj(x, gain, w):
    s, d = x.shape
    c = w.shape[1]
    n_col = 2
    tn = c // n_col
    assert tn * n_col == c and tn % LANES == 0 and s % PROJ_TM == 0
    return pl.pallas_call(
        _proj_kernel,
        grid=(n_col, s // PROJ_TM),
        in_specs=[
            pl.BlockSpec((PROJ_TM, d), lambda j, i: (i, 0)),
            pl.BlockSpec((1, d), lambda j, i: (0, 0)),
            pl.BlockSpec((d, tn), lambda j, i: (0, j)),
        ],
        out_specs=pl.BlockSpec((PROJ_TM, tn), lambda j, i: (i, j)),
        out_shape=jax.ShapeDtypeStruct((s, c), jnp.bfloat16),
        compiler_params=_params("arbitrary", "arbitrary"),
        name="norm_proj",
    )(x, gain, w)


def _diff_attn_kernel(q_ref, k_ref, v_ref, posq_ref, posk_ref, slope_ref, lq1_ref, lk1_ref, lq2_ref,
                      lk2_ref, subln_ref, o_ref, m_ref, acc_ref, *, lam_init):
    tq, tk = DIFF_TQ, DIFF_TK
    i = pl.program_id(1)
    d = DIFF_HEAD_DIM

    q = q_ref[...]
    lane = lax.broadcasted_iota(jnp.int32, q.shape, 1)
    zero = jnp.zeros_like(q)
    qq = jnp.concatenate([jnp.where(lane < d, q, zero), jnp.where(lane < d, zero, q)], axis=0)
    qq = qq * jnp.asarray(d ** -0.5, dtype=qq.dtype)

    slope = slope_ref[0, 0:1, :]
    q_start = pl.multiple_of(i * tq, tq)
    p0 = posq_ref[0][:, 0:1]
    ones = jnp.ones((tk, LANES), dtype=jnp.bfloat16)

    m_ref[...] = jnp.full(m_ref.shape, NEG_INF, dtype=jnp.float32)
    acc_ref[...] = jnp.zeros(acc_ref.shape, dtype=jnp.float32)

    def block(k_start, mask):
        kb = k_ref[pl.ds(k_start, tk), :]
        vb = v_ref[pl.ds(k_start, tk), :]
        pk = posk_ref[k_start // tk]
        s = lax.dot_general(qq, kb, (((1,), (1,)), ((), ())), preferred_element_type=jnp.float32)
        s = s + jnp.concatenate([slope] * (tk // LANES), axis=1) * (pk - p0)
        if mask is not None:
            s = jnp.where(mask, s, NEG_INF)
        m_prev = m_ref[...]
        m_new = jnp.maximum(m_prev, jnp.max(s, axis=1, keepdims=True))
        alpha = jnp.exp(m_prev - m_new)
        p = jnp.exp(s - jnp.concatenate([m_new] * (tk // LANES), axis=1))
        pv = jnp.dot(p.astype(jnp.bfloat16), jnp.concatenate([vb, ones], axis=1),
                     preferred_element_type=jnp.float32)
        acc_ref[...] = acc_ref[...] * jnp.concatenate([alpha, alpha], axis=1) + pv
        m_ref[...] = m_new

    n_full = (i * tq) // tk

    def body(j, carry):
        block(pl.multiple_of(j * tk, tk), None)
        return carry

    lax.fori_loop(0, n_full, body, 0)

    k_start = pl.multiple_of(n_full * tk, tk)
    row = lax.broadcasted_iota(jnp.int32, (2 * tq, tk), 0)
    row = jnp.where(row >= tq, row - tq, row) + (q_start - k_start)
    col = lax.broadcasted_iota(jnp.int32, (2 * tq, tk), 1)
    block(k_start, col <= row)

    lam = (jnp.exp(jnp.sum(lq1_ref[...] * lk1_ref[...], axis=1, keepdims=True))
           - jnp.exp(jnp.sum(lq2_ref[...] * lk2_ref[...], axis=1, keepdims=True)) + lam_init)
    acc = acc_ref[...]
    o = acc[:, :LANES] / acc[:, LANES:]
    o = o[:tq] - lam * o[tq:]
    r = lax.rsqrt(jnp.mean(o * o, axis=-1, keepdims=True) + EPS)
    o_ref[...] = (o * r * subln_ref[...] * (1.0 - lam_init)).astype(o_ref.dtype)


def _diff_attention(proj, pos_row, slopes, lq1, lk1, lq2, lk2, subln, lam_init):
    s = proj.shape[0]
    tq, tk = DIFF_TQ, DIFF_TK
    assert tk % tq == 0 and s % tk == 0
    hb = BRANCH // LANES
    vec = lambda n: pl.BlockSpec((1, n), lambda h, i: (0, 0))
    return pl.pallas_call(
        functools.partial(_diff_attn_kernel, lam_init=lam_init),
        grid=(DIFF_HEADS, s // tq),
        in_specs=[
            pl.BlockSpec((tq, LANES), lambda h, i: (i, h)),
            pl.BlockSpec((s, LANES), lambda h, i: (0, hb + h)),
            pl.BlockSpec((s, LANES), lambda h, i: (0, 2 * hb + h)),
            pl.BlockSpec((1, 1, tq), lambda h, i: (i, 0, 0)),
            pl.BlockSpec((s // tk, 1, tk), lambda h, i: (0, 0, 0)),
            pl.BlockSpec((1, 8, LANES), lambda h, i: (h, 0, 0)),
            vec(DIFF_HEAD_DIM), vec(DIFF_HEAD_DIM), vec(DIFF_HEAD_DIM), vec(DIFF_HEAD_DIM),
            vec(2 * DIFF_HEAD_DIM),
        ],
        out_specs=pl.BlockSpec((tq, LANES), lambda h, i: (i, h)),
        out_shape=jax.ShapeDtypeStruct((s, BRANCH), jnp.bfloat16),
        scratch_shapes=[
            pltpu.VMEM((2 * tq, LANES), jnp.float32),
            pltpu.VMEM((2 * tq, 2 * LANES), jnp.float32),
        ],
        compiler_params=_params("arbitrary", "arbitrary"),
        name="diff_attention",
    )(proj, proj, proj, pos_row.reshape(s // tq, 1, tq), pos_row.reshape(s // tk, 1, tk), slopes,
      lq1, lk1, lq2, lk2, subln)


def _swap_halves(x):
    half = LANES // 2
    return jnp.concatenate([x[:, half:], x[:, :half]], axis=1)


def _swa_kernel(q_ref, kc_ref, kp_ref, vc_ref, vp_ref, pq_ref, pkc_ref, pkp_ref, slope_ref,
                sink_ref, o_ref):
    i = pl.program_id(0)
    w = WINDOW
    d = SWA_HEAD_DIM
    rel = jnp.concatenate([pq_ref[...] - pkp_ref[...], pq_ref[...] - pkc_ref[...]], axis=1)
    col = lax.broadcasted_iota(jnp.int32, (w, 2 * w), 1)
    valid = (rel >= 0.0) & (rel < float(w)) & ((col >= w) | (i > 0))
    lane_half = lax.broadcasted_iota(jnp.int32, (2 * w, LANES), 1) // d
    out_half = lax.broadcasted_iota(jnp.int32, (w, LANES), 1) // d
    scale = jnp.asarray(d ** -0.5, dtype=jnp.bfloat16)

    for pair in range(SWA_KV_HEADS // 2):
        lanes = slice(pair * LANES, (pair + 1) * LANES)
        kk = jnp.concatenate([kp_ref[:, lanes], kc_ref[:, lanes]], axis=0) * scale
        vv = jnp.concatenate([vp_ref[:, lanes], vc_ref[:, lanes]], axis=0)
        kk_sw = _swap_halves(kk)
        vv_sw = _swap_halves(vv)
        for e in range(2):
            c = 2 * pair + e
            zero = jnp.zeros_like(kk)
            rhs = (jnp.where(lane_half == e, kk, zero), jnp.where(lane_half == e, zero, kk_sw))
            vals = (vv, vv_sw)
            outs = []
            for u in range(SWA_GROUP // 2):
                qcol = q_ref[:, pl.ds((c * (SWA_GROUP // 2) + u) * LANES, LANES)]
                res = []
                for t in range(2):
                    head = c * SWA_GROUP + 2 * u + (e if t == 0 else 1 - e)
                    s = lax.dot_general(qcol, rhs[t], (((1,), (1,)), ((), ())),
                                        preferred_element_type=jnp.float32)
                    s = s - slope_ref[head][:, 0:1] * rel
                    s = jnp.where(valid, s, NEG_INF)
                    sink = sink_ref[head][:, 0:1]
                    m = jnp.maximum(jnp.max(s, axis=1, keepdims=True), sink)
                    ex = jnp.exp(s - m)
                    den = jnp.sum(ex, axis=1, keepdims=True) + jnp.exp(sink - m)
                    p = (ex / den).astype(jnp.bfloat16)
                    res.append(jnp.dot(p, vals[t], preferred_element_type=jnp.float32))
                outs.append(jnp.where(out_half == e, res[0], res[1]))
            base = c * (SWA_GROUP // 2) * LANES
            o_ref[:, pl.ds(base, (SWA_GROUP // 2) * LANES)] = jnp.concatenate(outs, axis=1).astype(o_ref.dtype)


def _swa_attention(proj, pos_col, pos_row, slopes, sinks):
    s = proj.shape[0]
    w = WINDOW
    kvw = SWA_KV_HEADS * SWA_HEAD_DIM
    kblk = (2 * BRANCH) // kvw
    prev = lambda i: jnp.maximum(i - 1, 0)
    return pl.pallas_call(
        _swa_kernel,
        grid=(s // w,),
        in_specs=[
            pl.BlockSpec((w, BRANCH), lambda i: (i, 0)),
            pl.BlockSpec((w, kvw), lambda i: (i, kblk)),
            pl.BlockSpec((w, kvw), lambda i: (prev(i), kblk)),
            pl.BlockSpec((w, kvw), lambda i: (i, kblk + 1)),
            pl.BlockSpec((w, kvw), lambda i: (prev(i), kblk + 1)),
            pl.BlockSpec((w, LANES), lambda i: (i, 0)),
            pl.BlockSpec((1, w), lambda i: (0, i)),
            pl.BlockSpec((1, w), lambda i: (0, prev(i))),
            pl.BlockSpec((SWA_Q_HEADS, 1, LANES), lambda i: (0, 0, 0)),
            pl.BlockSpec((SWA_Q_HEADS, 1, LANES), lambda i: (0, 0, 0)),
        ],
        out_specs=pl.BlockSpec((w, BRANCH), lambda i: (i, 0)),
        out_shape=jax.ShapeDtypeStruct((s, BRANCH), jnp.bfloat16),
        compiler_params=_params("arbitrary"),
        name="swa_attention",
    )(proj, proj, proj, proj, proj, pos_col, pos_row, pos_row, slopes, sinks)


def _silu(g):
    return g / (1.0 + jnp.exp(-g))


def _out_kernel(*refs, conv):
    if conv:
        (x_ref, bg_ref, cg_ref, u_ref, hc_ref, hu_ref, cw_ref,
         gmix_ref, qm_ref, gmem_ref, kbd_ref, vobd_ref, wo_ref, gpost_ref, o_ref) = refs
        i = pl.program_id(0)
        z = cg_ref[...].astype(jnp.float32) * u_ref[...].astype(jnp.float32)
        zh = hc_ref[...].astype(jnp.float32) * hu_ref[...].astype(jnp.float32)
        zh = zh * (i > 0).astype(jnp.float32)
        row = lax.broadcasted_iota(jnp.int32, z.shape, 0)
        z1 = jnp.where(row == 0, zh[7:8], pltpu.roll(z, 1, 0))
        z2 = jnp.where(row == 0, zh[6:7], jnp.where(row == 1, zh[7:8], pltpu.roll(z, 2, 0)))
        cw = cw_ref[...]
        mix = bg_ref[...].astype(jnp.float32) * (cw[0:1] * z2 + cw[1:2] * z1 + cw[2:3] * z)
    else:
        (x_ref, mix_ref, gmix_ref, qm_ref, gmem_ref, kbd_ref, vobd_ref, wo_ref, gpost_ref,
         o_ref) = refs
        mix = mix_ref[...].astype(jnp.float32)

    s = lax.dot_general(qm_ref[...], kbd_ref[0], (((1,), (1,)), ((), ())),
                        preferred_element_type=jnp.float32)
    ps = []
    for h in range(MEM_HEADS):
        sh = s[:, h * MEM_LEN:(h + 1) * MEM_LEN]
        ps.append(jnp.exp(sh - jnp.max(sh, axis=1, keepdims=True)).astype(jnp.bfloat16))
    nd = jnp.dot(jnp.concatenate(ps, axis=1), vobd_ref[0], preferred_element_type=jnp.float32)
    mem_out = nd[:, :MEM_WIDTH] / nd[:, MEM_WIDTH:]

    y_mix = (mix * _silu(gmix_ref[...].astype(jnp.float32))).astype(jnp.bfloat16)
    y_mem = (mem_out * _silu(gmem_ref[...].astype(jnp.float32))).astype(jnp.bfloat16)
    y = (jnp.dot(y_mix, wo_ref[:BRANCH, :], preferred_element_type=jnp.float32)
         + jnp.dot(y_mem, wo_ref[BRANCH:, :], preferred_element_type=jnp.float32))
    r = lax.rsqrt(jnp.mean(y * y, axis=-1, keepdims=True) + EPS)
    o_ref[...] = x_ref[...] + y * r * gpost_ref[...]


def _out_layer(x, proj, mix, conv_w, kbd, vobd, layer, w_out, g_post, cols):
    s, d = x.shape
    tm = OUT_TM
    conv = mix is None
    row_blk = lambda width, off: pl.BlockSpec((tm, width), lambda i: (i, off // width))
    in_specs = [pl.BlockSpec((tm, d), lambda i: (i, 0))]
    args = [x]
    if conv:
        halo = lambda off: pl.BlockSpec(
            (8, BRANCH), lambda i: (jnp.maximum(i * (tm // 8) - 1, 0), off // BRANCH))
        in_specs += [row_blk(BRANCH, cols["a"]), row_blk(BRANCH, cols["b"]), row_blk(BRANCH, cols["c"]),
                     halo(cols["b"]), halo(cols["c"]),
                     pl.BlockSpec((CONV_WIDTH, BRANCH), lambda i: (0, 0))]
        args += [proj, proj, proj, proj, proj, conv_w]
    else:
        in_specs += [pl.BlockSpec((tm, BRANCH), lambda i: (i, 0))]
        args += [mix]
    in_specs += [
        row_blk(BRANCH, cols["gate_mix"]), row_blk(MEM_WIDTH, cols["q_mem"]),
        row_blk(MEM_WIDTH, cols["gate_mem"]),
        pl.BlockSpec((1,) + kbd.shape[1:], lambda i: (layer, 0, 0)),
        pl.BlockSpec((1,) + vobd.shape[1:], lambda i: (layer, 0, 0)),
        pl.BlockSpec((GATE_WIDTH, d), lambda i: (0, 0)),
        pl.BlockSpec((1, d), lambda i: (0, 0)),
    ]
    args += [proj, proj, proj, kbd, vobd, w_out, g_post]
    return pl.pallas_call(
        functools.partial(_out_kernel, conv=conv),
        grid=(s // tm,),
        in_specs=in_specs,
        out_specs=pl.BlockSpec((tm, d), lambda i: (i, 0)),
        out_shape=jax.ShapeDtypeStruct((s, d), jnp.float32),
        compiler_params=_params("arbitrary"),
        name="out_conv" if conv else "out_attn",
    )(*args)


def _relayout_w_in(w, kind):
    if kind == 2:
        q, k, v, q_mem, gate = jnp.split(w, [2048, 2304, 2560, 2816], axis=1)
        parts = [q, gate[:, :BRANCH], k, v, q_mem, gate[:, BRANCH:]]
        cols = dict(a=0, gate_mix=2048, k=4096, v=4352, q_mem=4608, gate_mem=4864)
    else:
        a, b, c, q_mem, gate = jnp.split(w, [2048, 4096, 6144, 6400], axis=1)
        parts = [a, b, c, gate[:, :BRANCH], q_mem, gate[:, BRANCH:]]
        cols = dict(a=0, b=2048, c=4096, gate_mix=6144, q_mem=8192, gate_mem=8448)
    return jnp.concatenate(parts, axis=1).astype(jnp.bfloat16), cols


def _alibi_slopes(n_heads):
    return 2.0 ** (-ALIBI_MAX_BIAS * jnp.arange(1, n_heads + 1, dtype=jnp.float32) / n_heads)


def kernel(x, mem, positions, norm_pre_0, norm_post_0, norm_mem_0, w_in_0, w_mem_kv_0, conv_w_0, w_out_0, norm_pre_1, norm_post_1, norm_mem_1, w_in_1, w_mem_kv_1, lambda_q1_1, lambda_k1_1, lambda_q2_1, lambda_k2_1, subln_1, w_out_1, norm_pre_2, norm_post_2, norm_mem_2, w_in_2, w_mem_kv_2, sinks_2, w_out_2, norm_pre_3, norm_post_3, norm_mem_3, w_in_3, w_mem_kv_3, conv_w_3, w_out_3):
    b, s, d = x.shape
    assert b == 1 and s == SEQ and d == D_MODEL
    xs = x.reshape(s, d)
    pos_f = positions.reshape(s).astype(jnp.float32)
    pos_row = pos_f.reshape(1, s)
    pos_col = jnp.broadcast_to(pos_f[:, None], (s, LANES))

    pre = [norm_pre_0, norm_pre_1, norm_pre_2, norm_pre_3]
    post = [norm_post_0, norm_post_1, norm_post_2, norm_post_3]
    w_in = [w_in_0, w_in_1, w_in_2, w_in_3]
    w_out = [w_out_0, w_out_1, w_out_2, w_out_3]
    conv_w = {0: conv_w_0, 3: conv_w_3}

    mem_gain = jnp.stack([norm_mem_0, norm_mem_1, norm_mem_2, norm_mem_3]).reshape(DEPTH, 1, d)
    mem_w = jnp.stack([w_mem_kv_0, w_mem_kv_1, w_mem_kv_2, w_mem_kv_3]).astype(jnp.bfloat16)
    kbd, vobd = _mem_kv(mem.reshape(MEM_LEN, d), mem_gain, mem_w)

    for layer in range(DEPTH):
        kind = layer % 3
        w, cols = _relayout_w_in(w_in[layer], kind)
        proj = _norm_proj(xs, pre[layer].reshape(1, d), w)
        if kind == 0:
            mix = None
        elif kind == 1:
            slopes = jnp.broadcast_to(_alibi_slopes(DIFF_HEADS)[:, None, None], (DIFF_HEADS, 8, LANES))
            lam_init = 0.8 - 0.6 * math.exp(-0.3 * layer)
            mix = _diff_attention(
                proj, pos_row, slopes,
                lambda_q1_1.reshape(1, -1), lambda_k1_1.reshape(1, -1),
                lambda_q2_1.reshape(1, -1), lambda_k2_1.reshape(1, -1),
                subln_1.reshape(1, -1), lam_init)
        else:
            slopes = jnp.broadcast_to(_alibi_slopes(SWA_Q_HEADS)[:, None, None], (SWA_Q_HEADS, 1, LANES))
            sinks = jnp.broadcast_to(sinks_2.astype(jnp.float32)[:, None, None], (SWA_Q_HEADS, 1, LANES))
            mix = _swa_attention(proj, pos_col, pos_row, slopes, sinks)
        xs = _out_layer(xs, proj, mix, conv_w.get(layer), kbd, vobd, layer,
                        w_out[layer].astype(jnp.bfloat16), post[layer].reshape(1, d), cols)
    return xs.reshape(b, s, d)
```

```python
import functools
import math

import jax
import jax.numpy as jnp
from jax import lax
from jax.experimental import pallas as pl
from jax.experimental.pallas import tpu as pltpu

D_MODEL = 1024
SEQ = 16384
DEPTH = 4
MEM_LEN = 256
BRANCH = 2048
CONV_WIDTH = 3
DIFF_HEAD_DIM = 64
DIFF_HEADS = 16
SWA_HEAD_DIM = 64
SWA_Q_HEADS = 32
SWA_KV_HEADS = 4
SWA_GROUP = SWA_Q_HEADS // SWA_KV_HEADS
WINDOW = 128
MEM_HEADS = 4
MEM_HEAD_DIM = 64
MEM_WIDTH = 256
GATE_WIDTH = BRANCH + MEM_WIDTH
ALIBI_MAX_BIAS = 8.0
EPS = 1e-6
NEG_INF = -1e30

LANES = 128
VMEM_LIMIT_BYTES = 56 * 1024 * 1024

PROJ_TM = 512
OUT_TM = 256
DIFF_TQ = 256
DIFF_TK = 256
DIFF_UNROLL = 4
LOG2E = math.log2(math.e)


def _params(*sem):
    return pltpu.CompilerParams(dimension_semantics=sem, vmem_limit_bytes=VMEM_LIMIT_BYTES)


def _mem_kv_kernel(mem_ref, g_ref, w_ref, kbd_ref, vobd_ref):
    m = mem_ref[...]
    r = lax.rsqrt(jnp.mean(m * m, axis=-1, keepdims=True) + EPS)
    mn = (m * r * g_ref[0]).astype(jnp.bfloat16)
    kv = jnp.dot(mn, w_ref[0], preferred_element_type=jnp.float32)
    km = kv[:, :MEM_WIDTH] * (MEM_HEAD_DIM ** -0.5)
    vm = kv[:, MEM_WIDTH:]
    head_of_lane = lax.broadcasted_iota(jnp.int32, (MEM_LEN, MEM_WIDTH), 1) // MEM_HEAD_DIM
    for h in range(MEM_HEADS):
        sel = head_of_lane == h
        rows = pl.ds(h * MEM_LEN, MEM_LEN)
        kbd_ref[0, rows, :] = jnp.where(sel, km, 0.0).astype(jnp.bfloat16)
        vobd_ref[0, rows, :MEM_WIDTH] = jnp.where(sel, vm, 0.0).astype(jnp.bfloat16)
        vobd_ref[0, rows, MEM_WIDTH:] = jnp.where(sel, 1.0, 0.0).astype(jnp.bfloat16)


def _mem_kv(mem, gains, weights):
    n_layers = gains.shape[0]
    rows = MEM_HEADS * MEM_LEN
    return pl.pallas_call(
        _mem_kv_kernel,
        grid=(n_layers,),
        in_specs=[
            pl.BlockSpec((MEM_LEN, D_MODEL), lambda l: (0, 0)),
            pl.BlockSpec((1, 1, D_MODEL), lambda l: (l, 0, 0)),
            pl.BlockSpec((1, D_MODEL, 2 * MEM_WIDTH), lambda l: (l, 0, 0)),
        ],
        out_specs=[
            pl.BlockSpec((1, rows, MEM_WIDTH), lambda l: (l, 0, 0)),
            pl.BlockSpec((1, rows, 2 * MEM_WIDTH), lambda l: (l, 0, 0)),
        ],
        out_shape=[
            jax.ShapeDtypeStruct((n_layers, rows, MEM_WIDTH), jnp.bfloat16),
            jax.ShapeDtypeStruct((n_layers, rows, 2 * MEM_WIDTH), jnp.bfloat16),
        ],
        compiler_params=_params("arbitrary"),
        name="mem_kv",
    )(mem, gains, weights)


def _proj_kernel(x_ref, g_ref, w_ref, o_ref):
    x = x_ref[...]
    r = lax.rsqrt(jnp.mean(x * x, axis=-1, keepdims=True) + EPS)
    h = (x * r * g_ref[...]).astype(jnp.bfloat16)
    o_ref[...] = jnp.dot(h, w_ref[...], preferred_element_type=jnp.float32).astype(o_ref.dtype)


def _norm_proj(x, gain, w):
    s, d = x.shape
    c = w.shape[1]
    n_col = 2
    tn = c // n_col
    assert tn * n_col == c and tn % LANES == 0 and s % PROJ_TM == 0
    return pl.pallas_call(
        _proj_kernel,
        grid=(n_col, s // PROJ_TM),
        in_specs=[
            pl.BlockSpec((PROJ_TM, d), lambda j, i: (i, 0)),
            pl.BlockSpec((1, d), lambda j, i: (0, 0)),
            pl.BlockSpec((d, tn), lambda j, i: (0, j)),
        ],
        out_specs=pl.BlockSpec((PROJ_TM, tn), lambda j, i: (i, j)),
        out_shape=jax.ShapeDtypeStruct((s, c), jnp.bfloat16),
        compiler_params=_params("arbitrary", "arbitrary"),
        name="norm_proj",
    )(x, gain, w)


def _diff_attn_kernel(q_ref, k_ref, v_ref, posq_ref, posk_ref, slope_ref, lq1_ref, lk1_ref, lq2_ref,
                      lk2_ref, subln_ref, o_ref, m_ref, acc_ref, s_ref, p_ref, alpha_ref, *, lam_init):
    tq, tk = DIFF_TQ, DIFF_TK
    i = pl.program_id(1)
    d = DIFF_HEAD_DIM
    last_block = SEQ // tk - 1
    reps = tk // LANES

    q = q_ref[...].astype(jnp.float32) * (d ** -0.5 * LOG2E)
    lane = lax.broadcasted_iota(jnp.int32, q.shape, 1)
    qq = jnp.concatenate([jnp.where(lane < d, q, 0.0), jnp.where(lane < d, 0.0, q)],
                         axis=0).astype(jnp.bfloat16)

    slope_row = jnp.concatenate([slope_ref[0, 0:1, :] * LOG2E] * reps, axis=1)
    p0 = posq_ref[0][:, 0:1]
    ones = jnp.ones((tk, LANES), dtype=jnp.bfloat16)
    n_full = (i * tq) // tk

    m_ref[...] = jnp.full(m_ref.shape, NEG_INF, dtype=jnp.float32)
    acc_ref[...] = jnp.zeros(acc_ref.shape, dtype=jnp.float32)
    p_ref[1] = jnp.zeros(p_ref.shape[1:], dtype=p_ref.dtype)
    alpha_ref[1] = jnp.ones(alpha_ref.shape[1:], dtype=jnp.float32)

    def logits(t, extra):
        kb = k_ref[pl.ds(pl.multiple_of(t * tk, tk), tk), :]
        s = lax.dot_general(qq, kb, (((1,), (1,)), ((), ())), preferred_element_type=jnp.float32)
        return s + (slope_row * (posk_ref[t] - p0) + extra)

    def softmax_step(s):
        m_prev = m_ref[...]
        m_new = jnp.maximum(m_prev, jnp.max(s, axis=1, keepdims=True))
        m_ref[...] = m_new
        p = jnp.exp2(s - jnp.concatenate([m_new] * reps, axis=1)).astype(jnp.bfloat16)
        return p, jnp.exp2(m_prev - m_new)

    def accumulate(t, p, alpha):
        vb = v_ref[pl.ds(pl.multiple_of(t * tk, tk), tk), :]
        pv = jnp.dot(p, jnp.concatenate([vb, ones], axis=1), preferred_element_type=jnp.float32)
        acc_ref[...] = acc_ref[...] * jnp.concatenate([alpha, alpha], axis=1) + pv

    row = lax.broadcasted_iota(jnp.int32, (2 * tq, tk), 0)
    row = jnp.where(row >= tq, row - tq, row) + (i * tq - n_full * tk)
    col = lax.broadcasted_iota(jnp.int32, (2 * tq, tk), 1)
    s_ref[0] = jnp.where(col <= row, logits(n_full, 0.0), NEG_INF)

    def stage_a(b, buf):
        off = jnp.where(b <= n_full, 0.0, NEG_INF)
        s_ref[buf] = logits(jnp.minimum(b - 1, last_block), off)

    def stage_b(buf):
        p, alpha = softmax_step(s_ref[buf])
        p_ref[buf] = p
        alpha_ref[buf] = alpha

    def stage_c(b, buf):
        t = jnp.where(b == 0, n_full, b - 1)
        accumulate(jnp.clip(t, 0, last_block), p_ref[buf], alpha_ref[buf])

    def body(u, carry):
        for r in range(DIFF_UNROLL):
            b = DIFF_UNROLL * u + r + 1
            stage_c(b - 2, (r + 1) % 2)
            stage_b(r % 2)
            stage_a(b, (r + 1) % 2)
        return carry

    lax.fori_loop(0, (n_full + 1 + DIFF_UNROLL) // DIFF_UNROLL, body, 0)

    lam = (jnp.exp(jnp.sum(lq1_ref[...] * lk1_ref[...], axis=1, keepdims=True))
           - jnp.exp(jnp.sum(lq2_ref[...] * lk2_ref[...], axis=1, keepdims=True)) + lam_init)
    acc = acc_ref[...]
    o = acc[:, :LANES] / acc[:, LANES:]
    o = o[:tq] - lam * o[tq:]
    r = lax.rsqrt(jnp.mean(o * o, axis=-1, keepdims=True) + EPS)
    o_ref[...] = (o * r * subln_ref[...] * (1.0 - lam_init)).astype(o_ref.dtype)


def _diff_attention(proj, pos_row, slopes, lq1, lk1, lq2, lk2, subln, lam_init):
    s = proj.shape[0]
    tq, tk = DIFF_TQ, DIFF_TK
    assert tk % tq == 0 and s % tk == 0
    hb = BRANCH // LANES
    vec = lambda n: pl.BlockSpec((1, n), lambda h, i: (0, 0))
    return pl.pallas_call(
        functools.partial(_diff_attn_kernel, lam_init=lam_init),
        grid=(DIFF_HEADS, s // tq),
        in_specs=[
            pl.BlockSpec((tq, LANES), lambda h, i: (i, h)),
            pl.BlockSpec((s, LANES), lambda h, i: (0, hb + h)),
            pl.BlockSpec((s, LANES), lambda h, i: (0, 2 * hb + h)),
            pl.BlockSpec((1, 1, tq), lambda h, i: (i, 0, 0)),
            pl.BlockSpec((s // tk, 1, tk), lambda h, i: (0, 0, 0)),
            pl.BlockSpec((1, 8, LANES), lambda h, i: (h, 0, 0)),
            vec(DIFF_HEAD_DIM), vec(DIFF_HEAD_DIM), vec(DIFF_HEAD_DIM), vec(DIFF_HEAD_DIM),
            vec(2 * DIFF_HEAD_DIM),
        ],
        out_specs=pl.BlockSpec((tq, LANES), lambda h, i: (i, h)),
        out_shape=jax.ShapeDtypeStruct((s, BRANCH), jnp.bfloat16),
        scratch_shapes=[
            pltpu.VMEM((2 * tq, LANES), jnp.float32),
            pltpu.VMEM((2 * tq, 2 * LANES), jnp.float32),
            pltpu.VMEM((2, 2 * tq, tk), jnp.float32),
            pltpu.VMEM((2, 2 * tq, tk), jnp.bfloat16),
            pltpu.VMEM((2, 2 * tq, LANES), jnp.float32),
        ],
        compiler_params=_params("arbitrary", "arbitrary"),
        name="diff_attention",
    )(proj, proj, proj, pos_row.reshape(s // tq, 1, tq), pos_row.reshape(s // tk, 1, tk), slopes,
      lq1, lk1, lq2, lk2, subln)


def _swap_halves(x):
    half = LANES // 2
    return jnp.concatenate([x[:, half:], x[:, :half]], axis=1)


def _swa_kernel(q_ref, kc_ref, kp_ref, vc_ref, vp_ref, pq_ref, pkc_ref, pkp_ref, slope_ref,
                sink_ref, o_ref):
    i = pl.program_id(0)
    w = WINDOW
    d = SWA_HEAD_DIM
    rel = jnp.concatenate([pq_ref[...] - pkp_ref[...], pq_ref[...] - pkc_ref[...]], axis=1)
    col = lax.broadcasted_iota(jnp.int32, (w, 2 * w), 1)
    valid = (rel >= 0.0) & (rel < float(w)) & ((col >= w) | (i > 0))
    lane_half = lax.broadcasted_iota(jnp.int32, (2 * w, LANES), 1) // d
    out_half = lax.broadcasted_iota(jnp.int32, (w, LANES), 1) // d
    scale = jnp.asarray(d ** -0.5, dtype=jnp.bfloat16)

    for pair in range(SWA_KV_HEADS // 2):
        lanes = slice(pair * LANES, (pair + 1) * LANES)
        kk = jnp.concatenate([kp_ref[:, lanes], kc_ref[:, lanes]], axis=0) * scale
        vv = jnp.concatenate([vp_ref[:, lanes], vc_ref[:, lanes]], axis=0)
        kk_sw = _swap_halves(kk)
        vv_sw = _swap_halves(vv)
        for e in range(2):
            c = 2 * pair + e
            zero = jnp.zeros_like(kk)
            rhs = (jnp.where(lane_half == e, kk, zero), jnp.where(lane_half == e, zero, kk_sw))
            vals = (vv, vv_sw)
            outs = []
            for u in range(SWA_GROUP // 2):
                qcol = q_ref[:, pl.ds((c * (SWA_GROUP // 2) + u) * LANES, LANES)]
                res = []
                for t in range(2):
                    head = c * SWA_GROUP + 2 * u + (e if t == 0 else 1 - e)
                    s = lax.dot_general(qcol, rhs[t], (((1,), (1,)), ((), ())),
                                        preferred_element_type=jnp.float32)
                    s = s - slope_ref[head][:, 0:1] * rel
                    s = jnp.where(valid, s, NEG_INF)
                    sink = sink_ref[head][:, 0:1]
                    m = jnp.maximum(jnp.max(s, axis=1, keepdims=True), sink)
                    ex = jnp.exp(s - m)
                    den = jnp.sum(ex, axis=1, keepdims=True) + jnp.exp(sink - m)
                    p = (ex / den).astype(jnp.bfloat16)
                    res.append(jnp.dot(p, vals[t], preferred_element_type=jnp.float32))
                outs.append(jnp.where(out_half == e, res[0], res[1]))
            base = c * (SWA_GROUP // 2) * LANES
            o_ref[:, pl.ds(base, (SWA_GROUP // 2) * LANES)] = jnp.concatenate(outs, axis=1).astype(o_ref.dtype)


def _swa_attention(proj, pos_col, pos_row, slopes, sinks):
    s = proj.shape[0]
    w = WINDOW
    kvw = SWA_KV_HEADS * SWA_HEAD_DIM
    kblk = (2 * BRANCH) // kvw
    prev = lambda i: jnp.maximum(i - 1, 0)
    return pl.pallas_call(
        _swa_kernel,
        grid=(s // w,),
        in_specs=[
            pl.BlockSpec((w, BRANCH), lambda i: (i, 0)),
            pl.BlockSpec((w, kvw), lambda i: (i, kblk)),
            pl.BlockSpec((w, kvw), lambda i: (prev(i), kblk)),
            pl.BlockSpec((w, kvw), lambda i: (i, kblk + 1)),
            pl.BlockSpec((w, kvw), lambda i: (prev(i), kblk + 1)),
            pl.BlockSpec((w, LANES), lambda i: (i, 0)),
            pl.BlockSpec((1, w), lambda i: (0, i)),
            pl.BlockSpec((1, w), lambda i: (0, prev(i))),
            pl.BlockSpec((SWA_Q_HEADS, 1, LANES), lambda i: (0, 0, 0)),
            pl.BlockSpec((SWA_Q_HEADS, 1, LANES), lambda i: (0, 0, 0)),
        ],
        out_specs=pl.BlockSpec((w, BRANCH), lambda i: (i, 0)),
        out_shape=jax.ShapeDtypeStruct((s, BRANCH), jnp.bfloat16),
        compiler_params=_params("arbitrary"),
        name="swa_attention",
    )(proj, proj, proj, proj, proj, pos_col, pos_row, pos_row, slopes, sinks)


def _silu(g):
    return g / (1.0 + jnp.exp(-g))


def _out_kernel(*refs, conv):
    if conv:
        (x_ref, bg_ref, cg_ref, u_ref, hc_ref, hu_ref, cw_ref,
         gmix_ref, qm_ref, gmem_ref, kbd_ref, vobd_ref, wo_ref, gpost_ref, o_ref) = refs
        i = pl.program_id(0)
        z = cg_ref[...].astype(jnp.float32) * u_ref[...].astype(jnp.float32)
        zh = hc_ref[...].astype(jnp.float32) * hu_ref[...].astype(jnp.float32)
        zh = zh * (i > 0).astype(jnp.float32)
        row = lax.broadcasted_iota(jnp.int32, z.shape, 0)
        z1 = jnp.where(row == 0, zh[7:8], pltpu.roll(z, 1, 0))
        z2 = jnp.where(row == 0, zh[6:7], jnp.where(row == 1, zh[7:8], pltpu.roll(z, 2, 0)))
        cw = cw_ref[...]
        mix = bg_ref[...].astype(jnp.float32) * (cw[0:1] * z2 + cw[1:2] * z1 + cw[2:3] * z)
    else:
        (x_ref, mix_ref, gmix_ref, qm_ref, gmem_ref, kbd_ref, vobd_ref, wo_ref, gpost_ref,
         o_ref) = refs
        mix = mix_ref[...].astype(jnp.float32)

    s = lax.dot_general(qm_ref[...], kbd_ref[0], (((1,), (1,)), ((), ())),
                        preferred_element_type=jnp.float32)
    ps = []
    for h in range(MEM_HEADS):
        sh = s[:, h * MEM_LEN:(h + 1) * MEM_LEN]
        ps.append(jnp.exp(sh - jnp.max(sh, axis=1, keepdims=True)).astype(jnp.bfloat16))
    nd = jnp.dot(jnp.concatenate(ps, axis=1), vobd_ref[0], preferred_element_type=jnp.float32)
    mem_out = nd[:, :MEM_WIDTH] / nd[:, MEM_WIDTH:]

    y_mix = (mix * _silu(gmix_ref[...].astype(jnp.float32))).astype(jnp.bfloat16)
    y_mem = (mem_out * _silu(gmem_ref[...].astype(jnp.float32))).astype(jnp.bfloat16)
    y = (jnp.dot(y_mix, wo_ref[:BRANCH, :], preferred_element_type=jnp.float32)
         + jnp.dot(y_mem, wo_ref[BRANCH:, :], preferred_element_type=jnp.float32))
    r = lax.rsqrt(jnp.mean(y * y, axis=-1, keepdims=True) + EPS)
    o_ref[...] = x_ref[...] + y * r * gpost_ref[...]


def _out_layer(x, proj, mix, conv_w, kbd, vobd, layer, w_out, g_post, cols):
    s, d = x.shape
    tm = OUT_TM
    conv = mix is None
    row_blk = lambda width, off: pl.BlockSpec((tm, width), lambda i: (i, off // width))
    in_specs = [pl.BlockSpec((tm, d), lambda i: (i, 0))]
    args = [x]
    if conv:
        halo = lambda off: pl.BlockSpec(
            (8, BRANCH), lambda i: (jnp.maximum(i * (tm // 8) - 1, 0), off // BRANCH))
        in_specs += [row_blk(BRANCH, cols["a"]), row_blk(BRANCH, cols["b"]), row_blk(BRANCH, cols["c"]),
                     halo(cols["b"]), halo(cols["c"]),
                     pl.BlockSpec((CONV_WIDTH, BRANCH), lambda i: (0, 0))]
        args += [proj, proj, proj, proj, proj, conv_w]
    else:
        in_specs += [pl.BlockSpec((tm, BRANCH), lambda i: (i, 0))]
        args += [mix]
    in_specs += [
        row_blk(BRANCH, cols["gate_mix"]), row_blk(MEM_WIDTH, cols["q_mem"]),
        row_blk(MEM_WIDTH, cols["gate_mem"]),
        pl.BlockSpec((1,) + kbd.shape[1:], lambda i: (layer, 0, 0)),
        pl.BlockSpec((1,) + vobd.shape[1:], lambda i: (layer, 0, 0)),
        pl.BlockSpec((GATE_WIDTH, d), lambda i: (0, 0)),
        pl.BlockSpec((1, d), lambda i: (0, 0)),
    ]
    args += [proj, proj, proj, kbd, vobd, w_out, g_post]
    return pl.pallas_call(
        functools.partial(_out_kernel, conv=conv),
        grid=(s // tm,),
        in_specs=in_specs,
        out_specs=pl.BlockSpec((tm, d), lambda i: (i, 0)),
        out_shape=jax.ShapeDtypeStruct((s, d), jnp.float32),
        compiler_params=_params("arbitrary"),
        name="out_conv" if conv else "out_attn",
    )(*args)


def _relayout_w_in(w, kind):
    if kind == 2:
        q, k, v, q_mem, gate = jnp.split(w, [2048, 2304, 2560, 2816], axis=1)
        parts = [q, gate[:, :BRANCH], k, v, q_mem, gate[:, BRANCH:]]
        cols = dict(a=0, gate_mix=2048, k=4096, v=4352, q_mem=4608, gate_mem=4864)
    else:
        a, b, c, q_mem, gate = jnp.split(w, [2048, 4096, 6144, 6400], axis=1)
        parts = [a, b, c, gate[:, :BRANCH], q_mem, gate[:, BRANCH:]]
        cols = dict(a=0, b=2048, c=4096, gate_mix=6144, q_mem=8192, gate_mem=8448)
    return jnp.concatenate(parts, axis=1).astype(jnp.bfloat16), cols


def _alibi_slopes(n_heads):
    return 2.0 ** (-ALIBI_MAX_BIAS * jnp.arange(1, n_heads + 1, dtype=jnp.float32) / n_heads)


def kernel(x, mem, positions, norm_pre_0, norm_post_0, norm_mem_0, w_in_0, w_mem_kv_0, conv_w_0, w_out_0, norm_pre_1, norm_post_1, norm_mem_1, w_in_1, w_mem_kv_1, lambda_q1_1, lambda_k1_1, lambda_q2_1, lambda_k2_1, subln_1, w_out_1, norm_pre_2, norm_post_2, norm_mem_2, w_in_2, w_mem_kv_2, sinks_2, w_out_2, norm_pre_3, norm_post_3, norm_mem_3, w_in_3, w_mem_kv_3, conv_w_3, w_out_3):
    b, s, d = x.shape
    assert b == 1 and s == SEQ and d == D_MODEL
    xs = x.reshape(s, d)
    pos_f = positions.reshape(s).astype(jnp.float32)
    pos_row = pos_f.reshape(1, s)
    pos_col = jnp.broadcast_to(pos_f[:, None], (s, LANES))

    pre = [norm_pre_0, norm_pre_1, norm_pre_2, norm_pre_3]
    post = [norm_post_0, norm_post_1, norm_post_2, norm_post_3]
    w_in = [w_in_0, w_in_1, w_in_2, w_in_3]
    w_out = [w_out_0, w_out_1, w_out_2, w_out_3]
    conv_w = {0: conv_w_0, 3: conv_w_3}

    mem_gain = jnp.stack([norm_mem_0, norm_mem_1, norm_mem_2, norm_mem_3]).reshape(DEPTH, 1, d)
    mem_w = jnp.stack([w_mem_kv_0, w_mem_kv_1, w_mem_kv_2, w_mem_kv_3]).astype(jnp.bfloat16)
    kbd, vobd = _mem_kv(mem.reshape(MEM_LEN, d), mem_gain, mem_w)

    for layer in range(DEPTH):
        kind = layer % 3
        w, cols = _relayout_w_in(w_in[layer], kind)
        proj = _norm_proj(xs, pre[layer].reshape(1, d), w)
        if kind == 0:
            mix = None
        elif kind == 1:
            slopes = jnp.broadcast_to(_alibi_slopes(DIFF_HEADS)[:, None, None], (DIFF_HEADS, 8, LANES))
            lam_init = 0.8 - 0.6 * math.exp(-0.3 * layer)
            mix = _diff_attention(
                proj, pos_row, slopes,
                lambda_q1_1.reshape(1, -1), lambda_k1_1.reshape(1, -1),
                lambda_q2_1.reshape(1, -1), lambda_k2_1.reshape(1, -1),
                subln_1.reshape(1, -1), lam_init)
        else:
            slopes = jnp.broadcast_to(_alibi_slopes(SWA_Q_HEADS)[:, None, None], (SWA_Q_HEADS, 1, LANES))
            sinks = jnp.broadcast_to(sinks_2.astype(jnp.float32)[:, None, None], (SWA_Q_HEADS, 1, LANES))
            mix = _swa_attention(proj, pos_col, pos_row, slopes, sinks)
        xs = _out_layer(xs, proj, mix, conv_w.get(layer), kbd, vobd, layer,
                        w_out[layer].astype(jnp.bfloat16), post[layer].reshape(1, d), cols)
    return xs.reshape(b, s, d)
```

```python
import functools
import math

import jax
import jax.numpy as jnp
from jax import lax
from jax.experimental import pallas as pl
from jax.experimental.pallas import tpu as pltpu

D_MODEL = 1024
SEQ = 16384
DEPTH = 4
MEM_LEN = 256
BRANCH = 2048
CONV_WIDTH = 3
DIFF_HEAD_DIM = 64
DIFF_HEADS = 16
SWA_HEAD_DIM = 64
SWA_Q_HEADS = 32
SWA_KV_HEADS = 4
SWA_GROUP = SWA_Q_HEADS // SWA_KV_HEADS
WINDOW = 128
MEM_HEADS = 4
MEM_HEAD_DIM = 64
MEM_WIDTH = 256
GATE_WIDTH = BRANCH + MEM_WIDTH
ALIBI_MAX_BIAS = 8.0
EPS = 1e-6
NEG_INF = -1e30

LANES = 128
VMEM_LIMIT_BYTES = 56 * 1024 * 1024

PROJ_TM = 512
OUT_TM = 512
DIFF_TQ = 256
DIFF_TK = 256
DIFF_UNROLL = 8
LOG2E = math.log2(math.e)


def _params(*sem):
    return pltpu.CompilerParams(dimension_semantics=sem, vmem_limit_bytes=VMEM_LIMIT_BYTES)


def _mem_kv_kernel(mem_ref, g_ref, w_ref, kbd_ref, vobd_ref):
    m = mem_ref[...]
    r = lax.rsqrt(jnp.mean(m * m, axis=-1, keepdims=True) + EPS)
    mn = (m * r * g_ref[0]).astype(jnp.bfloat16)
    kv = jnp.dot(mn, w_ref[0], preferred_element_type=jnp.float32)
    km = kv[:, :MEM_WIDTH] * (MEM_HEAD_DIM ** -0.5)
    vm = kv[:, MEM_WIDTH:]
    head_of_lane = lax.broadcasted_iota(jnp.int32, (MEM_LEN, MEM_WIDTH), 1) // MEM_HEAD_DIM
    for h in range(MEM_HEADS):
        sel = head_of_lane == h
        rows = pl.ds(h * MEM_LEN, MEM_LEN)
        kbd_ref[0, rows, :] = jnp.where(sel, km, 0.0).astype(jnp.bfloat16)
        vobd_ref[0, rows, :MEM_WIDTH] = jnp.where(sel, vm, 0.0).astype(jnp.bfloat16)
        vobd_ref[0, rows, MEM_WIDTH:] = jnp.where(sel, 1.0, 0.0).astype(jnp.bfloat16)


def _mem_kv(mem, gains, weights):
    n_layers = gains.shape[0]
    rows = MEM_HEADS * MEM_LEN
    return pl.pallas_call(
        _mem_kv_kernel,
        grid=(n_layers,),
        in_specs=[
            pl.BlockSpec((MEM_LEN, D_MODEL), lambda l: (0, 0)),
            pl.BlockSpec((1, 1, D_MODEL), lambda l: (l, 0, 0)),
            pl.BlockSpec((1, D_MODEL, 2 * MEM_WIDTH), lambda l: (l, 0, 0)),
        ],
        out_specs=[
            pl.BlockSpec((1, rows, MEM_WIDTH), lambda l: (l, 0, 0)),
            pl.BlockSpec((1, rows, 2 * MEM_WIDTH), lambda l: (l, 0, 0)),
        ],
        out_shape=[
            jax.ShapeDtypeStruct((n_layers, rows, MEM_WIDTH), jnp.bfloat16),
            jax.ShapeDtypeStruct((n_layers, rows, 2 * MEM_WIDTH), jnp.bfloat16),
        ],
        compiler_params=_params("arbitrary"),
        name="mem_kv",
    )(mem, gains, weights)


def _proj_kernel(x_ref, g_ref, w_ref, o_ref):
    x = x_ref[...]
    r = lax.rsqrt(jnp.mean(x * x, axis=-1, keepdims=True) + EPS)
    h = (x * r * g_ref[...]).astype(jnp.bfloat16)
    o_ref[...] = jnp.dot(h, w_ref[...], preferred_element_type=jnp.float32).astype(o_ref.dtype)


def _norm_proj(x, gain, w):
    s, d = x.shape
    c = w.shape[1]
    n_col = 2
    tn = c // n_col
    assert tn * n_col == c and tn % LANES == 0 and s % PROJ_TM == 0
    return pl.pallas_call(
        _proj_kernel,
        grid=(n_col, s // PROJ_TM),
        in_specs=[
            pl.BlockSpec((PROJ_TM, d), lambda j, i: (i, 0)),
            pl.BlockSpec((1, d), lambda j, i: (0, 0)),
            pl.BlockSpec((d, tn), lambda j, i: (0, j)),
        ],
        out_specs=pl.BlockSpec((PROJ_TM, tn), lambda j, i: (i, j)),
        out_shape=jax.ShapeDtypeStruct((s, c), jnp.bfloat16),
        compiler_params=_params("arbitrary", "arbitrary"),
        name="norm_proj",
    )(x, gain, w)


def _diff_attn_kernel(q_ref, k_ref, v_ref, posq_ref, posk_ref, slope_ref, lq1_ref, lk1_ref, lq2_ref,
                      lk2_ref, subln_ref, o_ref, m_ref, acc_ref, s_ref, p_ref, alpha_ref, *, lam_init):
    tq, tk = DIFF_TQ, DIFF_TK
    i = pl.program_id(1)
    d = DIFF_HEAD_DIM
    last_block = SEQ // tk - 1
    reps = tk // LANES

    q = q_ref[...].astype(jnp.float32) * (d ** -0.5 * LOG2E)
    lane = lax.broadcasted_iota(jnp.int32, q.shape, 1)
    qq = jnp.concatenate([jnp.where(lane < d, q, 0.0), jnp.where(lane < d, 0.0, q)],
                         axis=0).astype(jnp.bfloat16)

    slope_row = jnp.concatenate([slope_ref[0, 0:1, :] * LOG2E] * reps, axis=1)
    p0 = posq_ref[0][:, 0:1]
    ones = jnp.ones((tk, LANES), dtype=jnp.bfloat16)
    n_full = (i * tq) // tk

    m_ref[...] = jnp.full(m_ref.shape, NEG_INF, dtype=jnp.float32)
    acc_ref[...] = jnp.zeros(acc_ref.shape, dtype=jnp.float32)
    p_ref[1] = jnp.zeros(p_ref.shape[1:], dtype=p_ref.dtype)
    alpha_ref[1] = jnp.ones(alpha_ref.shape[1:], dtype=jnp.float32)

    def logits(t, extra):
        kb = k_ref[pl.ds(pl.multiple_of(t * tk, tk), tk), :]
        s = lax.dot_general(qq, kb, (((1,), (1,)), ((), ())), preferred_element_type=jnp.float32)
        return s + (slope_row * (posk_ref[t] - p0) + extra)

    def softmax_step(s):
        m_prev = m_ref[...]
        m_new = jnp.maximum(m_prev, jnp.max(s, axis=1, keepdims=True))
        m_ref[...] = m_new
        p = jnp.exp2(s - jnp.concatenate([m_new] * reps, axis=1)).astype(jnp.bfloat16)
        return p, jnp.exp2(m_prev - m_new)

    def accumulate(t, p, alpha):
        vb = v_ref[pl.ds(pl.multiple_of(t * tk, tk), tk), :]
        pv = jnp.dot(p, jnp.concatenate([vb, ones], axis=1), preferred_element_type=jnp.float32)
        acc_ref[...] = acc_ref[...] * jnp.concatenate([alpha, alpha], axis=1) + pv

    row = lax.broadcasted_iota(jnp.int32, (2 * tq, tk), 0)
    row = jnp.where(row >= tq, row - tq, row) + (i * tq - n_full * tk)
    col = lax.broadcasted_iota(jnp.int32, (2 * tq, tk), 1)
    s_ref[0] = jnp.where(col <= row, logits(n_full, 0.0), NEG_INF)

    def stage_a(b, buf):
        off = jnp.where(b <= n_full, 0.0, NEG_INF)
        s_ref[buf] = logits(jnp.minimum(b - 1, last_block), off)

    def stage_b(buf):
        p, alpha = softmax_step(s_ref[buf])
        p_ref[buf] = p
        alpha_ref[buf] = alpha

    def stage_c(b, buf):
        t = jnp.where(b == 0, n_full, b - 1)
        accumulate(jnp.clip(t, 0, last_block), p_ref[buf], alpha_ref[buf])

    def run_steps(first, trips, unroll):
        def body(u, carry):
            for r in range(unroll):
                b = first + unroll * u + r + 1
                stage_c(b - 2, (r + 1) % 2)
                stage_b(r % 2)
                stage_a(b, (r + 1) % 2)
            return carry
        lax.fori_loop(0, trips, body, 0)

    steps = n_full + 2
    main_trips = steps // DIFF_UNROLL
    done = main_trips * DIFF_UNROLL
    run_steps(0, main_trips, DIFF_UNROLL)
    run_steps(done, (steps - done + 1) // 2, 2)

    lam = (jnp.exp(jnp.sum(lq1_ref[...] * lk1_ref[...], axis=1, keepdims=True))
           - jnp.exp(jnp.sum(lq2_ref[...] * lk2_ref[...], axis=1, keepdims=True)) + lam_init)
    acc = acc_ref[...]
    o = acc[:, :LANES] / acc[:, LANES:]
    o = o[:tq] - lam * o[tq:]
    r = lax.rsqrt(jnp.mean(o * o, axis=-1, keepdims=True) + EPS)
    o_ref[...] = (o * r * subln_ref[...] * (1.0 - lam_init)).astype(o_ref.dtype)


def _diff_attention(proj, pos_row, slopes, lq1, lk1, lq2, lk2, subln, lam_init):
    s = proj.shape[0]
    tq, tk = DIFF_TQ, DIFF_TK
    assert tk % tq == 0 and s % tk == 0
    hb = BRANCH // LANES
    vec = lambda n: pl.BlockSpec((1, n), lambda h, i: (0, 0))
    return pl.pallas_call(
        functools.partial(_diff_attn_kernel, lam_init=lam_init),
        grid=(DIFF_HEADS, s // tq),
        in_specs=[
            pl.BlockSpec((tq, LANES), lambda h, i: (i, h)),
            pl.BlockSpec((s, LANES), lambda h, i: (0, hb + h)),
            pl.BlockSpec((s, LANES), lambda h, i: (0, 2 * hb + h)),
            pl.BlockSpec((1, 1, tq), lambda h, i: (i, 0, 0)),
            pl.BlockSpec((s // tk, 1, tk), lambda h, i: (0, 0, 0)),
            pl.BlockSpec((1, 8, LANES), lambda h, i: (h, 0, 0)),
            vec(DIFF_HEAD_DIM), vec(DIFF_HEAD_DIM), vec(DIFF_HEAD_DIM), vec(DIFF_HEAD_DIM),
            vec(2 * DIFF_HEAD_DIM),
        ],
        out_specs=pl.BlockSpec((tq, LANES), lambda h, i: (i, h)),
        out_shape=jax.ShapeDtypeStruct((s, BRANCH), jnp.bfloat16),
        scratch_shapes=[
            pltpu.VMEM((2 * tq, LANES), jnp.float32),
            pltpu.VMEM((2 * tq, 2 * LANES), jnp.float32),
            pltpu.VMEM((2, 2 * tq, tk), jnp.float32),
            pltpu.VMEM((2, 2 * tq, tk), jnp.bfloat16),
            pltpu.VMEM((2, 2 * tq, LANES), jnp.float32),
        ],
        compiler_params=_params("arbitrary", "arbitrary"),
        name="diff_attention",
    )(proj, proj, proj, pos_row.reshape(s // tq, 1, tq), pos_row.reshape(s // tk, 1, tk), slopes,
      lq1, lk1, lq2, lk2, subln)


def _swap_halves(x):
    half = LANES // 2
    return jnp.concatenate([x[:, half:], x[:, :half]], axis=1)


def _swa_kernel(q_ref, kc_ref, kp_ref, vc_ref, vp_ref, pq_ref, pkc_ref, pkp_ref, slope_ref,
                sink_ref, o_ref):
    i = pl.program_id(0)
    w = WINDOW
    d = SWA_HEAD_DIM
    cols_per_kv = SWA_GROUP // 2
    rel = jnp.concatenate([pq_ref[...] - pkp_ref[...], pq_ref[...] - pkc_ref[...]], axis=1)
    col = lax.broadcasted_iota(jnp.int32, (w, 2 * w), 1)
    valid = (rel >= 0.0) & (rel < float(w)) & ((col >= w) | (i > 0))
    rel_masked = jnp.where(valid, rel, -NEG_INF)
    lane_half = lax.broadcasted_iota(jnp.int32, (2 * w, LANES), 1) // d
    out_half = lax.broadcasted_iota(jnp.int32, (w, LANES), 1) // d
    ones = jnp.ones((2 * w, LANES), dtype=jnp.bfloat16)

    pairs = []
    for pair in range(SWA_KV_HEADS // 2):
        lanes = slice(pair * LANES, (pair + 1) * LANES)
        kk = jnp.concatenate([kp_ref[:, lanes], kc_ref[:, lanes]], axis=0)
        kk = (kk.astype(jnp.float32) * (d ** -0.5 * LOG2E)).astype(jnp.bfloat16)
        vv = jnp.concatenate([vp_ref[:, lanes], vc_ref[:, lanes]], axis=0)
        pairs.append((kk, _swap_halves(kk), vv, _swap_halves(vv)))

    def operands(n):
        c, t = divmod(n, 2)
        pair, e = divmod(c, 2)
        kk, kk_sw, vv, vv_sw = pairs[pair]
        zero = jnp.zeros_like(kk)
        if t == 0:
            rhs, val = jnp.where(lane_half == e, kk, zero), vv
        else:
            rhs, val = jnp.where(lane_half == e, zero, kk_sw), vv_sw
        heads = [c * SWA_GROUP + 2 * u + (e if t == 0 else 1 - e) for u in range(cols_per_kv)]
        return c, e, rhs, val, heads

    def logits(n):
        c, _, rhs, _, _ = operands(n)
        lhs = jnp.concatenate(
            [q_ref[:, pl.ds((c * cols_per_kv + u) * LANES, LANES)] for u in range(cols_per_kv)], axis=0)
        return lax.dot_general(lhs, rhs, (((1,), (1,)), ((), ())), preferred_element_type=jnp.float32)

    def softmax(n, s):
        heads = operands(n)[4]
        ex, stats = [], []
        for u, head in enumerate(heads):
            slope = slope_ref[head] * LOG2E
            su = s[u * w:(u + 1) * w] - jnp.concatenate([slope, slope], axis=1) * rel_masked
            sink = sink_ref[head] * LOG2E
            m = jnp.maximum(jnp.max(su, axis=1, keepdims=True), sink)
            ex.append(jnp.exp2(su - jnp.concatenate([m, m], axis=1)).astype(jnp.bfloat16))
            stats.append(jnp.exp2(sink - m))
        return jnp.concatenate(ex, axis=0), stats

    def values(n, ex, stats):
        val = operands(n)[3]
        acc = jnp.dot(ex, jnp.concatenate([val, ones], axis=1), preferred_element_type=jnp.float32)
        return [acc[u * w:(u + 1) * w, :LANES] / (acc[u * w:(u + 1) * w, LANES:] + stats[u])
                for u in range(cols_per_kv)]

    n_batches = 2 * SWA_KV_HEADS
    ahead = 2
    s = {n: logits(n) for n in range(ahead)}
    outs = {}
    for n in range(n_batches):
        if n + ahead < n_batches:
            s[n + ahead] = logits(n + ahead)
        ex, stats = softmax(n, s.pop(n))
        outs[n] = values(n, ex, stats)
        if n % 2 == 1:
            c, e = operands(n)[:2]
            merged = [jnp.where(out_half == e, outs[n - 1][u], outs[n][u]) for u in range(cols_per_kv)]
            o_ref[:, pl.ds(c * cols_per_kv * LANES, cols_per_kv * LANES)] = (
                jnp.concatenate(merged, axis=1).astype(o_ref.dtype))


def _swa_attention(proj, pos_col, pos_row, slopes, sinks):
    s = proj.shape[0]
    w = WINDOW
    kvw = SWA_KV_HEADS * SWA_HEAD_DIM
    kblk = (2 * BRANCH) // kvw
    prev = lambda i: jnp.maximum(i - 1, 0)
    return pl.pallas_call(
        _swa_kernel,
        grid=(s // w,),
        in_specs=[
            pl.BlockSpec((w, BRANCH), lambda i: (i, 0)),
            pl.BlockSpec((w, kvw), lambda i: (i, kblk)),
            pl.BlockSpec((w, kvw), lambda i: (prev(i), kblk)),
            pl.BlockSpec((w, kvw), lambda i: (i, kblk + 1)),
            pl.BlockSpec((w, kvw), lambda i: (prev(i), kblk + 1)),
            pl.BlockSpec((w, LANES), lambda i: (i, 0)),
            pl.BlockSpec((1, w), lambda i: (0, i)),
            pl.BlockSpec((1, w), lambda i: (0, prev(i))),
            pl.BlockSpec((SWA_Q_HEADS, 1, LANES), lambda i: (0, 0, 0)),
            pl.BlockSpec((SWA_Q_HEADS, 1, LANES), lambda i: (0, 0, 0)),
        ],
        out_specs=pl.BlockSpec((w, BRANCH), lambda i: (i, 0)),
        out_shape=jax.ShapeDtypeStruct((s, BRANCH), jnp.bfloat16),
        compiler_params=_params("arbitrary"),
        name="swa_attention",
    )(proj, proj, proj, proj, proj, pos_col, pos_row, pos_row, slopes, sinks)


def _silu(g):
    return g / (1.0 + jnp.exp(-g))


def _out_kernel(*refs, conv):
    if conv:
        (x_ref, bg_ref, cg_ref, u_ref, hc_ref, hu_ref, cw_ref,
         gmix_ref, qm_ref, gmem_ref, kbd_ref, vobd_ref, wo_ref, gpost_ref, o_ref) = refs
        i = pl.program_id(0)
        z = cg_ref[...].astype(jnp.float32) * u_ref[...].astype(jnp.float32)
        zh = hc_ref[...].astype(jnp.float32) * hu_ref[...].astype(jnp.float32)
        zh = zh * (i > 0).astype(jnp.float32)
        row = lax.broadcasted_iota(jnp.int32, z.shape, 0)
        z1 = jnp.where(row == 0, zh[7:8], pltpu.roll(z, 1, 0))
        z2 = jnp.where(row == 0, zh[6:7], jnp.where(row == 1, zh[7:8], pltpu.roll(z, 2, 0)))
        cw = cw_ref[...]
        mix = bg_ref[...].astype(jnp.float32) * (cw[0:1] * z2 + cw[1:2] * z1 + cw[2:3] * z)
    else:
        (x_ref, mix_ref, gmix_ref, qm_ref, gmem_ref, kbd_ref, vobd_ref, wo_ref, gpost_ref,
         o_ref) = refs
        mix = mix_ref[...]

    s = lax.dot_general(qm_ref[...], kbd_ref[0], (((1,), (1,)), ((), ())),
                        preferred_element_type=jnp.float32)
    ps = []
    for h in range(MEM_HEADS):
        sh = s[:, h * MEM_LEN:(h + 1) * MEM_LEN]
        ps.append(jnp.exp(sh - jnp.max(sh, axis=1, keepdims=True)).astype(jnp.bfloat16))
    nd = jnp.dot(jnp.concatenate(ps, axis=1), vobd_ref[0], preferred_element_type=jnp.float32)
    mem_out = nd[:, :MEM_WIDTH] / nd[:, MEM_WIDTH:]

    y_mix = mix.astype(jnp.bfloat16) * _silu(gmix_ref[...])
    y_mem = mem_out.astype(jnp.bfloat16) * _silu(gmem_ref[...])
    y = (jnp.dot(y_mix, wo_ref[:BRANCH, :], preferred_element_type=jnp.float32)
         + jnp.dot(y_mem, wo_ref[BRANCH:, :], preferred_element_type=jnp.float32))
    r = lax.rsqrt(jnp.mean(y * y, axis=-1, keepdims=True) + EPS)
    o_ref[...] = x_ref[...] + y * r * gpost_ref[...]


def _out_layer(x, proj, mix, conv_w, kbd, vobd, layer, w_out, g_post, cols):
    s, d = x.shape
    tm = OUT_TM
    conv = mix is None
    row_blk = lambda width, off: pl.BlockSpec((tm, width), lambda i: (i, off // width))
    in_specs = [pl.BlockSpec((tm, d), lambda i: (i, 0))]
    args = [x]
    if conv:
        halo = lambda off: pl.BlockSpec(
            (8, BRANCH), lambda i: (jnp.maximum(i * (tm // 8) - 1, 0), off // BRANCH))
        in_specs += [row_blk(BRANCH, cols["a"]), row_blk(BRANCH, cols["b"]), row_blk(BRANCH, cols["c"]),
                     halo(cols["b"]), halo(cols["c"]),
                     pl.BlockSpec((CONV_WIDTH, BRANCH), lambda i: (0, 0))]
        args += [proj, proj, proj, proj, proj, conv_w]
    else:
        in_specs += [pl.BlockSpec((tm, BRANCH), lambda i: (i, 0))]
        args += [mix]
    in_specs += [
        row_blk(BRANCH, cols["gate_mix"]), row_blk(MEM_WIDTH, cols["q_mem"]),
        row_blk(MEM_WIDTH, cols["gate_mem"]),
        pl.BlockSpec((1,) + kbd.shape[1:], lambda i: (layer, 0, 0)),
        pl.BlockSpec((1,) + vobd.shape[1:], lambda i: (layer, 0, 0)),
        pl.BlockSpec((GATE_WIDTH, d), lambda i: (0, 0)),
        pl.BlockSpec((1, d), lambda i: (0, 0)),
    ]
    args += [proj, proj, proj, kbd, vobd, w_out, g_post]
    return pl.pallas_call(
        functools.partial(_out_kernel, conv=conv),
        grid=(s // tm,),
        in_specs=in_specs,
        out_specs=pl.BlockSpec((tm, d), lambda i: (i, 0)),
        out_shape=jax.ShapeDtypeStruct((s, d), jnp.float32),
        compiler_params=_params("arbitrary"),
        name="out_conv" if conv else "out_attn",
    )(*args)


def _relayout_w_in(w, kind):
    if kind == 2:
        q, k, v, q_mem, gate = jnp.split(w, [2048, 2304, 2560, 2816], axis=1)
        parts = [q, gate[:, :BRANCH], k, v, q_mem, gate[:, BRANCH:]]
        cols = dict(a=0, gate_mix=2048, k=4096, v=4352, q_mem=4608, gate_mem=4864)
    else:
        a, b, c, q_mem, gate = jnp.split(w, [2048, 4096, 6144, 6400], axis=1)
        parts = [a, b, c, gate[:, :BRANCH], q_mem, gate[:, BRANCH:]]
        cols = dict(a=0, b=2048, c=4096, gate_mix=6144, q_mem=8192, gate_mem=8448)
    return jnp.concatenate(parts, axis=1).astype(jnp.bfloat16), cols


def _alibi_slopes(n_heads):
    return 2.0 ** (-ALIBI_MAX_BIAS * jnp.arange(1, n_heads + 1, dtype=jnp.float32) / n_heads)


def kernel(x, mem, positions, norm_pre_0, norm_post_0, norm_mem_0, w_in_0, w_mem_kv_0, conv_w_0, w_out_0, norm_pre_1, norm_post_1, norm_mem_1, w_in_1, w_mem_kv_1, lambda_q1_1, lambda_k1_1, lambda_q2_1, lambda_k2_1, subln_1, w_out_1, norm_pre_2, norm_post_2, norm_mem_2, w_in_2, w_mem_kv_2, sinks_2, w_out_2, norm_pre_3, norm_post_3, norm_mem_3, w_in_3, w_mem_kv_3, conv_w_3, w_out_3):
    b, s, d = x.shape
    assert b == 1 and s == SEQ and d == D_MODEL
    xs = x.reshape(s, d)
    pos_f = positions.reshape(s).astype(jnp.float32)
    pos_row = pos_f.reshape(1, s)
    pos_col = jnp.broadcast_to(pos_f[:, None], (s, LANES))

    pre = [norm_pre_0, norm_pre_1, norm_pre_2, norm_pre_3]
    post = [norm_post_0, norm_post_1, norm_post_2, norm_post_3]
    w_in = [w_in_0, w_in_1, w_in_2, w_in_3]
    w_out = [w_out_0, w_out_1, w_out_2, w_out_3]
    conv_w = {0: conv_w_0, 3: conv_w_3}

    mem_gain = jnp.stack([norm_mem_0, norm_mem_1, norm_mem_2, norm_mem_3]).reshape(DEPTH, 1, d)
    mem_w = jnp.stack([w_mem_kv_0, w_mem_kv_1, w_mem_kv_2, w_mem_kv_3]).astype(jnp.bfloat16)
    kbd, vobd = _mem_kv(mem.reshape(MEM_LEN, d), mem_gain, mem_w)

    for layer in range(DEPTH):
        kind = layer % 3
        w, cols = _relayout_w_in(w_in[layer], kind)
        proj = _norm_proj(xs, pre[layer].reshape(1, d), w)
        if kind == 0:
            mix = None
        elif kind == 1:
            slopes = jnp.broadcast_to(_alibi_slopes(DIFF_HEADS)[:, None, None], (DIFF_HEADS, 8, LANES))
            lam_init = 0.8 - 0.6 * math.exp(-0.3 * layer)
            mix = _diff_attention(
                proj, pos_row, slopes,
                lambda_q1_1.reshape(1, -1), lambda_k1_1.reshape(1, -1),
                lambda_q2_1.reshape(1, -1), lambda_k2_1.reshape(1, -1),
                subln_1.reshape(1, -1), lam_init)
        else:
            slopes = jnp.broadcast_to(_alibi_slopes(SWA_Q_HEADS)[:, None, None], (SWA_Q_HEADS, 1, LANES))
            sinks = jnp.broadcast_to(sinks_2.astype(jnp.float32)[:, None, None], (SWA_Q_HEADS, 1, LANES))
            mix = _swa_attention(proj, pos_col, pos_row, slopes, sinks)
        xs = _out_layer(xs, proj, mix, conv_w.get(layer), kbd, vobd, layer,
                        w_out[layer].astype(jnp.bfloat16), post[layer].reshape(1, d), cols)
    return xs.reshape(b, s, d)
```

```python
import functools
import math

import jax
import jax.numpy as jnp
from jax import lax
from jax.experimental import pallas as pl
from jax.experimental.pallas import tpu as pltpu

D_MODEL = 1024
SEQ = 16384
DEPTH = 4
MEM_LEN = 256
BRANCH = 2048
CONV_WIDTH = 3
DIFF_HEAD_DIM = 64
DIFF_HEADS = 16
SWA_HEAD_DIM = 64
SWA_Q_HEADS = 32
SWA_KV_HEADS = 4
SWA_GROUP = SWA_Q_HEADS // SWA_KV_HEADS
WINDOW = 128
MEM_HEADS = 4
MEM_HEAD_DIM = 64
MEM_WIDTH = 256
GATE_WIDTH = BRANCH + MEM_WIDTH
ALIBI_MAX_BIAS = 8.0
EPS = 1e-6
NEG_INF = -1e30

LANES = 128
VMEM_LIMIT_BYTES = 56 * 1024 * 1024

PROJ_TM = 512
OUT_TM = 512
DIFF_TQ = 512
DIFF_TK = 256
DIFF_UNROLLS = (16, 8, 4, 2)
LOG2E = math.log2(math.e)


def _params(*sem):
    return pltpu.CompilerParams(dimension_semantics=sem, vmem_limit_bytes=VMEM_LIMIT_BYTES)


def _mem_kv_kernel(mem_ref, g_ref, w_ref, kbd_ref, vobd_ref):
    m = mem_ref[...]
    r = lax.rsqrt(jnp.mean(m * m, axis=-1, keepdims=True) + EPS)
    mn = (m * r * g_ref[0]).astype(jnp.bfloat16)
    kv = jnp.dot(mn, w_ref[0], preferred_element_type=jnp.float32)
    km = kv[:, :MEM_WIDTH] * (MEM_HEAD_DIM ** -0.5)
    vm = kv[:, MEM_WIDTH:]
    head_of_lane = lax.broadcasted_iota(jnp.int32, (MEM_LEN, MEM_WIDTH), 1) // MEM_HEAD_DIM
    for h in range(MEM_HEADS):
        sel = head_of_lane == h
        rows = pl.ds(h * MEM_LEN, MEM_LEN)
        kbd_ref[0, rows, :] = jnp.where(sel, km, 0.0).astype(jnp.bfloat16)
        vobd_ref[0, rows, :MEM_WIDTH] = jnp.where(sel, vm, 0.0).astype(jnp.bfloat16)
        vobd_ref[0, rows, MEM_WIDTH:] = jnp.where(sel, 1.0, 0.0).astype(jnp.bfloat16)


def _mem_kv(mem, gains, weights):
    n_layers = gains.shape[0]
    rows = MEM_HEADS * MEM_LEN
    return pl.pallas_call(
        _mem_kv_kernel,
        grid=(n_layers,),
        in_specs=[
            pl.BlockSpec((MEM_LEN, D_MODEL), lambda l: (0, 0)),
            pl.BlockSpec((1, 1, D_MODEL), lambda l: (l, 0, 0)),
            pl.BlockSpec((1, D_MODEL, 2 * MEM_WIDTH), lambda l: (l, 0, 0)),
        ],
        out_specs=[
            pl.BlockSpec((1, rows, MEM_WIDTH), lambda l: (l, 0, 0)),
            pl.BlockSpec((1, rows, 2 * MEM_WIDTH), lambda l: (l, 0, 0)),
        ],
        out_shape=[
            jax.ShapeDtypeStruct((n_layers, rows, MEM_WIDTH), jnp.bfloat16),
            jax.ShapeDtypeStruct((n_layers, rows, 2 * MEM_WIDTH), jnp.bfloat16),
        ],
        compiler_params=_params("arbitrary"),
        name="mem_kv",
    )(mem, gains, weights)


def _proj_kernel(x_ref, g_ref, w_ref, o_ref):
    x = x_ref[...]
    r = lax.rsqrt(jnp.mean(x * x, axis=-1, keepdims=True) + EPS)
    h = (x * r * g_ref[...]).astype(jnp.bfloat16)
    o_ref[...] = jnp.dot(h, w_ref[...], preferred_element_type=jnp.float32).astype(o_ref.dtype)


def _norm_proj(x, gain, w):
    s, d = x.shape
    c = w.shape[1]
    n_col = 2
    tn = c // n_col
    assert tn * n_col == c and tn % LANES == 0 and s % PROJ_TM == 0
    return pl.pallas_call(
        _proj_kernel,
        grid=(n_col, s // PROJ_TM),
        in_specs=[
            pl.BlockSpec((PROJ_TM, d), lambda j, i: (i, 0)),
            pl.BlockSpec((1, d), lambda j, i: (0, 0)),
            pl.BlockSpec((d, tn), lambda j, i: (0, j)),
        ],
        out_specs=pl.BlockSpec((PROJ_TM, tn), lambda j, i: (i, j)),
        out_shape=jax.ShapeDtypeStruct((s, c), jnp.bfloat16),
        compiler_params=_params("arbitrary", "arbitrary"),
        name="norm_proj",
    )(x, gain, w)


def _diff_attn_kernel(q_ref, k_ref, v_ref, posq_ref, posk_ref, slope_ref, lq1_ref, lk1_ref, lq2_ref,
                      lk2_ref, subln_ref, o_ref, m_ref, acc_ref, s_ref, p_ref, alpha_ref, *, lam_init):
    tq, tk = DIFF_TQ, DIFF_TK
    i = pl.program_id(1)
    d = DIFF_HEAD_DIM
    last_block = SEQ // tk - 1
    reps = tk // LANES
    n_diag = tq // tk
    n_full = i * n_diag

    q = q_ref[...].astype(jnp.float32) * (d ** -0.5 * LOG2E)
    lane = lax.broadcasted_iota(jnp.int32, q.shape, 1)
    qq = jnp.concatenate([jnp.where(lane < d, q, 0.0), jnp.where(lane < d, 0.0, q)],
                         axis=0).astype(jnp.bfloat16)

    slope_row = jnp.concatenate([slope_ref[0, 0:1, :] * LOG2E] * reps, axis=1)
    p0 = posq_ref[0][:, 0:1]
    ones = jnp.ones((tk, LANES), dtype=jnp.bfloat16)

    m_ref[...] = jnp.full(m_ref.shape, NEG_INF, dtype=jnp.float32)
    acc_ref[...] = jnp.zeros(acc_ref.shape, dtype=jnp.float32)
    p_ref[1] = jnp.zeros(p_ref.shape[1:], dtype=p_ref.dtype)
    alpha_ref[1] = jnp.ones(alpha_ref.shape[1:], dtype=jnp.float32)

    def logits(t, extra):
        kb = k_ref[pl.ds(pl.multiple_of(t * tk, tk), tk), :]
        s = lax.dot_general(qq, kb, (((1,), (1,)), ((), ())), preferred_element_type=jnp.float32)
        return s + (slope_row * (posk_ref[t] - p0) + extra)

    def softmax_step(s):
        m_prev = m_ref[...]
        m_new = jnp.maximum(m_prev, jnp.max(s, axis=1, keepdims=True))
        m_ref[...] = m_new
        p = jnp.exp2(s - jnp.concatenate([m_new] * reps, axis=1)).astype(jnp.bfloat16)
        return p, jnp.exp2(m_prev - m_new)

    def accumulate(t, p, alpha):
        vb = v_ref[pl.ds(pl.multiple_of(t * tk, tk), tk), :]
        pv = jnp.dot(p, jnp.concatenate([vb, ones], axis=1), preferred_element_type=jnp.float32)
        acc_ref[...] = acc_ref[...] * jnp.concatenate([alpha, alpha], axis=1) + pv

    row = lax.broadcasted_iota(jnp.int32, (2 * tq, tk), 0)
    row = jnp.where(row >= tq, row - tq, row)
    col = lax.broadcasted_iota(jnp.int32, (2 * tq, tk), 1)

    def stage_a_boundary(b):
        s_ref[b % 2] = jnp.where(col + b * tk <= row, logits(n_full + b, 0.0), NEG_INF)

    def stage_a(b, buf):
        off = jnp.where(b < n_full + n_diag, 0.0, NEG_INF)
        s_ref[buf] = logits(jnp.minimum(b - n_diag, last_block), off)

    def stage_b(buf):
        p, alpha = softmax_step(s_ref[buf])
        p_ref[buf] = p
        alpha_ref[buf] = alpha

    def stage_c(b, buf):
        t = jnp.where(b < n_diag, n_full + b, b - n_diag)
        accumulate(jnp.clip(t, 0, last_block), p_ref[buf], alpha_ref[buf])

    stage_a_boundary(0)
    for b in range(1, n_diag):
        stage_c(b - 2, b % 2)
        stage_b((b - 1) % 2)
        stage_a_boundary(b)

    def run_steps(first, trips, unroll):
        def body(u, carry):
            for r in range(unroll):
                b = first + unroll * u + r + 1
                par = (n_diag + r) % 2
                stage_c(b - 2, par)
                stage_b(1 - par)
                stage_a(b, par)
            return carry
        lax.fori_loop(0, trips, body, 0)

    last_step = n_full + n_diag + 1
    done = n_diag - 1
    for size in DIFF_UNROLLS[:-1]:
        trips = (last_step - done) // size
        run_steps(done, trips, size)
        done = done + trips * size
    run_steps(done, (last_step - done + DIFF_UNROLLS[-1] - 1) // DIFF_UNROLLS[-1], DIFF_UNROLLS[-1])

    lam = (jnp.exp(jnp.sum(lq1_ref[...] * lk1_ref[...], axis=1, keepdims=True))
           - jnp.exp(jnp.sum(lq2_ref[...] * lk2_ref[...], axis=1, keepdims=True)) + lam_init)
    acc = acc_ref[...]
    o = acc[:, :LANES] / acc[:, LANES:]
    o = o[:tq] - lam * o[tq:]
    r = lax.rsqrt(jnp.mean(o * o, axis=-1, keepdims=True) + EPS)
    o_ref[...] = (o * r * subln_ref[...] * (1.0 - lam_init)).astype(o_ref.dtype)


def _diff_attention(proj, pos_row, slopes, lq1, lk1, lq2, lk2, subln, lam_init):
    s = proj.shape[0]
    tq, tk = DIFF_TQ, DIFF_TK
    assert tq % tk == 0 and s % tq == 0 and all(u % 2 == 0 for u in DIFF_UNROLLS)
    hb = BRANCH // LANES
    vec = lambda n: pl.BlockSpec((1, n), lambda h, i: (0, 0))
    return pl.pallas_call(
        functools.partial(_diff_attn_kernel, lam_init=lam_init),
        grid=(DIFF_HEADS, s // tq),
        in_specs=[
            pl.BlockSpec((tq, LANES), lambda h, i: (i, h)),
            pl.BlockSpec((s, LANES), lambda h, i: (0, hb + h)),
            pl.BlockSpec((s, LANES), lambda h, i: (0, 2 * hb + h)),
            pl.BlockSpec((1, 1, tq), lambda h, i: (i, 0, 0)),
            pl.BlockSpec((s // tk, 1, tk), lambda h, i: (0, 0, 0)),
            pl.BlockSpec((1, 8, LANES), lambda h, i: (h, 0, 0)),
            vec(DIFF_HEAD_DIM), vec(DIFF_HEAD_DIM), vec(DIFF_HEAD_DIM), vec(DIFF_HEAD_DIM),
            vec(2 * DIFF_HEAD_DIM),
        ],
        out_specs=pl.BlockSpec((tq, LANES), lambda h, i: (i, h)),
        out_shape=jax.ShapeDtypeStruct((s, BRANCH), jnp.bfloat16),
        scratch_shapes=[
            pltpu.VMEM((2 * tq, LANES), jnp.float32),
            pltpu.VMEM((2 * tq, 2 * LANES), jnp.float32),
            pltpu.VMEM((2, 2 * tq, tk), jnp.float32),
            pltpu.VMEM((2, 2 * tq, tk), jnp.bfloat16),
            pltpu.VMEM((2, 2 * tq, LANES), jnp.float32),
        ],
        compiler_params=_params("arbitrary", "arbitrary"),
        name="diff_attention",
    )(proj, proj, proj, pos_row.reshape(s // tq, 1, tq), pos_row.reshape(s // tk, 1, tk), slopes,
      lq1, lk1, lq2, lk2, subln)


def _swap_halves(x):
    half = LANES // 2
    return jnp.concatenate([x[:, half:], x[:, :half]], axis=1)


def _swa_kernel(q_ref, kc_ref, kp_ref, vc_ref, vp_ref, pq_ref, pkc_ref, pkp_ref, slope_ref,
                sink_ref, o_ref):
    i = pl.program_id(0)
    w = WINDOW
    d = SWA_HEAD_DIM
    cols_per_kv = SWA_GROUP // 2
    rel = jnp.concatenate([pq_ref[...] - pkp_ref[...], pq_ref[...] - pkc_ref[...]], axis=1)
    col = lax.broadcasted_iota(jnp.int32, (w, 2 * w), 1)
    valid = (rel >= 0.0) & (rel < float(w)) & ((col >= w) | (i > 0))
    rel_masked = jnp.where(valid, rel, -NEG_INF)
    lane_half = lax.broadcasted_iota(jnp.int32, (2 * w, LANES), 1) // d
    out_half = lax.broadcasted_iota(jnp.int32, (w, LANES), 1) // d
    ones = jnp.ones((2 * w, LANES), dtype=jnp.bfloat16)

    pairs = []
    for pair in range(SWA_KV_HEADS // 2):
        lanes = slice(pair * LANES, (pair + 1) * LANES)
        kk = jnp.concatenate([kp_ref[:, lanes], kc_ref[:, lanes]], axis=0)
        kk = (kk.astype(jnp.float32) * (d ** -0.5 * LOG2E)).astype(jnp.bfloat16)
        vv = jnp.concatenate([vp_ref[:, lanes], vc_ref[:, lanes]], axis=0)
        pairs.append((kk, _swap_halves(kk), vv, _swap_halves(vv)))

    def operands(n):
        c, t = divmod(n, 2)
        pair, e = divmod(c, 2)
        kk, kk_sw, vv, vv_sw = pairs[pair]
        zero = jnp.zeros_like(kk)
        if t == 0:
            rhs, val = jnp.where(lane_half == e, kk, zero), vv
        else:
            rhs, val = jnp.where(lane_half == e, zero, kk_sw), vv_sw
        heads = [c * SWA_GROUP + 2 * u + (e if t == 0 else 1 - e) for u in range(cols_per_kv)]
        return c, e, rhs, val, heads

    def logits(n):
        c, _, rhs, _, _ = operands(n)
        lhs = jnp.concatenate(
            [q_ref[:, pl.ds((c * cols_per_kv + u) * LANES, LANES)] for u in range(cols_per_kv)], axis=0)
        return lax.dot_general(lhs, rhs, (((1,), (1,)), ((), ())), preferred_element_type=jnp.float32)

    def softmax(n, s):
        heads = operands(n)[4]
        ex, stats = [], []
        for u, head in enumerate(heads):
            slope = slope_ref[head] * LOG2E
            su = s[u * w:(u + 1) * w] - jnp.concatenate([slope, slope], axis=1) * rel_masked
            sink = sink_ref[head] * LOG2E
            m = jnp.maximum(jnp.max(su, axis=1, keepdims=True), sink)
            ex.append(jnp.exp2(su - jnp.concatenate([m, m], axis=1)).astype(jnp.bfloat16))
            stats.append(jnp.exp2(sink - m))
        return jnp.concatenate(ex, axis=0), stats

    def values(n, ex, stats):
        val = operands(n)[3]
        acc = jnp.dot(ex, jnp.concatenate([val, ones], axis=1), preferred_element_type=jnp.float32)
        return [acc[u * w:(u + 1) * w, :LANES] / (acc[u * w:(u + 1) * w, LANES:] + stats[u])
                for u in range(cols_per_kv)]

    n_batches = 2 * SWA_KV_HEADS
    ahead = 2
    s = {n: logits(n) for n in range(ahead)}
    outs = {}
    for n in range(n_batches):
        if n + ahead < n_batches:
            s[n + ahead] = logits(n + ahead)
        ex, stats = softmax(n, s.pop(n))
        outs[n] = values(n, ex, stats)
        if n % 2 == 1:
            c, e = operands(n)[:2]
            merged = [jnp.where(out_half == e, outs[n - 1][u], outs[n][u]) for u in range(cols_per_kv)]
            o_ref[:, pl.ds(c * cols_per_kv * LANES, cols_per_kv * LANES)] = (
                jnp.concatenate(merged, axis=1).astype(o_ref.dtype))


def _swa_attention(proj, pos_col, pos_row, slopes, sinks):
    s = proj.shape[0]
    w = WINDOW
    kvw = SWA_KV_HEADS * SWA_HEAD_DIM
    kblk = (2 * BRANCH) // kvw
    prev = lambda i: jnp.maximum(i - 1, 0)
    return pl.pallas_call(
        _swa_kernel,
        grid=(s // w,),
        in_specs=[
            pl.BlockSpec((w, BRANCH), lambda i: (i, 0)),
            pl.BlockSpec((w, kvw), lambda i: (i, kblk)),
            pl.BlockSpec((w, kvw), lambda i: (prev(i), kblk)),
            pl.BlockSpec((w, kvw), lambda i: (i, kblk + 1)),
            pl.BlockSpec((w, kvw), lambda i: (prev(i), kblk + 1)),
            pl.BlockSpec((w, LANES), lambda i: (i, 0)),
            pl.BlockSpec((1, w), lambda i: (0, i)),
            pl.BlockSpec((1, w), lambda i: (0, prev(i))),
            pl.BlockSpec((SWA_Q_HEADS, 1, LANES), lambda i: (0, 0, 0)),
            pl.BlockSpec((SWA_Q_HEADS, 1, LANES), lambda i: (0, 0, 0)),
        ],
        out_specs=pl.BlockSpec((w, BRANCH), lambda i: (i, 0)),
        out_shape=jax.ShapeDtypeStruct((s, BRANCH), jnp.bfloat16),
        compiler_params=_params("arbitrary"),
        name="swa_attention",
    )(proj, proj, proj, proj, proj, pos_col, pos_row, pos_row, slopes, sinks)


def _silu(g):
    return g / (1.0 + jnp.exp(-g))


def _out_kernel(*refs, conv):
    if conv:
        (x_ref, bg_ref, cg_ref, u_ref, hc_ref, hu_ref, cw_ref,
         gmix_ref, qm_ref, gmem_ref, kbd_ref, vobd_ref, wo_ref, gpost_ref, o_ref) = refs
        i = pl.program_id(0)
        z = cg_ref[...].astype(jnp.float32) * u_ref[...].astype(jnp.float32)
        zh = hc_ref[...].astype(jnp.float32) * hu_ref[...].astype(jnp.float32)
        zh = zh * (i > 0).astype(jnp.float32)
        row = lax.broadcasted_iota(jnp.int32, z.shape, 0)
        z1 = jnp.where(row == 0, zh[7:8], pltpu.roll(z, 1, 0))
        z2 = jnp.where(row == 0, zh[6:7], jnp.where(row == 1, zh[7:8], pltpu.roll(z, 2, 0)))
        cw = cw_ref[...]
        mix = bg_ref[...].astype(jnp.float32) * (cw[0:1] * z2 + cw[1:2] * z1 + cw[2:3] * z)
    else:
        (x_ref, mix_ref, gmix_ref, qm_ref, gmem_ref, kbd_ref, vobd_ref, wo_ref, gpost_ref,
         o_ref) = refs
        mix = mix_ref[...]

    s = lax.dot_general(qm_ref[...], kbd_ref[0], (((1,), (1,)), ((), ())),
                        preferred_element_type=jnp.float32)
    ps = []
    for h in range(MEM_HEADS):
        sh = s[:, h * MEM_LEN:(h + 1) * MEM_LEN]
        ps.append(jnp.exp(sh - jnp.max(sh, axis=1, keepdims=True)).astype(jnp.bfloat16))
    nd = jnp.dot(jnp.concatenate(ps, axis=1), vobd_ref[0], preferred_element_type=jnp.float32)
    mem_out = nd[:, :MEM_WIDTH] / nd[:, MEM_WIDTH:]

    y_mix = mix.astype(jnp.bfloat16) * _silu(gmix_ref[...])
    y_mem = mem_out.astype(jnp.bfloat16) * _silu(gmem_ref[...])
    y = (jnp.dot(y_mix, wo_ref[:BRANCH, :], preferred_element_type=jnp.float32)
         + jnp.dot(y_mem, wo_ref[BRANCH:, :], preferred_element_type=jnp.float32))
    r = lax.rsqrt(jnp.mean(y * y, axis=-1, keepdims=True) + EPS)
    o_ref[...] = x_ref[...] + y * r * gpost_ref[...]


def _out_layer(x, proj, mix, conv_w, kbd, vobd, layer, w_out, g_post, cols):
    s, d = x.shape
    tm = OUT_TM
    conv = mix is None
    row_blk = lambda width, off: pl.BlockSpec((tm, width), lambda i: (i, off // width))
    in_specs = [pl.BlockSpec((tm, d), lambda i: (i, 0))]
    args = [x]
    if conv:
        halo = lambda off: pl.BlockSpec(
            (8, BRANCH), lambda i: (jnp.maximum(i * (tm // 8) - 1, 0), off // BRANCH))
        in_specs += [row_blk(BRANCH, cols["a"]), row_blk(BRANCH, cols["b"]), row_blk(BRANCH, cols["c"]),
                     halo(cols["b"]), halo(cols["c"]),
                     pl.BlockSpec((CONV_WIDTH, BRANCH), lambda i: (0, 0))]
        args += [proj, proj, proj, proj, proj, conv_w]
    else:
        in_specs += [pl.BlockSpec((tm, BRANCH), lambda i: (i, 0))]
        args += [mix]
    in_specs += [
        row_blk(BRANCH, cols["gate_mix"]), row_blk(MEM_WIDTH, cols["q_mem"]),
        row_blk(MEM_WIDTH, cols["gate_mem"]),
        pl.BlockSpec((1,) + kbd.shape[1:], lambda i: (layer, 0, 0)),
        pl.BlockSpec((1,) + vobd.shape[1:], lambda i: (layer, 0, 0)),
        pl.BlockSpec((GATE_WIDTH, d), lambda i: (0, 0)),
        pl.BlockSpec((1, d), lambda i: (0, 0)),
    ]
    args += [proj, proj, proj, kbd, vobd, w_out, g_post]
    return pl.pallas_call(
        functools.partial(_out_kernel, conv=conv),
        grid=(s // tm,),
        in_specs=in_specs,
        out_specs=pl.BlockSpec((tm, d), lambda i: (i, 0)),
        out_shape=jax.ShapeDtypeStruct((s, d), jnp.float32),
        compiler_params=_params("arbitrary"),
        name="out_conv" if conv else "out_attn",
    )(*args)


def _relayout_w_in(w, kind):
    if kind == 2:
        q, k, v, q_mem, gate = jnp.split(w, [2048, 2304, 2560, 2816], axis=1)
        parts = [q, gate[:, :BRANCH], k, v, q_mem, gate[:, BRANCH:]]
        cols = dict(a=0, gate_mix=2048, k=4096, v=4352, q_mem=4608, gate_mem=4864)
    else:
        a, b, c, q_mem, gate = jnp.split(w, [2048, 4096, 6144, 6400], axis=1)
        parts = [a, b, c, gate[:, :BRANCH], q_mem, gate[:, BRANCH:]]
        cols = dict(a=0, b=2048, c=4096, gate_mix=6144, q_mem=8192, gate_mem=8448)
    return jnp.concatenate(parts, axis=1).astype(jnp.bfloat16), cols


def _alibi_slopes(n_heads):
    return 2.0 ** (-ALIBI_MAX_BIAS * jnp.arange(1, n_heads + 1, dtype=jnp.float32) / n_heads)


def kernel(x, mem, positions, norm_pre_0, norm_post_0, norm_mem_0, w_in_0, w_mem_kv_0, conv_w_0, w_out_0, norm_pre_1, norm_post_1, norm_mem_1, w_in_1, w_mem_kv_1, lambda_q1_1, lambda_k1_1, lambda_q2_1, lambda_k2_1, subln_1, w_out_1, norm_pre_2, norm_post_2, norm_mem_2, w_in_2, w_mem_kv_2, sinks_2, w_out_2, norm_pre_3, norm_post_3, norm_mem_3, w_in_3, w_mem_kv_3, conv_w_3, w_out_3):
    b, s, d = x.shape
    assert b == 1 and s == SEQ and d == D_MODEL
    xs = x.reshape(s, d)
    pos_f = positions.reshape(s).astype(jnp.float32)
    pos_row = pos_f.reshape(1, s)
    pos_col = jnp.broadcast_to(pos_f[:, None], (s, LANES))

    pre = [norm_pre_0, norm_pre_1, norm_pre_2, norm_pre_3]
    post = [norm_post_0, norm_post_1, norm_post_2, norm_post_3]
    w_in = [w_in_0, w_in_1, w_in_2, w_in_3]
    w_out = [w_out_0, w_out_1, w_out_2, w_out_3]
    conv_w = {0: conv_w_0, 3: conv_w_3}

    mem_gain = jnp.stack([norm_mem_0, norm_mem_1, norm_mem_2, norm_mem_3]).reshape(DEPTH, 1, d)
    mem_w = jnp.stack([w_mem_kv_0, w_mem_kv_1, w_mem_kv_2, w_mem_kv_3]).astype(jnp.bfloat16)
    kbd, vobd = _mem_kv(mem.reshape(MEM_LEN, d), mem_gain, mem_w)

    for layer in range(DEPTH):
        kind = layer % 3
        w, cols = _relayout_w_in(w_in[layer], kind)
        proj = _norm_proj(xs, pre[layer].reshape(1, d), w)
        if kind == 0:
            mix = None
        elif kind == 1:
            slopes = jnp.broadcast_to(_alibi_slopes(DIFF_HEADS)[:, None, None], (DIFF_HEADS, 8, LANES))
            lam_init = 0.8 - 0.6 * math.exp(-0.3 * layer)
            mix = _diff_attention(
                proj, pos_row, slopes,
                lambda_q1_1.reshape(1, -1), lambda_k1_1.reshape(1, -1),
                lambda_q2_1.reshape(1, -1), lambda_k2_1.reshape(1, -1),
                subln_1.reshape(1, -1), lam_init)
        else:
            slopes = jnp.broadcast_to(_alibi_slopes(SWA_Q_HEADS)[:, None, None], (SWA_Q_HEADS, 1, LANES))
            sinks = jnp.broadcast_to(sinks_2.astype(jnp.float32)[:, None, None], (SWA_Q_HEADS, 1, LANES))
            mix = _swa_attention(proj, pos_col, pos_row, slopes, sinks)
        xs = _out_layer(xs, proj, mix, conv_w.get(layer), kbd, vobd, layer,
                        w_out[layer].astype(jnp.bfloat16), post[layer].reshape(1, d), cols)
    return xs.reshape(b, s, d)
```

```python
import functools
import math

import jax
import jax.numpy as jnp
from jax import lax
from jax.experimental import pallas as pl
from jax.experimental.pallas import tpu as pltpu

D_MODEL = 1024
SEQ = 16384
DEPTH = 4
MEM_LEN = 256
BRANCH = 2048
CONV_WIDTH = 3
DIFF_HEAD_DIM = 64
DIFF_HEADS = 16
SWA_HEAD_DIM = 64
SWA_Q_HEADS = 32
SWA_KV_HEADS = 4
SWA_GROUP = SWA_Q_HEADS // SWA_KV_HEADS
WINDOW = 128
MEM_HEADS = 4
MEM_HEAD_DIM = 64
MEM_WIDTH = 256
GATE_WIDTH = BRANCH + MEM_WIDTH
ALIBI_MAX_BIAS = 8.0
EPS = 1e-6
NEG_INF = -1e30

LANES = 128
VMEM_LIMIT_BYTES = 56 * 1024 * 1024

PROJ_TM = 512
OUT_TM = 512
DIFF_TQ = 512
DIFF_TK = 256
DIFF_UNROLLS = (16, 8, 4, 2)
FEAT_RADIX = 128.0
FEAT_DUMMY_LANE = 6
LOG2E = math.log2(math.e)


def _params(*sem):
    return pltpu.CompilerParams(dimension_semantics=sem, vmem_limit_bytes=VMEM_LIMIT_BYTES)


def _mem_kv_kernel(mem_ref, g_ref, w_ref, kbd_ref, vobd_ref):
    m = mem_ref[...]
    r = lax.rsqrt(jnp.mean(m * m, axis=-1, keepdims=True) + EPS)
    mn = (m * r * g_ref[0]).astype(jnp.bfloat16)
    kv = jnp.dot(mn, w_ref[0], preferred_element_type=jnp.float32)
    km = kv[:, :MEM_WIDTH] * (MEM_HEAD_DIM ** -0.5)
    vm = kv[:, MEM_WIDTH:]
    head_of_lane = lax.broadcasted_iota(jnp.int32, (MEM_LEN, MEM_WIDTH), 1) // MEM_HEAD_DIM
    for h in range(MEM_HEADS):
        sel = head_of_lane == h
        rows = pl.ds(h * MEM_LEN, MEM_LEN)
        kbd_ref[0, rows, :] = jnp.where(sel, km, 0.0).astype(jnp.bfloat16)
        vobd_ref[0, rows, :MEM_WIDTH] = jnp.where(sel, vm, 0.0).astype(jnp.bfloat16)
        vobd_ref[0, rows, MEM_WIDTH:] = jnp.where(sel, 1.0, 0.0).astype(jnp.bfloat16)


def _mem_kv(mem, gains, weights):
    n_layers = gains.shape[0]
    rows = MEM_HEADS * MEM_LEN
    return pl.pallas_call(
        _mem_kv_kernel,
        grid=(n_layers,),
        in_specs=[
            pl.BlockSpec((MEM_LEN, D_MODEL), lambda l: (0, 0)),
            pl.BlockSpec((1, 1, D_MODEL), lambda l: (l, 0, 0)),
            pl.BlockSpec((1, D_MODEL, 2 * MEM_WIDTH), lambda l: (l, 0, 0)),
        ],
        out_specs=[
            pl.BlockSpec((1, rows, MEM_WIDTH), lambda l: (l, 0, 0)),
            pl.BlockSpec((1, rows, 2 * MEM_WIDTH), lambda l: (l, 0, 0)),
        ],
        out_shape=[
            jax.ShapeDtypeStruct((n_layers, rows, MEM_WIDTH), jnp.bfloat16),
            jax.ShapeDtypeStruct((n_layers, rows, 2 * MEM_WIDTH), jnp.bfloat16),
        ],
        compiler_params=_params("arbitrary"),
        name="mem_kv",
    )(mem, gains, weights)


def _proj_kernel(x_ref, g_ref, w_ref, o_ref):
    x = x_ref[...]
    r = lax.rsqrt(jnp.mean(x * x, axis=-1, keepdims=True) + EPS)
    h = (x * r * g_ref[...]).astype(jnp.bfloat16)
    o_ref[...] = jnp.dot(h, w_ref[...], preferred_element_type=jnp.float32).astype(o_ref.dtype)


def _norm_proj(x, gain, w):
    s, d = x.shape
    c = w.shape[1]
    n_col = 2
    tn = c // n_col
    assert tn * n_col == c and tn % LANES == 0 and s % PROJ_TM == 0
    return pl.pallas_call(
        _proj_kernel,
        grid=(n_col, s // PROJ_TM),
        in_specs=[
            pl.BlockSpec((PROJ_TM, d), lambda j, i: (i, 0)),
            pl.BlockSpec((1, d), lambda j, i: (0, 0)),
            pl.BlockSpec((d, tn), lambda j, i: (0, j)),
        ],
        out_specs=pl.BlockSpec((PROJ_TM, tn), lambda j, i: (i, j)),
        out_shape=jax.ShapeDtypeStruct((s, c), jnp.bfloat16),
        compiler_params=_params("arbitrary", "arbitrary"),
        name="norm_proj",
    )(x, gain, w)


def _diff_attn_kernel(q_ref, k_ref, v_ref, posk_ref, slope_ref, lq1_ref, lk1_ref, lq2_ref, lk2_ref,
                      subln_ref, o_ref, m_ref, acc_ref, s_ref, p_ref, alpha_ref, kx_ref, *, lam_init):
    tq, tk = DIFF_TQ, DIFF_TK
    i = pl.program_id(1)
    d = DIFF_HEAD_DIM
    last_block = SEQ // tk - 1
    reps = tk // LANES
    n_diag = tq // tk
    n_full = i * n_diag

    q = q_ref[...].astype(jnp.float32) * (d ** -0.5 * LOG2E)
    lane = lax.broadcasted_iota(jnp.int32, q.shape, 1)
    qq = jnp.concatenate([jnp.where(lane < d, q, 0.0), jnp.where(lane < d, 0.0, q)],
                         axis=0).astype(jnp.bfloat16)

    slope = slope_ref[0, 0:1, :] * LOG2E
    s1 = slope.astype(jnp.bfloat16).astype(jnp.float32)
    s2 = (slope - s1).astype(jnp.bfloat16).astype(jnp.float32)
    s3 = (slope - s1 - s2).astype(jnp.bfloat16).astype(jnp.float32)
    flane = lax.broadcasted_iota(jnp.int32, (1, LANES), 1)
    qfeat = jnp.zeros((1, LANES), jnp.float32)
    for f, val in enumerate([s1 * FEAT_RADIX, s2 * FEAT_RADIX, s3 * FEAT_RADIX, s1, s2, s3]):
        qfeat = jnp.where(flane == f, val, qfeat)
    qfeat = jnp.where(flane == FEAT_DUMMY_LANE, NEG_INF, qfeat)
    qq = jnp.concatenate(
        [qq, jnp.broadcast_to(qfeat, (2 * tq, LANES)).astype(jnp.bfloat16)], axis=1)

    n_blocks = SEQ // tk

    @pl.when(i == 0)
    def _build_keys():
        lane_t = lax.broadcasted_iota(jnp.int32, (tk, LANES), 1)

        def fill(t, carry):
            rows = pl.ds(pl.multiple_of(t * tk, tk), tk)
            pos = posk_ref[rows, :]
            hi = jnp.floor(pos * (1.0 / FEAT_RADIX))
            lo = pos - hi * FEAT_RADIX
            feat = jnp.where(lane_t < 3, hi, jnp.where(lane_t < 6, lo, 0.0))
            kx_ref[t, :, :LANES] = k_ref[rows, :]
            kx_ref[t, :, LANES:] = feat.astype(jnp.bfloat16)
            return carry

        lax.fori_loop(0, n_blocks, fill, 0)
        kx_ref[n_blocks, :, :LANES] = jnp.zeros((tk, LANES), jnp.bfloat16)
        kx_ref[n_blocks, :, LANES:] = jnp.where(lane_t == FEAT_DUMMY_LANE, 1.0, 0.0).astype(jnp.bfloat16)

    ones = jnp.ones((tk, LANES), dtype=jnp.bfloat16)

    m_ref[...] = jnp.full(m_ref.shape, NEG_INF, dtype=jnp.float32)
    acc_ref[...] = jnp.zeros(acc_ref.shape, dtype=jnp.float32)
    if n_diag == 1:
        p_ref[1] = jnp.zeros(p_ref.shape[1:], dtype=p_ref.dtype)
        alpha_ref[1] = jnp.ones(alpha_ref.shape[1:], dtype=jnp.float32)

    def logits(t):
        return lax.dot_general(qq, kx_ref[t], (((1,), (1,)), ((), ())),
                               preferred_element_type=jnp.float32)

    def softmax_step(s):
        m_prev = m_ref[...]
        m_new = jnp.maximum(m_prev, jnp.max(s, axis=1, keepdims=True))
        m_ref[...] = m_new
        p = jnp.exp2(s - jnp.concatenate([m_new] * reps, axis=1)).astype(jnp.bfloat16)
        return p, jnp.exp2(m_prev - m_new)

    def accumulate(t, p, alpha):
        vb = v_ref[pl.ds(pl.multiple_of(t * tk, tk), tk), :]
        pv = jnp.dot(p, jnp.concatenate([vb, ones], axis=1), preferred_element_type=jnp.float32)
        acc_ref[...] = acc_ref[...] * jnp.concatenate([alpha, alpha], axis=1) + pv

    row = lax.broadcasted_iota(jnp.int32, (2 * tq, tk), 0)
    row = jnp.where(row >= tq, row - tq, row)
    col = lax.broadcasted_iota(jnp.int32, (2 * tq, tk), 1)

    def stage_a_boundary(b):
        s_ref[b % 2] = jnp.where(col + b * tk <= row, logits(n_full + b), NEG_INF)

    def stage_a(b, buf):
        s_ref[buf] = logits(jnp.where(b < n_full + n_diag, b - n_diag, n_blocks))

    def stage_b(buf):
        p, alpha = softmax_step(s_ref[buf])
        p_ref[buf] = p
        alpha_ref[buf] = alpha

    def stage_c(b, buf):
        t = jnp.where(b < n_diag, n_full + b, b - n_diag)
        accumulate(jnp.clip(t, 0, last_block), p_ref[buf], alpha_ref[buf])

    stage_a_boundary(0)
    for b in range(1, n_diag):
        if b >= 2:
            stage_c(b - 2, b % 2)
        stage_b((b - 1) % 2)
        stage_a_boundary(b)

    def run_steps(first, trips, unroll):
        def body(u, carry):
            for r in range(unroll):
                b = first + unroll * u + r + 1
                par = (n_diag + r) % 2
                stage_c(b - 2, par)
                stage_b(1 - par)
                stage_a(b, par)
            return carry
        lax.fori_loop(0, trips, body, 0)

    last_step = n_full + n_diag + 1
    done = n_diag - 1
    for size in DIFF_UNROLLS[:-1]:
        trips = (last_step - done) // size
        run_steps(done, trips, size)
        done = done + trips * size
    run_steps(done, (last_step - done + DIFF_UNROLLS[-1] - 1) // DIFF_UNROLLS[-1], DIFF_UNROLLS[-1])

    lam = (jnp.exp(jnp.sum(lq1_ref[...] * lk1_ref[...], axis=1, keepdims=True))
           - jnp.exp(jnp.sum(lq2_ref[...] * lk2_ref[...], axis=1, keepdims=True)) + lam_init)
    acc = acc_ref[...]
    o = acc[:, :LANES] / acc[:, LANES:]
    o = o[:tq] - lam * o[tq:]
    r = lax.rsqrt(jnp.mean(o * o, axis=-1, keepdims=True) + EPS)
    o_ref[...] = (o * r * subln_ref[...] * (1.0 - lam_init)).astype(o_ref.dtype)


def _diff_attention(proj, pos_col, slopes, lq1, lk1, lq2, lk2, subln, lam_init):
    s = proj.shape[0]
    tq, tk = DIFF_TQ, DIFF_TK
    assert tq % tk == 0 and s % tq == 0 and all(u % 2 == 0 for u in DIFF_UNROLLS)
    hb = BRANCH // LANES
    vec = lambda n: pl.BlockSpec((1, n), lambda h, i: (0, 0))
    return pl.pallas_call(
        functools.partial(_diff_attn_kernel, lam_init=lam_init),
        grid=(DIFF_HEADS, s // tq),
        in_specs=[
            pl.BlockSpec((tq, LANES), lambda h, i: (i, h)),
            pl.BlockSpec((s, LANES), lambda h, i: (0, hb + h)),
            pl.BlockSpec((s, LANES), lambda h, i: (0, 2 * hb + h)),
            pl.BlockSpec((s, LANES), lambda h, i: (0, 0)),
            pl.BlockSpec((1, 8, LANES), lambda h, i: (h, 0, 0)),
            vec(DIFF_HEAD_DIM), vec(DIFF_HEAD_DIM), vec(DIFF_HEAD_DIM), vec(DIFF_HEAD_DIM),
            vec(2 * DIFF_HEAD_DIM),
        ],
        out_specs=pl.BlockSpec((tq, LANES), lambda h, i: (i, h)),
        out_shape=jax.ShapeDtypeStruct((s, BRANCH), jnp.bfloat16),
        scratch_shapes=[
            pltpu.VMEM((2 * tq, LANES), jnp.float32),
            pltpu.VMEM((2 * tq, 2 * LANES), jnp.float32),
            pltpu.VMEM((2, 2 * tq, tk), jnp.float32),
            pltpu.VMEM((2, 2 * tq, tk), jnp.bfloat16),
            pltpu.VMEM((2, 2 * tq, LANES), jnp.float32),
            pltpu.VMEM((s // tk + 1, tk, 2 * LANES), jnp.bfloat16),
        ],
        compiler_params=_params("arbitrary", "arbitrary"),
        name="diff_attention",
    )(proj, proj, proj, pos_col, slopes, lq1, lk1, lq2, lk2, subln)


def _swap_halves(x):
    half = LANES // 2
    return jnp.concatenate([x[:, half:], x[:, :half]], axis=1)


def _swa_kernel(q_ref, kc_ref, kp_ref, vc_ref, vp_ref, pq_ref, pkc_ref, pkp_ref, slope_ref,
                sink_ref, o_ref):
    i = pl.program_id(0)
    w = WINDOW
    d = SWA_HEAD_DIM
    cols_per_kv = SWA_GROUP // 2
    rel = jnp.concatenate([pq_ref[...] - pkp_ref[...], pq_ref[...] - pkc_ref[...]], axis=1)
    col = lax.broadcasted_iota(jnp.int32, (w, 2 * w), 1)
    valid = (rel >= 0.0) & (rel < float(w)) & ((col >= w) | (i > 0))
    rel_masked = jnp.where(valid, rel, -NEG_INF)
    lane_half = lax.broadcasted_iota(jnp.int32, (2 * w, LANES), 1) // d
    out_half = lax.broadcasted_iota(jnp.int32, (w, LANES), 1) // d
    ones = jnp.ones((2 * w, LANES), dtype=jnp.bfloat16)

    pairs = []
    for pair in range(SWA_KV_HEADS // 2):
        lanes = slice(pair * LANES, (pair + 1) * LANES)
        kk = jnp.concatenate([kp_ref[:, lanes], kc_ref[:, lanes]], axis=0)
        kk = (kk.astype(jnp.float32) * (d ** -0.5 * LOG2E)).astype(jnp.bfloat16)
        vv = jnp.concatenate([vp_ref[:, lanes], vc_ref[:, lanes]], axis=0)
        pairs.append((kk, _swap_halves(kk), vv, _swap_halves(vv)))

    def operands(n):
        c, t = divmod(n, 2)
        pair, e = divmod(c, 2)
        kk, kk_sw, vv, vv_sw = pairs[pair]
        zero = jnp.zeros_like(kk)
        if t == 0:
            rhs, val = jnp.where(lane_half == e, kk, zero), vv
        else:
            rhs, val = jnp.where(lane_half == e, zero, kk_sw), vv_sw
        heads = [c * SWA_GROUP + 2 * u + (e if t == 0 else 1 - e) for u in range(cols_per_kv)]
        return c, e, rhs, val, heads

    def logits(n):
        c, _, rhs, _, _ = operands(n)
        lhs = jnp.concatenate(
            [q_ref[:, pl.ds((c * cols_per_kv + u) * LANES, LANES)] for u in range(cols_per_kv)], axis=0)
        return lax.dot_general(lhs, rhs, (((1,), (1,)), ((), ())), preferred_element_type=jnp.float32)

    def softmax(n, s):
        heads = operands(n)[4]
        ex, stats = [], []
        for u, head in enumerate(heads):
            slope = slope_ref[head] * LOG2E
            su = s[u * w:(u + 1) * w] - jnp.concatenate([slope, slope], axis=1) * rel_masked
            sink = sink_ref[head] * LOG2E
            m = jnp.maximum(jnp.max(su, axis=1, keepdims=True), sink)
            ex.append(jnp.exp2(su - jnp.concatenate([m, m], axis=1)).astype(jnp.bfloat16))
            stats.append(jnp.exp2(sink - m))
        return jnp.concatenate(ex, axis=0), stats

    def values(n, ex, stats):
        val = operands(n)[3]
        acc = jnp.dot(ex, jnp.concatenate([val, ones], axis=1), preferred_element_type=jnp.float32)
        return [acc[u * w:(u + 1) * w, :LANES] / (acc[u * w:(u + 1) * w, LANES:] + stats[u])
                for u in range(cols_per_kv)]

    n_batches = 2 * SWA_KV_HEADS
    ahead = 2
    s = {n: logits(n) for n in range(ahead)}
    outs = {}
    for n in range(n_batches):
        if n + ahead < n_batches:
            s[n + ahead] = logits(n + ahead)
        ex, stats = softmax(n, s.pop(n))
        outs[n] = values(n, ex, stats)
        if n % 2 == 1:
            c, e = operands(n)[:2]
            merged = [jnp.where(out_half == e, outs[n - 1][u], outs[n][u]) for u in range(cols_per_kv)]
            o_ref[:, pl.ds(c * cols_per_kv * LANES, cols_per_kv * LANES)] = (
                jnp.concatenate(merged, axis=1).astype(o_ref.dtype))


def _swa_attention(proj, pos_col, pos_row, slopes, sinks):
    s = proj.shape[0]
    w = WINDOW
    kvw = SWA_KV_HEADS * SWA_HEAD_DIM
    kblk = (2 * BRANCH) // kvw
    prev = lambda i: jnp.maximum(i - 1, 0)
    return pl.pallas_call(
        _swa_kernel,
        grid=(s // w,),
        in_specs=[
            pl.BlockSpec((w, BRANCH), lambda i: (i, 0)),
            pl.BlockSpec((w, kvw), lambda i: (i, kblk)),
            pl.BlockSpec((w, kvw), lambda i: (prev(i), kblk)),
            pl.BlockSpec((w, kvw), lambda i: (i, kblk + 1)),
            pl.BlockSpec((w, kvw), lambda i: (prev(i), kblk + 1)),
            pl.BlockSpec((w, LANES), lambda i: (i, 0)),
            pl.BlockSpec((1, w), lambda i: (0, i)),
            pl.BlockSpec((1, w), lambda i: (0, prev(i))),
            pl.BlockSpec((SWA_Q_HEADS, 1, LANES), lambda i: (0, 0, 0)),
            pl.BlockSpec((SWA_Q_HEADS, 1, LANES), lambda i: (0, 0, 0)),
        ],
        out_specs=pl.BlockSpec((w, BRANCH), lambda i: (i, 0)),
        out_shape=jax.ShapeDtypeStruct((s, BRANCH), jnp.bfloat16),
        compiler_params=_params("arbitrary"),
        name="swa_attention",
    )(proj, proj, proj, proj, proj, pos_col, pos_row, pos_row, slopes, sinks)


def _silu(g):
    return g / (1.0 + jnp.exp(-g))


def _out_kernel(*refs, conv):
    if conv:
        (x_ref, bg_ref, cg_ref, u_ref, hc_ref, hu_ref, cw_ref,
         gmix_ref, qm_ref, gmem_ref, kbd_ref, vobd_ref, wo_ref, gpost_ref, o_ref) = refs
        i = pl.program_id(0)
        z = cg_ref[...].astype(jnp.float32) * u_ref[...].astype(jnp.float32)
        zh = hc_ref[...].astype(jnp.float32) * hu_ref[...].astype(jnp.float32)
        zh = zh * (i > 0).astype(jnp.float32)
        row = lax.broadcasted_iota(jnp.int32, z.shape, 0)
        z1 = jnp.where(row == 0, zh[7:8], pltpu.roll(z, 1, 0))
        z2 = jnp.where(row == 0, zh[6:7], jnp.where(row == 1, zh[7:8], pltpu.roll(z, 2, 0)))
        cw = cw_ref[...]
        mix = bg_ref[...].astype(jnp.float32) * (cw[0:1] * z2 + cw[1:2] * z1 + cw[2:3] * z)
    else:
        (x_ref, mix_ref, gmix_ref, qm_ref, gmem_ref, kbd_ref, vobd_ref, wo_ref, gpost_ref,
         o_ref) = refs
        mix = mix_ref[...]

    s = lax.dot_general(qm_ref[...], kbd_ref[0], (((1,), (1,)), ((), ())),
                        preferred_element_type=jnp.float32)
    ps = []
    for h in range(MEM_HEADS):
        sh = s[:, h * MEM_LEN:(h + 1) * MEM_LEN]
        ps.append(jnp.exp(sh - jnp.max(sh, axis=1, keepdims=True)).astype(jnp.bfloat16))
    nd = jnp.dot(jnp.concatenate(ps, axis=1), vobd_ref[0], preferred_element_type=jnp.float32)
    mem_out = nd[:, :MEM_WIDTH] / nd[:, MEM_WIDTH:]

    y_mix = mix.astype(jnp.bfloat16) * _silu(gmix_ref[...])
    y_mem = mem_out.astype(jnp.bfloat16) * _silu(gmem_ref[...])
    y = (jnp.dot(y_mix, wo_ref[:BRANCH, :], preferred_element_type=jnp.float32)
         + jnp.dot(y_mem, wo_ref[BRANCH:, :], preferred_element_type=jnp.float32))
    r = lax.rsqrt(jnp.mean(y * y, axis=-1, keepdims=True) + EPS)
    o_ref[...] = x_ref[...] + y * r * gpost_ref[...]


def _out_layer(x, proj, mix, conv_w, kbd, vobd, layer, w_out, g_post, cols):
    s, d = x.shape
    tm = OUT_TM
    conv = mix is None
    row_blk = lambda width, off: pl.BlockSpec((tm, width), lambda i: (i, off // width))
    in_specs = [pl.BlockSpec((tm, d), lambda i: (i, 0))]
    args = [x]
    if conv:
        halo = lambda off: pl.BlockSpec(
            (8, BRANCH), lambda i: (jnp.maximum(i * (tm // 8) - 1, 0), off // BRANCH))
        in_specs += [row_blk(BRANCH, cols["a"]), row_blk(BRANCH, cols["b"]), row_blk(BRANCH, cols["c"]),
                     halo(cols["b"]), halo(cols["c"]),
                     pl.BlockSpec((CONV_WIDTH, BRANCH), lambda i: (0, 0))]
        args += [proj, proj, proj, proj, proj, conv_w]
    else:
        in_specs += [pl.BlockSpec((tm, BRANCH), lambda i: (i, 0))]
        args += [mix]
    in_specs += [
        row_blk(BRANCH, cols["gate_mix"]), row_blk(MEM_WIDTH, cols["q_mem"]),
        row_blk(MEM_WIDTH, cols["gate_mem"]),
        pl.BlockSpec((1,) + kbd.shape[1:], lambda i: (layer, 0, 0)),
        pl.BlockSpec((1,) + vobd.shape[1:], lambda i: (layer, 0, 0)),
        pl.BlockSpec((GATE_WIDTH, d), lambda i: (0, 0)),
        pl.BlockSpec((1, d), lambda i: (0, 0)),
    ]
    args += [proj, proj, proj, kbd, vobd, w_out, g_post]
    return pl.pallas_call(
        functools.partial(_out_kernel, conv=conv),
        grid=(s // tm,),
        in_specs=in_specs,
        out_specs=pl.BlockSpec((tm, d), lambda i: (i, 0)),
        out_shape=jax.ShapeDtypeStruct((s, d), jnp.float32),
        compiler_params=_params("arbitrary"),
        name="out_conv" if conv else "out_attn",
    )(*args)


def _relayout_w_in(w, kind):
    if kind == 2:
        q, k, v, q_mem, gate = jnp.split(w, [2048, 2304, 2560, 2816], axis=1)
        parts = [q, gate[:, :BRANCH], k, v, q_mem, gate[:, BRANCH:]]
        cols = dict(a=0, gate_mix=2048, k=4096, v=4352, q_mem=4608, gate_mem=4864)
    else:
        a, b, c, q_mem, gate = jnp.split(w, [2048, 4096, 6144, 6400], axis=1)
        parts = [a, b, c, gate[:, :BRANCH], q_mem, gate[:, BRANCH:]]
        cols = dict(a=0, b=2048, c=4096, gate_mix=6144, q_mem=8192, gate_mem=8448)
    return jnp.concatenate(parts, axis=1).astype(jnp.bfloat16), cols


def _alibi_slopes(n_heads):
    return 2.0 ** (-ALIBI_MAX_BIAS * jnp.arange(1, n_heads + 1, dtype=jnp.float32) / n_heads)


def kernel(x, mem, positions, norm_pre_0, norm_post_0, norm_mem_0, w_in_0, w_mem_kv_0, conv_w_0, w_out_0, norm_pre_1, norm_post_1, norm_mem_1, w_in_1, w_mem_kv_1, lambda_q1_1, lambda_k1_1, lambda_q2_1, lambda_k2_1, subln_1, w_out_1, norm_pre_2, norm_post_2, norm_mem_2, w_in_2, w_mem_kv_2, sinks_2, w_out_2, norm_pre_3, norm_post_3, norm_mem_3, w_in_3, w_mem_kv_3, conv_w_3, w_out_3):
    b, s, d = x.shape
    assert b == 1 and s == SEQ and d == D_MODEL
    xs = x.reshape(s, d)
    pos_f = positions.reshape(s).astype(jnp.float32)
    pos_row = pos_f.reshape(1, s)
    pos_col = jnp.broadcast_to(pos_f[:, None], (s, LANES))

    pre = [norm_pre_0, norm_pre_1, norm_pre_2, norm_pre_3]
    post = [norm_post_0, norm_post_1, norm_post_2, norm_post_3]
    w_in = [w_in_0, w_in_1, w_in_2, w_in_3]
    w_out = [w_out_0, w_out_1, w_out_2, w_out_3]
    conv_w = {0: conv_w_0, 3: conv_w_3}

    mem_gain = jnp.stack([norm_mem_0, norm_mem_1, norm_mem_2, norm_mem_3]).reshape(DEPTH, 1, d)
    mem_w = jnp.stack([w_mem_kv_0, w_mem_kv_1, w_mem_kv_2, w_mem_kv_3]).astype(jnp.bfloat16)
    kbd, vobd = _mem_kv(mem.reshape(MEM_LEN, d), mem_gain, mem_w)

    for layer in range(DEPTH):
        kind = layer % 3
        w, cols = _relayout_w_in(w_in[layer], kind)
        proj = _norm_proj(xs, pre[layer].reshape(1, d), w)
        if kind == 0:
            mix = None
        elif kind == 1:
            slopes = jnp.broadcast_to(_alibi_slopes(DIFF_HEADS)[:, None, None], (DIFF_HEADS, 8, LANES))
            lam_init = 0.8 - 0.6 * math.exp(-0.3 * layer)
            mix = _diff_attention(
                proj, pos_col, slopes,
                lambda_q1_1.reshape(1, -1), lambda_k1_1.reshape(1, -1),
                lambda_q2_1.reshape(1, -1), lambda_k2_1.reshape(1, -1),
                subln_1.reshape(1, -1), lam_init)
        else:
            slopes = jnp.broadcast_to(_alibi_slopes(SWA_Q_HEADS)[:, None, None], (SWA_Q_HEADS, 1, LANES))
            sinks = jnp.broadcast_to(sinks_2.astype(jnp.float32)[:, None, None], (SWA_Q_HEADS, 1, LANES))
            mix = _swa_attention(proj, pos_col, pos_row, slopes, sinks)
        xs = _out_layer(xs, proj, mix, conv_w.get(layer), kbd, vobd, layer,
                        w_out[layer].astype(jnp.bfloat16), post[layer].reshape(1, d), cols)
    return xs.reshape(b, s, d)
```

```python
import functools
import math

import jax
import jax.numpy as jnp
from jax import lax
from jax.experimental import pallas as pl
from jax.experimental.pallas import tpu as pltpu

D_MODEL = 1024
SEQ = 16384
DEPTH = 4
MEM_LEN = 256
BRANCH = 2048
CONV_WIDTH = 3
DIFF_HEAD_DIM = 64
DIFF_HEADS = 16
SWA_HEAD_DIM = 64
SWA_Q_HEADS = 32
SWA_KV_HEADS = 4
SWA_GROUP = SWA_Q_HEADS // SWA_KV_HEADS
WINDOW = 128
MEM_HEADS = 4
MEM_HEAD_DIM = 64
MEM_WIDTH = 256
GATE_WIDTH = BRANCH + MEM_WIDTH
ALIBI_MAX_BIAS = 8.0
EPS = 1e-6
NEG_INF = -1e30

LANES = 128
VMEM_LIMIT_BYTES = 56 * 1024 * 1024

PROJ_TM = 512
OUT_TM = 512
DIFF_TQ = 512
DIFF_TK = 256
DIFF_GROUP = 1
DIFF_UNROLLS = (16, 8, 4, 2)
FEAT_RADIX = 128.0
FEAT_DUMMY_LANE = 6
LOG2E = math.log2(math.e)


def _params(*sem):
    return pltpu.CompilerParams(dimension_semantics=sem, vmem_limit_bytes=VMEM_LIMIT_BYTES)


def _mem_kv_kernel(mem_ref, g_ref, w_ref, kbd_ref, vobd_ref):
    m = mem_ref[...]
    r = lax.rsqrt(jnp.mean(m * m, axis=-1, keepdims=True) + EPS)
    mn = (m * r * g_ref[0]).astype(jnp.bfloat16)
    kv = jnp.dot(mn, w_ref[0], preferred_element_type=jnp.float32)
    km = kv[:, :MEM_WIDTH] * (MEM_HEAD_DIM ** -0.5)
    vm = kv[:, MEM_WIDTH:]
    head_of_lane = lax.broadcasted_iota(jnp.int32, (MEM_LEN, MEM_WIDTH), 1) // MEM_HEAD_DIM
    for h in range(MEM_HEADS):
        sel = head_of_lane == h
        rows = pl.ds(h * MEM_LEN, MEM_LEN)
        kbd_ref[0, rows, :] = jnp.where(sel, km, 0.0).astype(jnp.bfloat16)
        vobd_ref[0, rows, :MEM_WIDTH] = jnp.where(sel, vm, 0.0).astype(jnp.bfloat16)
        vobd_ref[0, rows, MEM_WIDTH:] = jnp.where(sel, 1.0, 0.0).astype(jnp.bfloat16)


def _mem_kv(mem, gains, weights):
    n_layers = gains.shape[0]
    rows = MEM_HEADS * MEM_LEN
    return pl.pallas_call(
        _mem_kv_kernel,
        grid=(n_layers,),
        in_specs=[
            pl.BlockSpec((MEM_LEN, D_MODEL), lambda l: (0, 0)),
            pl.BlockSpec((1, 1, D_MODEL), lambda l: (l, 0, 0)),
            pl.BlockSpec((1, D_MODEL, 2 * MEM_WIDTH), lambda l: (l, 0, 0)),
        ],
        out_specs=[
            pl.BlockSpec((1, rows, MEM_WIDTH), lambda l: (l, 0, 0)),
            pl.BlockSpec((1, rows, 2 * MEM_WIDTH), lambda l: (l, 0, 0)),
        ],
        out_shape=[
            jax.ShapeDtypeStruct((n_layers, rows, MEM_WIDTH), jnp.bfloat16),
            jax.ShapeDtypeStruct((n_layers, rows, 2 * MEM_WIDTH), jnp.bfloat16),
        ],
        compiler_params=_params("arbitrary"),
        name="mem_kv",
    )(mem, gains, weights)


def _proj_kernel(x_ref, g_ref, w_ref, o_ref):
    x = x_ref[...]
    r = lax.rsqrt(jnp.mean(x * x, axis=-1, keepdims=True) + EPS)
    h = (x * r * g_ref[...]).astype(jnp.bfloat16)
    o_ref[...] = jnp.dot(h, w_ref[...], preferred_element_type=jnp.float32).astype(o_ref.dtype)


def _norm_proj(x, gain, w):
    s, d = x.shape
    c = w.shape[1]
    n_col = 2
    tn = c // n_col
    assert tn * n_col == c and tn % LANES == 0 and s % PROJ_TM == 0
    return pl.pallas_call(
        _proj_kernel,
        grid=(n_col, s // PROJ_TM),
        in_specs=[
            pl.BlockSpec((PROJ_TM, d), lambda j, i: (i, 0)),
            pl.BlockSpec((1, d), lambda j, i: (0, 0)),
            pl.BlockSpec((d, tn), lambda j, i: (0, j)),
        ],
        out_specs=pl.BlockSpec((PROJ_TM, tn), lambda j, i: (i, j)),
        out_shape=jax.ShapeDtypeStruct((s, c), jnp.bfloat16),
        compiler_params=_params("arbitrary", "arbitrary"),
        name="norm_proj",
    )(x, gain, w)


def _diff_attn_kernel(q_ref, k_ref, v_ref, posk_ref, slope_ref, lq1_ref, lk1_ref, lq2_ref, lk2_ref,
                      subln_ref, o_ref, m_ref, acc_ref, s_ref, p_ref, alpha_ref, kf_ref, *, lam_init):
    tq, tk = DIFF_TQ, DIFF_TK
    group = range(DIFF_GROUP)
    i = pl.program_id(1)
    d = DIFF_HEAD_DIM
    n_blocks = SEQ // tk
    reps = tk // LANES
    n_diag = tq // tk
    n_full = i * n_diag
    head_lanes = [slice(g * LANES, (g + 1) * LANES) for g in group]

    @pl.when(i == 0)
    def _build_key_features():
        lane_t = lax.broadcasted_iota(jnp.int32, (tk, LANES), 1)

        def fill(t, carry):
            pos = posk_ref[pl.ds(pl.multiple_of(t * tk, tk), tk), :]
            hi = jnp.floor(pos * (1.0 / FEAT_RADIX))
            lo = pos - hi * FEAT_RADIX
            kf_ref[t] = jnp.where(lane_t < 3, hi, jnp.where(lane_t < 6, lo, 0.0)).astype(jnp.bfloat16)
            return carry

        lax.fori_loop(0, n_blocks, fill, 0)
        kf_ref[n_blocks] = jnp.where(lane_t == FEAT_DUMMY_LANE, 1.0, 0.0).astype(jnp.bfloat16)

    flane = lax.broadcasted_iota(jnp.int32, (1, LANES), 1)
    qqs = []
    for g in group:
        q = q_ref[:, head_lanes[g]].astype(jnp.float32) * (d ** -0.5 * LOG2E)
        lane = lax.broadcasted_iota(jnp.int32, q.shape, 1)
        qq = jnp.concatenate([jnp.where(lane < d, q, 0.0), jnp.where(lane < d, 0.0, q)],
                             axis=0).astype(jnp.bfloat16)
        slope = slope_ref[g, 0:1, :] * LOG2E
        s1 = slope.astype(jnp.bfloat16).astype(jnp.float32)
        s2 = (slope - s1).astype(jnp.bfloat16).astype(jnp.float32)
        s3 = (slope - s1 - s2).astype(jnp.bfloat16).astype(jnp.float32)
        qfeat = jnp.zeros((1, LANES), jnp.float32)
        for f, val in enumerate([s1 * FEAT_RADIX, s2 * FEAT_RADIX, s3 * FEAT_RADIX, s1, s2, s3]):
            qfeat = jnp.where(flane == f, val, qfeat)
        qfeat = jnp.where(flane == FEAT_DUMMY_LANE, NEG_INF, qfeat)
        qqs.append(jnp.concatenate(
            [qq, jnp.broadcast_to(qfeat, (2 * tq, LANES)).astype(jnp.bfloat16)], axis=1))
        m_ref[g] = jnp.full(m_ref.shape[1:], NEG_INF, dtype=jnp.float32)
        acc_ref[g] = jnp.zeros(acc_ref.shape[1:], dtype=jnp.float32)
        if n_diag == 1:
            p_ref[2 * g + 1] = jnp.zeros(p_ref.shape[1:], dtype=p_ref.dtype)
            alpha_ref[2 * g + 1] = jnp.ones(alpha_ref.shape[1:], dtype=jnp.float32)

    ones = jnp.ones((tk, LANES), dtype=jnp.bfloat16)

    def keys(g, t_key, t_feat):
        rows = pl.ds(pl.multiple_of(t_key * tk, tk), tk)
        return jnp.concatenate([k_ref[rows, head_lanes[g]], kf_ref[t_feat]], axis=1)

    def logits(g, t_key, t_feat):
        return lax.dot_general(qqs[g], keys(g, t_key, t_feat), (((1,), (1,)), ((), ())),
                               preferred_element_type=jnp.float32)

    def softmax_step(g, s):
        m_prev = m_ref[g]
        m_new = jnp.maximum(m_prev, jnp.max(s, axis=1, keepdims=True))
        m_ref[g] = m_new
        p = jnp.exp2(s - jnp.concatenate([m_new] * reps, axis=1)).astype(jnp.bfloat16)
        return p, jnp.exp2(m_prev - m_new)

    def accumulate(g, t, p, alpha):
        vb = v_ref[pl.ds(pl.multiple_of(t * tk, tk), tk), head_lanes[g]]
        pv = jnp.dot(p, jnp.concatenate([vb, ones], axis=1), preferred_element_type=jnp.float32)
        acc_ref[g] = acc_ref[g] * jnp.concatenate([alpha, alpha], axis=1) + pv

    row = lax.broadcasted_iota(jnp.int32, (2 * tq, tk), 0)
    row = jnp.where(row >= tq, row - tq, row)
    col = lax.broadcasted_iota(jnp.int32, (2 * tq, tk), 1)

    def stage_a_boundary(b):
        for g in group:
            s = logits(g, n_full + b, n_full + b)
            s_ref[2 * g + b % 2] = jnp.where(col + b * tk <= row, s, NEG_INF)

    def stage_a(b, buf):
        valid = b < n_full + n_diag
        t_key = jnp.clip(b - n_diag, 0, n_blocks - 1)
        t_feat = jnp.where(valid, b - n_diag, n_blocks)
        for g in group:
            s_ref[2 * g + buf] = logits(g, t_key, t_feat)

    def stage_b(buf):
        for g in group:
            p, alpha = softmax_step(g, s_ref[2 * g + buf])
            p_ref[2 * g + buf] = p
            alpha_ref[2 * g + buf] = alpha

    def stage_c(b, buf):
        t = jnp.clip(jnp.where(b < n_diag, n_full + b, b - n_diag), 0, n_blocks - 1)
        for g in group:
            accumulate(g, t, p_ref[2 * g + buf], alpha_ref[2 * g + buf])

    stage_a_boundary(0)
    for b in range(1, n_diag):
        if b >= 2:
            stage_c(b - 2, b % 2)
        stage_b((b - 1) % 2)
        stage_a_boundary(b)

    def run_steps(first, trips, unroll):
        def body(u, carry):
            for r in range(unroll):
                b = first + unroll * u + r + 1
                par = (n_diag + r) % 2
                stage_c(b - 2, par)
                stage_b(1 - par)
                stage_a(b, par)
            return carry
        lax.fori_loop(0, trips, body, 0)

    last_step = n_full + n_diag + 1
    done = n_diag - 1
    for size in DIFF_UNROLLS[:-1]:
        trips = (last_step - done) // size
        run_steps(done, trips, size)
        done = done + trips * size
    run_steps(done, (last_step - done + DIFF_UNROLLS[-1] - 1) // DIFF_UNROLLS[-1], DIFF_UNROLLS[-1])

    lam = (jnp.exp(jnp.sum(lq1_ref[...] * lk1_ref[...], axis=1, keepdims=True))
           - jnp.exp(jnp.sum(lq2_ref[...] * lk2_ref[...], axis=1, keepdims=True)) + lam_init)
    for g in group:
        acc = acc_ref[g]
        o = acc[:, :LANES] / acc[:, LANES:]
        o = o[:tq] - lam * o[tq:]
        r = lax.rsqrt(jnp.mean(o * o, axis=-1, keepdims=True) + EPS)
        o_ref[:, head_lanes[g]] = (o * r * subln_ref[...] * (1.0 - lam_init)).astype(o_ref.dtype)


def _diff_attention(proj, pos_col, slopes, lq1, lk1, lq2, lk2, subln, lam_init):
    s = proj.shape[0]
    tq, tk, n_group = DIFF_TQ, DIFF_TK, DIFF_GROUP
    assert tq % tk == 0 and s % tq == 0 and all(u % 2 == 0 for u in DIFF_UNROLLS)
    assert DIFF_HEADS % n_group == 0
    gw = n_group * LANES
    gb = BRANCH // gw
    vec = lambda n: pl.BlockSpec((1, n), lambda h, i: (0, 0))
    resident = lambda shape, index_map: pl.BlockSpec(shape, index_map, pipeline_mode=pl.Buffered(1))
    return pl.pallas_call(
        functools.partial(_diff_attn_kernel, lam_init=lam_init),
        grid=(DIFF_HEADS // n_group, s // tq),
        in_specs=[
            pl.BlockSpec((tq, gw), lambda h, i: (i, h)),
            resident((s, gw), lambda h, i: (0, gb + h)),
            resident((s, gw), lambda h, i: (0, 2 * gb + h)),
            resident((s, LANES), lambda h, i: (0, 0)),
            pl.BlockSpec((n_group, 8, LANES), lambda h, i: (h, 0, 0)),
            vec(DIFF_HEAD_DIM), vec(DIFF_HEAD_DIM), vec(DIFF_HEAD_DIM), vec(DIFF_HEAD_DIM),
            vec(2 * DIFF_HEAD_DIM),
        ],
        out_specs=pl.BlockSpec((tq, gw), lambda h, i: (i, h)),
        out_shape=jax.ShapeDtypeStruct((s, BRANCH), jnp.bfloat16),
        scratch_shapes=[
            pltpu.VMEM((n_group, 2 * tq, LANES), jnp.float32),
            pltpu.VMEM((n_group, 2 * tq, 2 * LANES), jnp.float32),
            pltpu.VMEM((2 * n_group, 2 * tq, tk), jnp.float32),
            pltpu.VMEM((2 * n_group, 2 * tq, tk), jnp.bfloat16),
            pltpu.VMEM((2 * n_group, 2 * tq, LANES), jnp.float32),
            pltpu.VMEM((s // tk + 1, tk, LANES), jnp.bfloat16),
        ],
        compiler_params=_params("arbitrary", "arbitrary"),
        name="diff_attention",
    )(proj, proj, proj, pos_col, slopes, lq1, lk1, lq2, lk2, subln)


def _swap_halves(x):
    half = LANES // 2
    return jnp.concatenate([x[:, half:], x[:, :half]], axis=1)


def _swa_kernel(q_ref, kc_ref, kp_ref, vc_ref, vp_ref, pq_ref, pkc_ref, pkp_ref, slope_ref,
                sink_ref, o_ref):
    i = pl.program_id(0)
    w = WINDOW
    d = SWA_HEAD_DIM
    cols_per_kv = SWA_GROUP // 2
    rel = jnp.concatenate([pq_ref[...] - pkp_ref[...], pq_ref[...] - pkc_ref[...]], axis=1)
    col = lax.broadcasted_iota(jnp.int32, (w, 2 * w), 1)
    valid = (rel >= 0.0) & (rel < float(w)) & ((col >= w) | (i > 0))
    rel_masked = jnp.where(valid, rel, -NEG_INF)
    lane_half = lax.broadcasted_iota(jnp.int32, (2 * w, LANES), 1) // d
    out_half = lax.broadcasted_iota(jnp.int32, (w, LANES), 1) // d
    ones = jnp.ones((2 * w, LANES), dtype=jnp.bfloat16)

    pairs = []
    for pair in range(SWA_KV_HEADS // 2):
        lanes = slice(pair * LANES, (pair + 1) * LANES)
        kk = jnp.concatenate([kp_ref[:, lanes], kc_ref[:, lanes]], axis=0)
        kk = (kk.astype(jnp.float32) * (d ** -0.5 * LOG2E)).astype(jnp.bfloat16)
        vv = jnp.concatenate([vp_ref[:, lanes], vc_ref[:, lanes]], axis=0)
        pairs.append((kk, _swap_halves(kk), vv, _swap_halves(vv)))

    def operands(n):
        c, t = divmod(n, 2)
        pair, e = divmod(c, 2)
        kk, kk_sw, vv, vv_sw = pairs[pair]
        zero = jnp.zeros_like(kk)
        if t == 0:
            rhs, val = jnp.where(lane_half == e, kk, zero), vv
        else:
            rhs, val = jnp.where(lane_half == e, zero, kk_sw), vv_sw
        heads = [c * SWA_GROUP + 2 * u + (e if t == 0 else 1 - e) for u in range(cols_per_kv)]
        return c, e, rhs, val, heads

    def logits(n):
        c, _, rhs, _, _ = operands(n)
        lhs = jnp.concatenate(
            [q_ref[:, pl.ds((c * cols_per_kv + u) * LANES, LANES)] for u in range(cols_per_kv)], axis=0)
        return lax.dot_general(lhs, rhs, (((1,), (1,)), ((), ())), preferred_element_type=jnp.float32)

    def softmax(n, s):
        heads = operands(n)[4]
        ex, stats = [], []
        for u, head in enumerate(heads):
            slope = slope_ref[head] * LOG2E
            su = s[u * w:(u + 1) * w] - jnp.concatenate([slope, slope], axis=1) * rel_masked
            sink = sink_ref[head] * LOG2E
            m = jnp.maximum(jnp.max(su, axis=1, keepdims=True), sink)
            ex.append(jnp.exp2(su - jnp.concatenate([m, m], axis=1)).astype(jnp.bfloat16))
            stats.append(jnp.exp2(sink - m))
        return jnp.concatenate(ex, axis=0), stats

    def values(n, ex, stats):
        val = operands(n)[3]
        acc = jnp.dot(ex, jnp.concatenate([val, ones], axis=1), preferred_element_type=jnp.float32)
        return [acc[u * w:(u + 1) * w, :LANES] / (acc[u * w:(u + 1) * w, LANES:] + stats[u])
                for u in range(cols_per_kv)]

    n_batches = 2 * SWA_KV_HEADS
    ahead = 1
    s = {n: logits(n) for n in range(ahead)}
    outs = {}
    for n in range(n_batches):
        if n + ahead < n_batches:
            s[n + ahead] = logits(n + ahead)
        ex, stats = softmax(n, s.pop(n))
        outs[n] = values(n, ex, stats)
        if n % 2 == 1:
            c, e = operands(n)[:2]
            merged = [jnp.where(out_half == e, outs[n - 1][u], outs[n][u]) for u in range(cols_per_kv)]
            o_ref[:, pl.ds(c * cols_per_kv * LANES, cols_per_kv * LANES)] = (
                jnp.concatenate(merged, axis=1).astype(o_ref.dtype))


def _swa_attention(proj, pos_col, pos_row, slopes, sinks):
    s = proj.shape[0]
    w = WINDOW
    kvw = SWA_KV_HEADS * SWA_HEAD_DIM
    kblk = (2 * BRANCH) // kvw
    prev = lambda i: jnp.maximum(i - 1, 0)
    return pl.pallas_call(
        _swa_kernel,
        grid=(s // w,),
        in_specs=[
            pl.BlockSpec((w, BRANCH), lambda i: (i, 0)),
            pl.BlockSpec((w, kvw), lambda i: (i, kblk)),
            pl.BlockSpec((w, kvw), lambda i: (prev(i), kblk)),
            pl.BlockSpec((w, kvw), lambda i: (i, kblk + 1)),
            pl.BlockSpec((w, kvw), lambda i: (prev(i), kblk + 1)),
            pl.BlockSpec((w, LANES), lambda i: (i, 0)),
            pl.BlockSpec((1, w), lambda i: (0, i)),
            pl.BlockSpec((1, w), lambda i: (0, prev(i))),
            pl.BlockSpec((SWA_Q_HEADS, 1, LANES), lambda i: (0, 0, 0)),
            pl.BlockSpec((SWA_Q_HEADS, 1, LANES), lambda i: (0, 0, 0)),
        ],
        out_specs=pl.BlockSpec((w, BRANCH), lambda i: (i, 0)),
        out_shape=jax.ShapeDtypeStruct((s, BRANCH), jnp.bfloat16),
        compiler_params=_params("arbitrary"),
        name="swa_attention",
    )(proj, proj, proj, proj, proj, pos_col, pos_row, pos_row, slopes, sinks)


def _silu(g):
    return g / (1.0 + jnp.exp(-g))


def _out_kernel(*refs, conv):
    if conv:
        (x_ref, bg_ref, cg_ref, u_ref, hc_ref, hu_ref, cw_ref,
         gmix_ref, qm_ref, gmem_ref, kbd_ref, vobd_ref, wo_ref, gpost_ref, o_ref) = refs
        i = pl.program_id(0)
        z = cg_ref[...].astype(jnp.float32) * u_ref[...].astype(jnp.float32)
        zh = hc_ref[...].astype(jnp.float32) * hu_ref[...].astype(jnp.float32)
        zh = zh * (i > 0).astype(jnp.float32)
        row = lax.broadcasted_iota(jnp.int32, (8, BRANCH), 0)
        z1 = pltpu.roll(z, 1, 0)
        z2 = pltpu.roll(z, 2, 0)
        z1 = jnp.concatenate([jnp.where(row == 0, zh[7:8], z1[:8]), z1[8:]], axis=0)
        z2 = jnp.concatenate(
            [jnp.where(row == 0, zh[6:7], jnp.where(row == 1, zh[7:8], z2[:8])), z2[8:]], axis=0)
        cw = cw_ref[...]
        mix = bg_ref[...].astype(jnp.float32) * (cw[0:1] * z2 + cw[1:2] * z1 + cw[2:3] * z)
    else:
        (x_ref, mix_ref, gmix_ref, qm_ref, gmem_ref, kbd_ref, vobd_ref, wo_ref, gpost_ref,
         o_ref) = refs
        mix = mix_ref[...]

    s = lax.dot_general(qm_ref[...], kbd_ref[0], (((1,), (1,)), ((), ())),
                        preferred_element_type=jnp.float32)
    ps = []
    for h in range(MEM_HEADS):
        sh = s[:, h * MEM_LEN:(h + 1) * MEM_LEN]
        ps.append(jnp.exp(sh - jnp.max(sh, axis=1, keepdims=True)).astype(jnp.bfloat16))
    nd = jnp.dot(jnp.concatenate(ps, axis=1), vobd_ref[0], preferred_element_type=jnp.float32)
    mem_out = nd[:, :MEM_WIDTH] / nd[:, MEM_WIDTH:]

    y_mix = mix.astype(jnp.bfloat16) * _silu(gmix_ref[...])
    y_mem = mem_out.astype(jnp.bfloat16) * _silu(gmem_ref[...])
    y = (jnp.dot(y_mix, wo_ref[:BRANCH, :], preferred_element_type=jnp.float32)
         + jnp.dot(y_mem, wo_ref[BRANCH:, :], preferred_element_type=jnp.float32))
    r = lax.rsqrt(jnp.mean(y * y, axis=-1, keepdims=True) + EPS)
    o_ref[...] = x_ref[...] + y * r * gpost_ref[...]


def _out_layer(x, proj, mix, conv_w, kbd, vobd, layer, w_out, g_post, cols):
    s, d = x.shape
    tm = OUT_TM
    conv = mix is None
    row_blk = lambda width, off: pl.BlockSpec((tm, width), lambda i: (i, off // width))
    in_specs = [pl.BlockSpec((tm, d), lambda i: (i, 0))]
    args = [x]
    if conv:
        halo = lambda off: pl.BlockSpec(
            (8, BRANCH), lambda i: (jnp.maximum(i * (tm // 8) - 1, 0), off // BRANCH))
        in_specs += [row_blk(BRANCH, cols["a"]), row_blk(BRANCH, cols["b"]), row_blk(BRANCH, cols["c"]),
                     halo(cols["b"]), halo(cols["c"]),
                     pl.BlockSpec((CONV_WIDTH, BRANCH), lambda i: (0, 0))]
        args += [proj, proj, proj, proj, proj, conv_w]
    else:
        in_specs += [pl.BlockSpec((tm, BRANCH), lambda i: (i, 0))]
        args += [mix]
    in_specs += [
        row_blk(BRANCH, cols["gate_mix"]), row_blk(MEM_WIDTH, cols["q_mem"]),
        row_blk(MEM_WIDTH, cols["gate_mem"]),
        pl.BlockSpec((1,) + kbd.shape[1:], lambda i: (layer, 0, 0)),
        pl.BlockSpec((1,) + vobd.shape[1:], lambda i: (layer, 0, 0)),
        pl.BlockSpec((GATE_WIDTH, d), lambda i: (0, 0)),
        pl.BlockSpec((1, d), lambda i: (0, 0)),
    ]
    args += [proj, proj, proj, kbd, vobd, w_out, g_post]
    return pl.pallas_call(
        functools.partial(_out_kernel, conv=conv),
        grid=(s // tm,),
        in_specs=in_specs,
        out_specs=pl.BlockSpec((tm, d), lambda i: (i, 0)),
        out_shape=jax.ShapeDtypeStruct((s, d), jnp.float32),
        compiler_params=_params("arbitrary"),
        name="out_conv" if conv else "out_attn",
    )(*args)


def _relayout_w_in(w, kind):
    if kind == 2:
        q, k, v, q_mem, gate = jnp.split(w, [2048, 2304, 2560, 2816], axis=1)
        parts = [q, gate[:, :BRANCH], k, v, q_mem, gate[:, BRANCH:]]
        cols = dict(a=0, gate_mix=2048, k=4096, v=4352, q_mem=4608, gate_mem=4864)
    else:
        a, b, c, q_mem, gate = jnp.split(w, [2048, 4096, 6144, 6400], axis=1)
        parts = [a, b, c, gate[:, :BRANCH], q_mem, gate[:, BRANCH:]]
        cols = dict(a=0, b=2048, c=4096, gate_mix=6144, q_mem=8192, gate_mem=8448)
    return jnp.concatenate([p.astype(jnp.bfloat16) for p in parts], axis=1), cols


def _alibi_slopes(n_heads):
    return 2.0 ** (-ALIBI_MAX_BIAS * jnp.arange(1, n_heads + 1, dtype=jnp.float32) / n_heads)


def kernel(x, mem, positions, norm_pre_0, norm_post_0, norm_mem_0, w_in_0, w_mem_kv_0, conv_w_0, w_out_0, norm_pre_1, norm_post_1, norm_mem_1, w_in_1, w_mem_kv_1, lambda_q1_1, lambda_k1_1, lambda_q2_1, lambda_k2_1, subln_1, w_out_1, norm_pre_2, norm_post_2, norm_mem_2, w_in_2, w_mem_kv_2, sinks_2, w_out_2, norm_pre_3, norm_post_3, norm_mem_3, w_in_3, w_mem_kv_3, conv_w_3, w_out_3):
    b, s, d = x.shape
    assert b == 1 and s == SEQ and d == D_MODEL
    xs = x.reshape(s, d)
    pos_f = positions.reshape(s).astype(jnp.float32)
    pos_row = pos_f.reshape(1, s)
    pos_col = jnp.broadcast_to(pos_f[:, None], (s, LANES))

    pre = [norm_pre_0, norm_pre_1, norm_pre_2, norm_pre_3]
    post = [norm_post_0, norm_post_1, norm_post_2, norm_post_3]
    w_in = [w_in_0, w_in_1, w_in_2, w_in_3]
    w_out = [w_out_0, w_out_1, w_out_2, w_out_3]
    conv_w = {0: conv_w_0, 3: conv_w_3}

    mem_gain = jnp.stack([norm_mem_0, norm_mem_1, norm_mem_2, norm_mem_3]).reshape(DEPTH, 1, d)
    mem_w = jnp.stack([w_mem_kv_0, w_mem_kv_1, w_mem_kv_2, w_mem_kv_3]).astype(jnp.bfloat16)
    kbd, vobd = _mem_kv(mem.reshape(MEM_LEN, d), mem_gain, mem_w)

    for layer in range(DEPTH):
        kind = layer % 3
        w, cols = _relayout_w_in(w_in[layer], kind)
        proj = _norm_proj(xs, pre[layer].reshape(1, d), w)
        if kind == 0:
            mix = None
        elif kind == 1:
            slopes = jnp.broadcast_to(_alibi_slopes(DIFF_HEADS)[:, None, None], (DIFF_HEADS, 8, LANES))
            lam_init = 0.8 - 0.6 * math.exp(-0.3 * layer)
            mix = _diff_attention(
                proj, pos_col, slopes,
                lambda_q1_1.reshape(1, -1), lambda_k1_1.reshape(1, -1),
                lambda_q2_1.reshape(1, -1), lambda_k2_1.reshape(1, -1),
                subln_1.reshape(1, -1), lam_init)
        else:
            slopes = jnp.broadcast_to(_alibi_slopes(SWA_Q_HEADS)[:, None, None], (SWA_Q_HEADS, 1, LANES))
            sinks = jnp.broadcast_to(sinks_2.astype(jnp.float32)[:, None, None], (SWA_Q_HEADS, 1, LANES))
            mix = _swa_attention(proj, pos_col, pos_row, slopes, sinks)
        xs = _out_layer(xs, proj, mix, conv_w.get(layer), kbd, vobd, layer,
                        w_out[layer].astype(jnp.bfloat16), post[layer].reshape(1, d), cols)
    return xs.reshape(b, s, d)
```

```python
import functools
import math

import jax
import jax.numpy as jnp
from jax import lax
from jax.experimental import pallas as pl
from jax.experimental.pallas import tpu as pltpu

D_MODEL = 1024
SEQ = 16384
DEPTH = 4
MEM_LEN = 256
BRANCH = 2048
CONV_WIDTH = 3
DIFF_HEAD_DIM = 64
DIFF_HEADS = 16
SWA_HEAD_DIM = 64
SWA_Q_HEADS = 32
SWA_KV_HEADS = 4
SWA_GROUP = SWA_Q_HEADS // SWA_KV_HEADS
WINDOW = 128
MEM_HEADS = 4
MEM_HEAD_DIM = 64
MEM_WIDTH = 256
GATE_WIDTH = BRANCH + MEM_WIDTH
ALIBI_MAX_BIAS = 8.0
EPS = 1e-6
NEG_INF = -1e30

LANES = 128
VMEM_LIMIT_BYTES = 56 * 1024 * 1024

PROJ_TM = 512
CONV_BLOCK = 256
OUT_TM = 512
DIFF_TQ = 512
DIFF_TK = 256
DIFF_GROUP = 1
DIFF_UNROLLS = (16, 8, 4, 2)
FEAT_RADIX = 128.0
FEAT_DUMMY_LANE = 6
LOG2E = math.log2(math.e)


def _params(*sem):
    return pltpu.CompilerParams(dimension_semantics=sem, vmem_limit_bytes=VMEM_LIMIT_BYTES)


def _mem_kv_kernel(mem_ref, g_ref, w_ref, kbd_ref, vobd_ref):
    m = mem_ref[...]
    r = lax.rsqrt(jnp.mean(m * m, axis=-1, keepdims=True) + EPS)
    mn = (m * r * g_ref[0]).astype(jnp.bfloat16)
    kv = jnp.dot(mn, w_ref[0], preferred_element_type=jnp.float32)
    km = kv[:, :MEM_WIDTH] * (MEM_HEAD_DIM ** -0.5)
    vm = kv[:, MEM_WIDTH:]
    head_of_lane = lax.broadcasted_iota(jnp.int32, (MEM_LEN, MEM_WIDTH), 1) // MEM_HEAD_DIM
    for h in range(MEM_HEADS):
        sel = head_of_lane == h
        rows = pl.ds(h * MEM_LEN, MEM_LEN)
        kbd_ref[0, rows, :] = jnp.where(sel, km, 0.0).astype(jnp.bfloat16)
        vobd_ref[0, rows, :MEM_WIDTH] = jnp.where(sel, vm, 0.0).astype(jnp.bfloat16)
        vobd_ref[0, rows, MEM_WIDTH:] = jnp.where(sel, 1.0, 0.0).astype(jnp.bfloat16)


def _mem_kv(mem, gains, weights):
    n_layers = gains.shape[0]
    rows = MEM_HEADS * MEM_LEN
    return pl.pallas_call(
        _mem_kv_kernel,
        grid=(n_layers,),
        in_specs=[
            pl.BlockSpec((MEM_LEN, D_MODEL), lambda l: (0, 0)),
            pl.BlockSpec((1, 1, D_MODEL), lambda l: (l, 0, 0)),
            pl.BlockSpec((1, D_MODEL, 2 * MEM_WIDTH), lambda l: (l, 0, 0)),
        ],
        out_specs=[
            pl.BlockSpec((1, rows, MEM_WIDTH), lambda l: (l, 0, 0)),
            pl.BlockSpec((1, rows, 2 * MEM_WIDTH), lambda l: (l, 0, 0)),
        ],
        out_shape=[
            jax.ShapeDtypeStruct((n_layers, rows, MEM_WIDTH), jnp.bfloat16),
            jax.ShapeDtypeStruct((n_layers, rows, 2 * MEM_WIDTH), jnp.bfloat16),
        ],
        compiler_params=_params("arbitrary"),
        name="mem_kv",
    )(mem, gains, weights)


def _proj_kernel(x_ref, g_ref, w_ref, o_ref):
    x = x_ref[...]
    r = lax.rsqrt(jnp.mean(x * x, axis=-1, keepdims=True) + EPS)
    h = (x * r * g_ref[...]).astype(jnp.bfloat16)
    o_ref[...] = jnp.dot(h, w_ref[...], preferred_element_type=jnp.float32).astype(o_ref.dtype)


def _norm_proj(x, gain, w):
    s, d = x.shape
    c = w.shape[1]
    n_col = 2
    tn = c // n_col
    assert tn * n_col == c and tn % LANES == 0 and s % PROJ_TM == 0
    return pl.pallas_call(
        _proj_kernel,
        grid=(n_col, s // PROJ_TM),
        in_specs=[
            pl.BlockSpec((PROJ_TM, d), lambda j, i: (i, 0)),
            pl.BlockSpec((1, d), lambda j, i: (0, 0)),
            pl.BlockSpec((d, tn), lambda j, i: (0, j)),
        ],
        out_specs=pl.BlockSpec((PROJ_TM, tn), lambda j, i: (i, j)),
        out_shape=jax.ShapeDtypeStruct((s, c), jnp.bfloat16),
        compiler_params=_params("arbitrary", "arbitrary"),
        name="norm_proj",
    )(x, gain, w)


def _proj_conv_kernel(x_ref, g_ref, w_ref, cw_ref, y_ref, qg_ref, zh_ref, h_ref):
    i = pl.program_id(0)
    x = x_ref[...]
    r = lax.rsqrt(jnp.mean(x * x, axis=-1, keepdims=True) + EPS)
    h_ref[...] = (x * r * g_ref[...]).astype(jnp.bfloat16)
    tm = x.shape[0]
    ch = CONV_BLOCK
    row = lax.broadcasted_iota(jnp.int32, (8, ch), 0)
    for j in range(BRANCH // ch):
        pr = jnp.dot(h_ref[...], w_ref[:, 4 * ch * j:4 * ch * (j + 1)],
                     preferred_element_type=jnp.float32)
        bg, cg, u, gate = (pr[:, n * ch:(n + 1) * ch] for n in range(4))
        z = cg * u
        zh = jnp.where(i > 0, zh_ref[j], 0.0)
        z1 = pltpu.roll(z, 1, 0)
        z2 = pltpu.roll(z, 2, 0)
        z1 = jnp.concatenate([jnp.where(row == 0, zh[7:8], z1[:8]), z1[8:]], axis=0)
        z2 = jnp.concatenate(
            [jnp.where(row == 0, zh[6:7], jnp.where(row == 1, zh[7:8], z2[:8])), z2[8:]], axis=0)
        cw = cw_ref[:, j * ch:(j + 1) * ch]
        mix = bg * (cw[0:1] * z2 + cw[1:2] * z1 + cw[2:3] * z)
        y_ref[:, j * ch:(j + 1) * ch] = (mix * _silu(gate)).astype(y_ref.dtype)
        zh_ref[j] = z[tm - 8:]
    qg_ref[...] = jnp.dot(h_ref[...], w_ref[:, 4 * BRANCH:],
                          preferred_element_type=jnp.float32).astype(qg_ref.dtype)


def _norm_proj_conv(x, gain, w, conv_w):
    s, d = x.shape
    c = w.shape[1]
    assert c == 4 * BRANCH + 2 * MEM_WIDTH and BRANCH % CONV_BLOCK == 0
    return pl.pallas_call(
        _proj_conv_kernel,
        grid=(s // PROJ_TM,),
        in_specs=[
            pl.BlockSpec((PROJ_TM, d), lambda i: (i, 0)),
            pl.BlockSpec((1, d), lambda i: (0, 0)),
            pl.BlockSpec((d, c), lambda i: (0, 0), pipeline_mode=pl.Buffered(1)),
            pl.BlockSpec((CONV_WIDTH, BRANCH), lambda i: (0, 0)),
        ],
        out_specs=[
            pl.BlockSpec((PROJ_TM, BRANCH), lambda i: (i, 0)),
            pl.BlockSpec((PROJ_TM, 2 * MEM_WIDTH), lambda i: (i, 0)),
        ],
        out_shape=[
            jax.ShapeDtypeStruct((s, BRANCH), jnp.bfloat16),
            jax.ShapeDtypeStruct((s, 2 * MEM_WIDTH), jnp.bfloat16),
        ],
        scratch_shapes=[pltpu.VMEM((BRANCH // CONV_BLOCK, 8, CONV_BLOCK), jnp.float32),
                        pltpu.VMEM((PROJ_TM, d), jnp.bfloat16)],
        compiler_params=_params("arbitrary"),
        name="norm_proj_conv",
    )(x, gain, w, conv_w)


def _diff_attn_kernel(q_ref, k_ref, v_ref, posk_ref, slope_ref, lq1_ref, lk1_ref, lq2_ref, lk2_ref,
                      subln_ref, o_ref, m_ref, acc_ref, s_ref, p_ref, alpha_ref, kf_ref, *, lam_init):
    tq, tk = DIFF_TQ, DIFF_TK
    group = range(DIFF_GROUP)
    i = pl.program_id(1)
    d = DIFF_HEAD_DIM
    n_blocks = SEQ // tk
    reps = tk // LANES
    n_diag = tq // tk
    n_full = i * n_diag
    head_lanes = [slice(g * LANES, (g + 1) * LANES) for g in group]

    @pl.when(i == 0)
    def _build_key_features():
        lane_t = lax.broadcasted_iota(jnp.int32, (tk, LANES), 1)

        def fill(t, carry):
            pos = posk_ref[pl.ds(pl.multiple_of(t * tk, tk), tk), :]
            hi = jnp.floor(pos * (1.0 / FEAT_RADIX))
            lo = pos - hi * FEAT_RADIX
            kf_ref[t] = jnp.where(lane_t < 3, hi, jnp.where(lane_t < 6, lo, 0.0)).astype(jnp.bfloat16)
            return carry

        lax.fori_loop(0, n_blocks, fill, 0)
        kf_ref[n_blocks] = jnp.where(lane_t == FEAT_DUMMY_LANE, 1.0, 0.0).astype(jnp.bfloat16)

    flane = lax.broadcasted_iota(jnp.int32, (1, LANES), 1)
    qqs = []
    for g in group:
        q = q_ref[:, head_lanes[g]].astype(jnp.float32) * (d ** -0.5 * LOG2E)
        lane = lax.broadcasted_iota(jnp.int32, q.shape, 1)
        qq = jnp.concatenate([jnp.where(lane < d, q, 0.0), jnp.where(lane < d, 0.0, q)],
                             axis=0).astype(jnp.bfloat16)
        slope = slope_ref[g, 0:1, :] * LOG2E
        s1 = slope.astype(jnp.bfloat16).astype(jnp.float32)
        s2 = (slope - s1).astype(jnp.bfloat16).astype(jnp.float32)
        s3 = (slope - s1 - s2).astype(jnp.bfloat16).astype(jnp.float32)
        qfeat = jnp.zeros((1, LANES), jnp.float32)
        for f, val in enumerate([s1 * FEAT_RADIX, s2 * FEAT_RADIX, s3 * FEAT_RADIX, s1, s2, s3]):
            qfeat = jnp.where(flane == f, val, qfeat)
        qfeat = jnp.where(flane == FEAT_DUMMY_LANE, NEG_INF, qfeat)
        qqs.append(jnp.concatenate(
            [qq, jnp.broadcast_to(qfeat, (2 * tq, LANES)).astype(jnp.bfloat16)], axis=1))
        m_ref[g] = jnp.full(m_ref.shape[1:], NEG_INF, dtype=jnp.float32)
        acc_ref[g] = jnp.zeros(acc_ref.shape[1:], dtype=jnp.float32)
        if n_diag == 1:
            p_ref[2 * g + 1] = jnp.zeros(p_ref.shape[1:], dtype=p_ref.dtype)
            alpha_ref[2 * g + 1] = jnp.ones(alpha_ref.shape[1:], dtype=jnp.float32)

    ones = jnp.ones((tk, LANES), dtype=jnp.bfloat16)

    def keys(g, t_key, t_feat):
        rows = pl.ds(pl.multiple_of(t_key * tk, tk), tk)
        return jnp.concatenate([k_ref[rows, head_lanes[g]], kf_ref[t_feat]], axis=1)

    def logits(g, t_key, t_feat):
        return lax.dot_general(qqs[g], keys(g, t_key, t_feat), (((1,), (1,)), ((), ())),
                               preferred_element_type=jnp.float32)

    def softmax_step(g, s):
        m_prev = m_ref[g]
        m_new = jnp.maximum(m_prev, jnp.max(s, axis=1, keepdims=True))
        m_ref[g] = m_new
        p = jnp.exp2(s - jnp.concatenate([m_new] * reps, axis=1)).astype(jnp.bfloat16)
        return p, jnp.exp2(m_prev - m_new)

    def accumulate(g, t, p, alpha):
        vb = v_ref[pl.ds(pl.multiple_of(t * tk, tk), tk), head_lanes[g]]
        pv = jnp.dot(p, jnp.concatenate([vb, ones], axis=1), preferred_element_type=jnp.float32)
        acc_ref[g] = acc_ref[g] * jnp.concatenate([alpha, alpha], axis=1) + pv

    row = lax.broadcasted_iota(jnp.int32, (2 * tq, tk), 0)
    row = jnp.where(row >= tq, row - tq, row)
    col = lax.broadcasted_iota(jnp.int32, (2 * tq, tk), 1)

    def stage_a_boundary(b):
        for g in group:
            s = logits(g, n_full + b, n_full + b)
            s_ref[2 * g + b % 2] = jnp.where(col + b * tk <= row, s, NEG_INF)

    def stage_a(b, buf):
        valid = b < n_full + n_diag
        t_key = jnp.clip(b - n_diag, 0, n_blocks - 1)
        t_feat = jnp.where(valid, b - n_diag, n_blocks)
        for g in group:
            s_ref[2 * g + buf] = logits(g, t_key, t_feat)

    def stage_b(buf):
        for g in group:
            p, alpha = softmax_step(g, s_ref[2 * g + buf])
            p_ref[2 * g + buf] = p
            alpha_ref[2 * g + buf] = alpha

    def stage_c(b, buf):
        t = jnp.clip(jnp.where(b < n_diag, n_full + b, b - n_diag), 0, n_blocks - 1)
        for g in group:
            accumulate(g, t, p_ref[2 * g + buf], alpha_ref[2 * g + buf])

    stage_a_boundary(0)
    for b in range(1, n_diag):
        if b >= 2:
            stage_c(b - 2, b % 2)
        stage_b((b - 1) % 2)
        stage_a_boundary(b)

    def run_steps(first, trips, unroll):
        def body(u, carry):
            for r in range(unroll):
                b = first + unroll * u + r + 1
                par = (n_diag + r) % 2
                stage_c(b - 2, par)
                stage_b(1 - par)
                stage_a(b, par)
            return carry
        lax.fori_loop(0, trips, body, 0)

    last_step = n_full + n_diag + 1
    done = n_diag - 1
    for size in DIFF_UNROLLS[:-1]:
        trips = (last_step - done) // size
        run_steps(done, trips, size)
        done = done + trips * size
    run_steps(done, (last_step - done + DIFF_UNROLLS[-1] - 1) // DIFF_UNROLLS[-1], DIFF_UNROLLS[-1])

    lam = (jnp.exp(jnp.sum(lq1_ref[...] * lk1_ref[...], axis=1, keepdims=True))
           - jnp.exp(jnp.sum(lq2_ref[...] * lk2_ref[...], axis=1, keepdims=True)) + lam_init)
    for g in group:
        acc = acc_ref[g]
        o = acc[:, :LANES] / acc[:, LANES:]
        o = o[:tq] - lam * o[tq:]
        r = lax.rsqrt(jnp.mean(o * o, axis=-1, keepdims=True) + EPS)
        o_ref[:, head_lanes[g]] = (o * r * subln_ref[...] * (1.0 - lam_init)).astype(o_ref.dtype)


def _diff_attention(proj, pos_col, slopes, lq1, lk1, lq2, lk2, subln, lam_init):
    s = proj.shape[0]
    tq, tk, n_group = DIFF_TQ, DIFF_TK, DIFF_GROUP
    assert tq % tk == 0 and s % tq == 0 and all(u % 2 == 0 for u in DIFF_UNROLLS)
    assert DIFF_HEADS % n_group == 0
    gw = n_group * LANES
    gb = BRANCH // gw
    vec = lambda n: pl.BlockSpec((1, n), lambda h, i: (0, 0))
    resident = pl.BlockSpec
    return pl.pallas_call(
        functools.partial(_diff_attn_kernel, lam_init=lam_init),
        grid=(DIFF_HEADS // n_group, s // tq),
        in_specs=[
            pl.BlockSpec((tq, gw), lambda h, i: (i, h)),
            resident((s, gw), lambda h, i: (0, gb + h)),
            resident((s, gw), lambda h, i: (0, 2 * gb + h)),
            resident((s, LANES), lambda h, i: (0, 0)),
            pl.BlockSpec((n_group, 8, LANES), lambda h, i: (h, 0, 0)),
            vec(DIFF_HEAD_DIM), vec(DIFF_HEAD_DIM), vec(DIFF_HEAD_DIM), vec(DIFF_HEAD_DIM),
            vec(2 * DIFF_HEAD_DIM),
        ],
        out_specs=pl.BlockSpec((tq, gw), lambda h, i: (i, h)),
        out_shape=jax.ShapeDtypeStruct((s, BRANCH), jnp.bfloat16),
        scratch_shapes=[
            pltpu.VMEM((n_group, 2 * tq, LANES), jnp.float32),
            pltpu.VMEM((n_group, 2 * tq, 2 * LANES), jnp.float32),
            pltpu.VMEM((2 * n_group, 2 * tq, tk), jnp.float32),
            pltpu.VMEM((2 * n_group, 2 * tq, tk), jnp.bfloat16),
            pltpu.VMEM((2 * n_group, 2 * tq, LANES), jnp.float32),
            pltpu.VMEM((s // tk + 1, tk, LANES), jnp.bfloat16),
        ],
        compiler_params=_params("arbitrary", "arbitrary"),
        name="diff_attention",
    )(proj, proj, proj, pos_col, slopes, lq1, lk1, lq2, lk2, subln)


def _swap_halves(x):
    half = LANES // 2
    return jnp.concatenate([x[:, half:], x[:, :half]], axis=1)


def _swa_kernel(q_ref, kc_ref, kp_ref, vc_ref, vp_ref, pq_ref, pkc_ref, pkp_ref, slope_ref,
                sink_ref, o_ref):
    i = pl.program_id(0)
    w = WINDOW
    d = SWA_HEAD_DIM
    cols_per_kv = SWA_GROUP // 2
    rel = jnp.concatenate([pq_ref[...] - pkp_ref[...], pq_ref[...] - pkc_ref[...]], axis=1)
    col = lax.broadcasted_iota(jnp.int32, (w, 2 * w), 1)
    valid = (rel >= 0.0) & (rel < float(w)) & ((col >= w) | (i > 0))
    rel_masked = jnp.where(valid, rel, -NEG_INF)
    lane_half = lax.broadcasted_iota(jnp.int32, (2 * w, LANES), 1) // d
    out_half = lax.broadcasted_iota(jnp.int32, (w, LANES), 1) // d
    ones = jnp.ones((2 * w, LANES), dtype=jnp.bfloat16)

    pairs = []
    for pair in range(SWA_KV_HEADS // 2):
        lanes = slice(pair * LANES, (pair + 1) * LANES)
        kk = jnp.concatenate([kp_ref[:, lanes], kc_ref[:, lanes]], axis=0)
        kk = (kk.astype(jnp.float32) * (d ** -0.5 * LOG2E)).astype(jnp.bfloat16)
        vv = jnp.concatenate([vp_ref[:, lanes], vc_ref[:, lanes]], axis=0)
        pairs.append((kk, _swap_halves(kk), vv, _swap_halves(vv)))

    def operands(n):
        c, t = divmod(n, 2)
        pair, e = divmod(c, 2)
        kk, kk_sw, vv, vv_sw = pairs[pair]
        zero = jnp.zeros_like(kk)
        if t == 0:
            rhs, val = jnp.where(lane_half == e, kk, zero), vv
        else:
            rhs, val = jnp.where(lane_half == e, zero, kk_sw), vv_sw
        heads = [c * SWA_GROUP + 2 * u + (e if t == 0 else 1 - e) for u in range(cols_per_kv)]
        return c, e, rhs, val, heads

    def logits(n):
        c, _, rhs, _, _ = operands(n)
        lhs = jnp.concatenate(
            [q_ref[:, pl.ds((c * cols_per_kv + u) * LANES, LANES)] for u in range(cols_per_kv)], axis=0)
        return lax.dot_general(lhs, rhs, (((1,), (1,)), ((), ())), preferred_element_type=jnp.float32)

    def softmax(n, s):
        heads = operands(n)[4]
        ex, stats = [], []
        for u, head in enumerate(heads):
            slope = slope_ref[head] * LOG2E
            su = s[u * w:(u + 1) * w] - jnp.concatenate([slope, slope], axis=1) * rel_masked
            sink = sink_ref[head] * LOG2E
            m = jnp.maximum(jnp.max(su, axis=1, keepdims=True), sink)
            ex.append(jnp.exp2(su - jnp.concatenate([m, m], axis=1)).astype(jnp.bfloat16))
            stats.append(jnp.exp2(sink - m))
        return jnp.concatenate(ex, axis=0), stats

    def values(n, ex, stats):
        val = operands(n)[3]
        acc = jnp.dot(ex, jnp.concatenate([val, ones], axis=1), preferred_element_type=jnp.float32)
        return [acc[u * w:(u + 1) * w, :LANES] / (acc[u * w:(u + 1) * w, LANES:] + stats[u])
                for u in range(cols_per_kv)]

    n_batches = 2 * SWA_KV_HEADS
    ahead = 1
    s = {n: logits(n) for n in range(ahead)}
    outs = {}
    for n in range(n_batches):
        if n + ahead < n_batches:
            s[n + ahead] = logits(n + ahead)
        ex, stats = softmax(n, s.pop(n))
        outs[n] = values(n, ex, stats)
        if n % 2 == 1:
            c, e = operands(n)[:2]
            merged = [jnp.where(out_half == e, outs[n - 1][u], outs[n][u]) for u in range(cols_per_kv)]
            o_ref[:, pl.ds(c * cols_per_kv * LANES, cols_per_kv * LANES)] = (
                jnp.concatenate(merged, axis=1).astype(o_ref.dtype))


def _swa_attention(proj, pos_col, pos_row, slopes, sinks):
    s = proj.shape[0]
    w = WINDOW
    kvw = SWA_KV_HEADS * SWA_HEAD_DIM
    kblk = (2 * BRANCH) // kvw
    prev = lambda i: jnp.maximum(i - 1, 0)
    return pl.pallas_call(
        _swa_kernel,
        grid=(s // w,),
        in_specs=[
            pl.BlockSpec((w, BRANCH), lambda i: (i, 0)),
            pl.BlockSpec((w, kvw), lambda i: (i, kblk)),
            pl.BlockSpec((w, kvw), lambda i: (prev(i), kblk)),
            pl.BlockSpec((w, kvw), lambda i: (i, kblk + 1)),
            pl.BlockSpec((w, kvw), lambda i: (prev(i), kblk + 1)),
            pl.BlockSpec((w, LANES), lambda i: (i, 0)),
            pl.BlockSpec((1, w), lambda i: (0, i)),
            pl.BlockSpec((1, w), lambda i: (0, prev(i))),
            pl.BlockSpec((SWA_Q_HEADS, 1, LANES), lambda i: (0, 0, 0)),
            pl.BlockSpec((SWA_Q_HEADS, 1, LANES), lambda i: (0, 0, 0)),
        ],
        out_specs=pl.BlockSpec((w, BRANCH), lambda i: (i, 0)),
        out_shape=jax.ShapeDtypeStruct((s, BRANCH), jnp.bfloat16),
        compiler_params=_params("arbitrary"),
        name="swa_attention",
    )(proj, proj, proj, proj, proj, pos_col, pos_row, pos_row, slopes, sinks)


def _silu(g):
    return g / (1.0 + jnp.exp(-g))


def _out_kernel(*refs, gated):
    if gated:
        x_ref, ymix_ref, qm_ref, gmem_ref, kbd_ref, vobd_ref, wo_ref, gpost_ref, o_ref = refs
        y_mix = ymix_ref[...]
    else:
        (x_ref, mix_ref, gmix_ref, qm_ref, gmem_ref, kbd_ref, vobd_ref, wo_ref, gpost_ref,
         o_ref) = refs
        y_mix = mix_ref[...] * _silu(gmix_ref[...])

    s = lax.dot_general(qm_ref[...], kbd_ref[0], (((1,), (1,)), ((), ())),
                        preferred_element_type=jnp.float32)
    ps = []
    for h in range(MEM_HEADS):
        sh = s[:, h * MEM_LEN:(h + 1) * MEM_LEN]
        ps.append(jnp.exp(sh - jnp.max(sh, axis=1, keepdims=True)).astype(jnp.bfloat16))
    nd = jnp.dot(jnp.concatenate(ps, axis=1), vobd_ref[0], preferred_element_type=jnp.float32)
    mem_out = nd[:, :MEM_WIDTH] / nd[:, MEM_WIDTH:]

    y_mem = mem_out.astype(jnp.bfloat16) * _silu(gmem_ref[...])
    y = (jnp.dot(y_mix, wo_ref[:BRANCH, :], preferred_element_type=jnp.float32)
         + jnp.dot(y_mem, wo_ref[BRANCH:, :], preferred_element_type=jnp.float32))
    r = lax.rsqrt(jnp.mean(y * y, axis=-1, keepdims=True) + EPS)
    o_ref[...] = x_ref[...] + y * r * gpost_ref[...]


def _out_layer(x, mix, gate_mix, q_mem, gate_mem, kbd, vobd, layer, w_out, g_post):
    s, d = x.shape
    tm = OUT_TM
    gated = gate_mix is None
    col_blk = lambda width, src: pl.BlockSpec((tm, width), lambda i: (i, src[1] // width))
    in_specs = [pl.BlockSpec((tm, d), lambda i: (i, 0)), pl.BlockSpec((tm, BRANCH), lambda i: (i, 0))]
    args = [x, mix]
    if not gated:
        in_specs.append(col_blk(BRANCH, gate_mix))
        args.append(gate_mix[0])
    in_specs += [
        col_blk(MEM_WIDTH, q_mem), col_blk(MEM_WIDTH, gate_mem),
        pl.BlockSpec((1,) + kbd.shape[1:], lambda i: (layer, 0, 0)),
        pl.BlockSpec((1,) + vobd.shape[1:], lambda i: (layer, 0, 0)),
        pl.BlockSpec((GATE_WIDTH, d), lambda i: (0, 0)),
        pl.BlockSpec((1, d), lambda i: (0, 0)),
    ]
    args += [q_mem[0], gate_mem[0], kbd, vobd, w_out, g_post]
    return pl.pallas_call(
        functools.partial(_out_kernel, gated=gated),
        grid=(s // tm,),
        in_specs=in_specs,
        out_specs=pl.BlockSpec((tm, d), lambda i: (i, 0)),
        out_shape=jax.ShapeDtypeStruct((s, d), jnp.float32),
        compiler_params=_params("arbitrary"),
        name="out_gated" if gated else "out_attn",
    )(*args)


def _relayout_w_in(w, kind):
    if kind == 2:
        q, k, v, q_mem, gate = jnp.split(w, [2048, 2304, 2560, 2816], axis=1)
        parts = [q, gate[:, :BRANCH], k, v, q_mem, gate[:, BRANCH:]]
        cols = dict(a=0, gate_mix=2048, k=4096, v=4352, q_mem=4608, gate_mem=4864)
    elif kind == 1:
        a, b, c, q_mem, gate = jnp.split(w, [2048, 4096, 6144, 6400], axis=1)
        parts = [a, b, c, gate[:, :BRANCH], q_mem, gate[:, BRANCH:]]
        cols = dict(a=0, b=2048, c=4096, gate_mix=6144, q_mem=8192, gate_mem=8448)
    else:
        a, b, c, q_mem, gate = jnp.split(w, [2048, 4096, 6144, 6400], axis=1)
        parts = []
        for j in range(BRANCH // CONV_BLOCK):
            ch = slice(j * CONV_BLOCK, (j + 1) * CONV_BLOCK)
            parts += [a[:, ch], b[:, ch], c[:, ch], gate[:, ch]]
        parts += [q_mem, gate[:, BRANCH:]]
        cols = dict(q_mem=0, gate_mem=MEM_WIDTH)
    return jnp.concatenate([p.astype(jnp.bfloat16) for p in parts], axis=1), cols


def _alibi_slopes(n_heads):
    return 2.0 ** (-ALIBI_MAX_BIAS * jnp.arange(1, n_heads + 1, dtype=jnp.float32) / n_heads)


def kernel(x, mem, positions, norm_pre_0, norm_post_0, norm_mem_0, w_in_0, w_mem_kv_0, conv_w_0, w_out_0, norm_pre_1, norm_post_1, norm_mem_1, w_in_1, w_mem_kv_1, lambda_q1_1, lambda_k1_1, lambda_q2_1, lambda_k2_1, subln_1, w_out_1, norm_pre_2, norm_post_2, norm_mem_2, w_in_2, w_mem_kv_2, sinks_2, w_out_2, norm_pre_3, norm_post_3, norm_mem_3, w_in_3, w_mem_kv_3, conv_w_3, w_out_3):
    b, s, d = x.shape
    assert b == 1 and s == SEQ and d == D_MODEL
    xs = x.reshape(s, d)
    pos_f = positions.reshape(s).astype(jnp.float32)
    pos_row = pos_f.reshape(1, s)
    pos_col = jnp.broadcast_to(pos_f[:, None], (s, LANES))

    pre = [norm_pre_0, norm_pre_1, norm_pre_2, norm_pre_3]
    post = [norm_post_0, norm_post_1, norm_post_2, norm_post_3]
    w_in = [w_in_0, w_in_1, w_in_2, w_in_3]
    w_out = [w_out_0, w_out_1, w_out_2, w_out_3]
    conv_w = {0: conv_w_0, 3: conv_w_3}

    mem_gain = jnp.stack([norm_mem_0, norm_mem_1, norm_mem_2, norm_mem_3]).reshape(DEPTH, 1, d)
    mem_w = jnp.stack([w_mem_kv_0, w_mem_kv_1, w_mem_kv_2, w_mem_kv_3]).astype(jnp.bfloat16)
    kbd, vobd = _mem_kv(mem.reshape(MEM_LEN, d), mem_gain, mem_w)

    for layer in range(DEPTH):
        kind = layer % 3
        w, cols = _relayout_w_in(w_in[layer], kind)
        gain = pre[layer].reshape(1, d)
        if kind == 0:
            mix, proj = _norm_proj_conv(xs, gain, w, conv_w[layer])
            gate_mix = None
        else:
            proj = _norm_proj(xs, gain, w)
            gate_mix = (proj, cols["gate_mix"])
        if kind == 1:
            slopes = jnp.broadcast_to(_alibi_slopes(DIFF_HEADS)[:, None, None], (DIFF_HEADS, 8, LANES))
            lam_init = 0.8 - 0.6 * math.exp(-0.3 * layer)
            mix = _diff_attention(
                proj, pos_col, slopes,
                lambda_q1_1.reshape(1, -1), lambda_k1_1.reshape(1, -1),
                lambda_q2_1.reshape(1, -1), lambda_k2_1.reshape(1, -1),
                subln_1.reshape(1, -1), lam_init)
        elif kind == 2:
            slopes = jnp.broadcast_to(_alibi_slopes(SWA_Q_HEADS)[:, None, None], (SWA_Q_HEADS, 1, LANES))
            sinks = jnp.broadcast_to(sinks_2.astype(jnp.float32)[:, None, None], (SWA_Q_HEADS, 1, LANES))
            mix = _swa_attention(proj, pos_col, pos_row, slopes, sinks)
        xs = _out_layer(xs, mix, gate_mix, (proj, cols["q_mem"]), (proj, cols["gate_mem"]), kbd, vobd,
                        layer, w_out[layer].astype(jnp.bfloat16), post[layer].reshape(1, d))
    return xs.reshape(b, s, d)
```

```python
import functools
import math

import jax
import jax.numpy as jnp
from jax import lax
from jax.experimental import pallas as pl
from jax.experimental.pallas import tpu as pltpu

D_MODEL = 1024
SEQ = 16384
DEPTH = 4
MEM_LEN = 256
BRANCH = 2048
CONV_WIDTH = 3
DIFF_HEAD_DIM = 64
DIFF_HEADS = 16
SWA_HEAD_DIM = 64
SWA_Q_HEADS = 32
SWA_KV_HEADS = 4
SWA_GROUP = SWA_Q_HEADS // SWA_KV_HEADS
WINDOW = 128
MEM_HEADS = 4
MEM_HEAD_DIM = 64
MEM_WIDTH = 256
GATE_WIDTH = BRANCH + MEM_WIDTH
ALIBI_MAX_BIAS = 8.0
EPS = 1e-6
NEG_INF = -1e30

LANES = 128
VMEM_LIMIT_BYTES = 56 * 1024 * 1024

PROJ_TM = 512
CONV_BLOCK = 256
OUT_TM = 512
DIFF_TQ = 512
DIFF_TK = 256
DIFF_GROUP = 1
DIFF_UNROLLS = (16, 8, 4, 2)
FEAT_RADIX = 128.0
LOG2E = math.log2(math.e)


def _params(*sem):
    return pltpu.CompilerParams(dimension_semantics=sem, vmem_limit_bytes=VMEM_LIMIT_BYTES)


def _mem_kv_kernel(mem_ref, g_ref, w_ref, kbd_ref, vobd_ref):
    m = mem_ref[...]
    r = lax.rsqrt(jnp.mean(m * m, axis=-1, keepdims=True) + EPS)
    mn = (m * r * g_ref[0]).astype(jnp.bfloat16)
    kv = jnp.dot(mn, w_ref[0], preferred_element_type=jnp.float32)
    km = kv[:, :MEM_WIDTH] * (MEM_HEAD_DIM ** -0.5)
    vm = kv[:, MEM_WIDTH:]
    head_of_lane = lax.broadcasted_iota(jnp.int32, (MEM_LEN, MEM_WIDTH), 1) // MEM_HEAD_DIM
    for h in range(MEM_HEADS):
        sel = head_of_lane == h
        rows = pl.ds(h * MEM_LEN, MEM_LEN)
        kbd_ref[0, rows, :] = jnp.where(sel, km, 0.0).astype(jnp.bfloat16)
        vobd_ref[0, rows, :MEM_WIDTH] = jnp.where(sel, vm, 0.0).astype(jnp.bfloat16)
        vobd_ref[0, rows, MEM_WIDTH:] = jnp.where(sel, 1.0, 0.0).astype(jnp.bfloat16)


def _mem_kv(mem, gains, weights):
    n_layers = gains.shape[0]
    rows = MEM_HEADS * MEM_LEN
    return pl.pallas_call(
        _mem_kv_kernel,
        grid=(n_layers,),
        in_specs=[
            pl.BlockSpec((MEM_LEN, D_MODEL), lambda l: (0, 0)),
            pl.BlockSpec((1, 1, D_MODEL), lambda l: (l, 0, 0)),
            pl.BlockSpec((1, D_MODEL, 2 * MEM_WIDTH), lambda l: (l, 0, 0)),
        ],
        out_specs=[
            pl.BlockSpec((1, rows, MEM_WIDTH), lambda l: (l, 0, 0)),
            pl.BlockSpec((1, rows, 2 * MEM_WIDTH), lambda l: (l, 0, 0)),
        ],
        out_shape=[
            jax.ShapeDtypeStruct((n_layers, rows, MEM_WIDTH), jnp.bfloat16),
            jax.ShapeDtypeStruct((n_layers, rows, 2 * MEM_WIDTH), jnp.bfloat16),
        ],
        compiler_params=_params("arbitrary"),
        name="mem_kv",
    )(mem, gains, weights)


def _proj_kernel(x_ref, g_ref, w_ref, o_ref):
    x = x_ref[...]
    r = lax.rsqrt(jnp.mean(x * x, axis=-1, keepdims=True) + EPS)
    h = (x * r * g_ref[...]).astype(jnp.bfloat16)
    o_ref[...] = jnp.dot(h, w_ref[...], preferred_element_type=jnp.float32).astype(o_ref.dtype)


def _norm_proj(x, gain, w):
    s, d = x.shape
    c = w.shape[1]
    n_col = 2
    tn = c // n_col
    assert tn * n_col == c and tn % LANES == 0 and s % PROJ_TM == 0
    return pl.pallas_call(
        _proj_kernel,
        grid=(n_col, s // PROJ_TM),
        in_specs=[
            pl.BlockSpec((PROJ_TM, d), lambda j, i: (i, 0)),
            pl.BlockSpec((1, d), lambda j, i: (0, 0)),
            pl.BlockSpec((d, tn), lambda j, i: (0, j)),
        ],
        out_specs=pl.BlockSpec((PROJ_TM, tn), lambda j, i: (i, j)),
        out_shape=jax.ShapeDtypeStruct((s, c), jnp.bfloat16),
        compiler_params=_params("arbitrary", "arbitrary"),
        name="norm_proj",
    )(x, gain, w)


def _proj_conv_kernel(x_ref, g_ref, w_ref, cw_ref, y_ref, qg_ref, zh_ref, h_ref):
    i = pl.program_id(0)
    x = x_ref[...]
    r = lax.rsqrt(jnp.mean(x * x, axis=-1, keepdims=True) + EPS)
    h_ref[...] = (x * r * g_ref[...]).astype(jnp.bfloat16)
    tm = x.shape[0]
    ch = CONV_BLOCK
    row = lax.broadcasted_iota(jnp.int32, (8, ch), 0)
    for j in range(BRANCH // ch):
        pr = jnp.dot(h_ref[...], w_ref[:, 4 * ch * j:4 * ch * (j + 1)],
                     preferred_element_type=jnp.float32)
        bg, cg, u, gate = (pr[:, n * ch:(n + 1) * ch] for n in range(4))
        z = cg * u
        zh = jnp.where(i > 0, zh_ref[j], 0.0)
        z1 = pltpu.roll(z, 1, 0)
        z2 = pltpu.roll(z, 2, 0)
        z1 = jnp.concatenate([jnp.where(row == 0, zh[7:8], z1[:8]), z1[8:]], axis=0)
        z2 = jnp.concatenate(
            [jnp.where(row == 0, zh[6:7], jnp.where(row == 1, zh[7:8], z2[:8])), z2[8:]], axis=0)
        cw = cw_ref[:, j * ch:(j + 1) * ch]
        mix = bg * (cw[0:1] * z2 + cw[1:2] * z1 + cw[2:3] * z)
        y_ref[:, j * ch:(j + 1) * ch] = (mix * _silu(gate)).astype(y_ref.dtype)
        zh_ref[j] = z[tm - 8:]
    qg_ref[...] = jnp.dot(h_ref[...], w_ref[:, 4 * BRANCH:],
                          preferred_element_type=jnp.float32).astype(qg_ref.dtype)


def _norm_proj_conv(x, gain, w, conv_w):
    s, d = x.shape
    c = w.shape[1]
    assert c == 4 * BRANCH + 2 * MEM_WIDTH and BRANCH % CONV_BLOCK == 0
    return pl.pallas_call(
        _proj_conv_kernel,
        grid=(s // PROJ_TM,),
        in_specs=[
            pl.BlockSpec((PROJ_TM, d), lambda i: (i, 0)),
            pl.BlockSpec((1, d), lambda i: (0, 0)),
            pl.BlockSpec((d, c), lambda i: (0, 0), pipeline_mode=pl.Buffered(1)),
            pl.BlockSpec((CONV_WIDTH, BRANCH), lambda i: (0, 0)),
        ],
        out_specs=[
            pl.BlockSpec((PROJ_TM, BRANCH), lambda i: (i, 0)),
            pl.BlockSpec((PROJ_TM, 2 * MEM_WIDTH), lambda i: (i, 0)),
        ],
        out_shape=[
            jax.ShapeDtypeStruct((s, BRANCH), jnp.bfloat16),
            jax.ShapeDtypeStruct((s, 2 * MEM_WIDTH), jnp.bfloat16),
        ],
        scratch_shapes=[pltpu.VMEM((BRANCH // CONV_BLOCK, 8, CONV_BLOCK), jnp.float32),
                        pltpu.VMEM((PROJ_TM, d), jnp.bfloat16)],
        compiler_params=_params("arbitrary"),
        name="norm_proj_conv",
    )(x, gain, w, conv_w)


def _diff_attn_kernel(q_ref, k_ref, v_ref, posk_ref, slope_ref, lq1_ref, lk1_ref, lq2_ref, lk2_ref,
                      subln_ref, o_ref, m_ref, acc_ref, s_ref, p_ref, alpha_ref, kf_ref, *, lam_init):
    tq, tk = DIFF_TQ, DIFF_TK
    group = range(DIFF_GROUP)
    i = pl.program_id(1)
    d = DIFF_HEAD_DIM
    n_blocks = SEQ // tk
    reps = tk // LANES
    n_diag = tq // tk
    n_full = i * n_diag
    head_lanes = [slice(g * LANES, (g + 1) * LANES) for g in group]

    @pl.when(i == 0)
    def _build_key_features():
        lane_t = lax.broadcasted_iota(jnp.int32, (tk, LANES), 1)

        def fill(t, carry):
            pos = posk_ref[pl.ds(pl.multiple_of(t * tk, tk), tk), :]
            hi = jnp.floor(pos * (1.0 / FEAT_RADIX))
            lo = pos - hi * FEAT_RADIX
            kf_ref[t] = jnp.where(lane_t < 3, hi, jnp.where(lane_t < 6, lo, 0.0)).astype(jnp.bfloat16)
            return carry

        lax.fori_loop(0, n_blocks, fill, 0)

    flane = lax.broadcasted_iota(jnp.int32, (1, LANES), 1)
    qqs = []
    for g in group:
        q = q_ref[:, head_lanes[g]].astype(jnp.float32) * (d ** -0.5 * LOG2E)
        lane = lax.broadcasted_iota(jnp.int32, q.shape, 1)
        qq = jnp.concatenate([jnp.where(lane < d, q, 0.0), jnp.where(lane < d, 0.0, q)],
                             axis=0).astype(jnp.bfloat16)
        slope = slope_ref[g, 0:1, :] * LOG2E
        s1 = slope.astype(jnp.bfloat16).astype(jnp.float32)
        s2 = (slope - s1).astype(jnp.bfloat16).astype(jnp.float32)
        s3 = (slope - s1 - s2).astype(jnp.bfloat16).astype(jnp.float32)
        qfeat = jnp.zeros((1, LANES), jnp.float32)
        for f, val in enumerate([s1 * FEAT_RADIX, s2 * FEAT_RADIX, s3 * FEAT_RADIX, s1, s2, s3]):
            qfeat = jnp.where(flane == f, val, qfeat)
        qqs.append(jnp.concatenate(
            [qq, jnp.broadcast_to(qfeat, (2 * tq, LANES)).astype(jnp.bfloat16)], axis=1))
        m_ref[g] = jnp.full(m_ref.shape[1:], NEG_INF, dtype=jnp.float32)
        acc_ref[g] = jnp.zeros(acc_ref.shape[1:], dtype=jnp.float32)

    ones = jnp.ones((tk, LANES), dtype=jnp.bfloat16)

    def logits(g, t):
        rows = pl.ds(pl.multiple_of(t * tk, tk), tk)
        keys = jnp.concatenate([k_ref[rows, head_lanes[g]], kf_ref[t]], axis=1)
        return lax.dot_general(qqs[g], keys, (((1,), (1,)), ((), ())),
                               preferred_element_type=jnp.float32)

    def softmax_step(g, s):
        m_prev = m_ref[g]
        m_new = jnp.maximum(m_prev, jnp.max(s, axis=1, keepdims=True))
        m_ref[g] = m_new
        p = jnp.exp2(s - jnp.concatenate([m_new] * reps, axis=1)).astype(jnp.bfloat16)
        return p, jnp.exp2(m_prev - m_new)

    def accumulate(g, t, p, alpha):
        vb = v_ref[pl.ds(pl.multiple_of(t * tk, tk), tk), head_lanes[g]]
        pv = jnp.dot(p, jnp.concatenate([vb, ones], axis=1), preferred_element_type=jnp.float32)
        acc_ref[g] = acc_ref[g] * jnp.concatenate([alpha, alpha], axis=1) + pv

    row = lax.broadcasted_iota(jnp.int32, (2 * tq, tk), 0)
    row = jnp.where(row >= tq, row - tq, row)
    col = lax.broadcasted_iota(jnp.int32, (2 * tq, tk), 1)

    def stage_a_boundary(b):
        for g in group:
            s = logits(g, n_full + b)
            s_ref[2 * g + b % 2] = jnp.where(col + b * tk <= row, s, NEG_INF)

    def stage_a(b, buf):
        for g in group:
            s_ref[2 * g + buf] = logits(g, b - n_diag)

    def stage_b(buf):
        for g in group:
            p, alpha = softmax_step(g, s_ref[2 * g + buf])
            p_ref[2 * g + buf] = p
            alpha_ref[2 * g + buf] = alpha

    def stage_c(b, buf):
        t = jnp.where(b < n_diag, n_full + b, b - n_diag)
        for g in group:
            accumulate(g, t, p_ref[2 * g + buf], alpha_ref[2 * g + buf])

    stage_a_boundary(0)
    for b in range(1, n_diag):
        if b >= 2:
            stage_c(b - 2, b % 2)
        stage_b((b - 1) % 2)
        stage_a_boundary(b)

    def run_steps(first, trips, unroll):
        def body(u, carry):
            for r in range(unroll):
                b = first + unroll * u + r + 1
                par = (n_diag + r) % 2
                stage_c(b - 2, par)
                stage_b(1 - par)
                stage_a(b, par)
            return carry
        lax.fori_loop(0, trips, body, 0)

    n_blk = n_full + n_diag
    done = n_diag - 1
    for size in DIFF_UNROLLS:
        trips = (n_blk - 1 - done) // size
        run_steps(done, trips, size)
        done = done + trips * size
    par = n_diag % 2
    stage_c(n_blk - 2, par)
    stage_b(1 - par)
    stage_c(n_blk - 1, 1 - par)

    lam =(jnp.exp(jnp.sum(lq1_ref[...] * lk1_ref[...], axis=1, keepdims=True))
           - jnp.exp(jnp.sum(lq2_ref[...] * lk2_ref[...], axis=1, keepdims=True)) + lam_init)
    for g in group:
        acc = acc_ref[g]
        o = acc[:, :LANES] / acc[:, LANES:]
        o = o[:tq] - lam * o[tq:]
        r = lax.rsqrt(jnp.mean(o * o, axis=-1, keepdims=True) + EPS)
        o_ref[:, head_lanes[g]] = (o * r * subln_ref[...] * (1.0 - lam_init)).astype(o_ref.dtype)


def _diff_attention(proj, pos_col, slopes, lq1, lk1, lq2, lk2, subln, lam_init):
    s = proj.shape[0]
    tq, tk, n_group = DIFF_TQ, DIFF_TK, DIFF_GROUP
    assert tq % (2 * tk) == 0 and s % tq == 0
    assert all(u % 2 == 0 for u in DIFF_UNROLLS) and DIFF_UNROLLS[-1] == 2
    assert DIFF_HEADS % n_group == 0
    gw = n_group * LANES
    gb = BRANCH // gw
    vec = lambda n: pl.BlockSpec((1, n), lambda h, i: (0, 0))
    resident = pl.BlockSpec
    return pl.pallas_call(
        functools.partial(_diff_attn_kernel, lam_init=lam_init),
        grid=(DIFF_HEADS // n_group, s // tq),
        in_specs=[
            pl.BlockSpec((tq, gw), lambda h, i: (i, h)),
            resident((s, gw), lambda h, i: (0, gb + h)),
            resident((s, gw), lambda h, i: (0, 2 * gb + h)),
            resident((s, LANES), lambda h, i: (0, 0)),
            pl.BlockSpec((n_group, 8, LANES), lambda h, i: (h, 0, 0)),
            vec(DIFF_HEAD_DIM), vec(DIFF_HEAD_DIM), vec(DIFF_HEAD_DIM), vec(DIFF_HEAD_DIM),
            vec(2 * DIFF_HEAD_DIM),
        ],
        out_specs=pl.BlockSpec((tq, gw), lambda h, i: (i, h)),
        out_shape=jax.ShapeDtypeStruct((s, BRANCH), jnp.bfloat16),
        scratch_shapes=[
            pltpu.VMEM((n_group, 2 * tq, LANES), jnp.float32),
            pltpu.VMEM((n_group, 2 * tq, 2 * LANES), jnp.float32),
            pltpu.VMEM((2 * n_group, 2 * tq, tk), jnp.float32),
            pltpu.VMEM((2 * n_group, 2 * tq, tk), jnp.bfloat16),
            pltpu.VMEM((2 * n_group, 2 * tq, LANES), jnp.float32),
            pltpu.VMEM((s // tk, tk, LANES), jnp.bfloat16),
        ],
        compiler_params=_params("arbitrary", "arbitrary"),
        name="diff_attention",
    )(proj, proj, proj, pos_col, slopes, lq1, lk1, lq2, lk2, subln)


def _swap_halves(x):
    half = LANES // 2
    return jnp.concatenate([x[:, half:], x[:, :half]], axis=1)


def _swa_kernel(q_ref, kc_ref, kp_ref, vc_ref, vp_ref, pq_ref, pkc_ref, pkp_ref, slope_ref,
                sink_ref, o_ref):
    i = pl.program_id(0)
    w = WINDOW
    d = SWA_HEAD_DIM
    cols_per_kv = SWA_GROUP // 2
    rel = jnp.concatenate([pq_ref[...] - pkp_ref[...], pq_ref[...] - pkc_ref[...]], axis=1)
    col = lax.broadcasted_iota(jnp.int32, (w, 2 * w), 1)
    valid = (rel >= 0.0) & (rel < float(w)) & ((col >= w) | (i > 0))
    rel_masked = jnp.where(valid, rel, -NEG_INF)
    lane_half = lax.broadcasted_iota(jnp.int32, (2 * w, LANES), 1) // d
    out_half = lax.broadcasted_iota(jnp.int32, (w, LANES), 1) // d
    ones = jnp.ones((2 * w, LANES), dtype=jnp.bfloat16)

    pairs = []
    for pair in range(SWA_KV_HEADS // 2):
        lanes = slice(pair * LANES, (pair + 1) * LANES)
        kk = jnp.concatenate([kp_ref[:, lanes], kc_ref[:, lanes]], axis=0)
        kk = (kk.astype(jnp.float32) * (d ** -0.5 * LOG2E)).astype(jnp.bfloat16)
        vv = jnp.concatenate([vp_ref[:, lanes], vc_ref[:, lanes]], axis=0)
        pairs.append((kk, _swap_halves(kk), vv, _swap_halves(vv)))

    def operands(n):
        c, t = divmod(n, 2)
        pair, e = divmod(c, 2)
        kk, kk_sw, vv, vv_sw = pairs[pair]
        zero = jnp.zeros_like(kk)
        if t == 0:
            rhs, val = jnp.where(lane_half == e, kk, zero), vv
        else:
            rhs, val = jnp.where(lane_half == e, zero, kk_sw), vv_sw
        heads = [c * SWA_GROUP + 2 * u + (e if t == 0 else 1 - e) for u in range(cols_per_kv)]
        return c, e, rhs, val, heads

    def logits(n):
        c, _, rhs, _, _ = operands(n)
        lhs = jnp.concatenate(
            [q_ref[:, pl.ds((c * cols_per_kv + u) * LANES, LANES)] for u in range(cols_per_kv)], axis=0)
        return lax.dot_general(lhs, rhs, (((1,), (1,)), ((), ())), preferred_element_type=jnp.float32)

    def softmax(n, s):
        heads = operands(n)[4]
        ex, stats = [], []
        for u, head in enumerate(heads):
            slope = slope_ref[head] * LOG2E
            su = s[u * w:(u + 1) * w] - jnp.concatenate([slope, slope], axis=1) * rel_masked
            sink = sink_ref[head] * LOG2E
            m = jnp.maximum(jnp.max(su, axis=1, keepdims=True), sink)
            ex.append(jnp.exp2(su - jnp.concatenate([m, m], axis=1)).astype(jnp.bfloat16))
            stats.append(jnp.exp2(sink - m))
        return jnp.concatenate(ex, axis=0), stats

    def values(n, ex, stats):
        val = operands(n)[3]
        acc = jnp.dot(ex, jnp.concatenate([val, ones], axis=1), preferred_element_type=jnp.float32)
        return [acc[u * w:(u + 1) * w, :LANES] / (acc[u * w:(u + 1) * w, LANES:] + stats[u])
                for u in range(cols_per_kv)]

    n_batches = 2 * SWA_KV_HEADS
    ahead = 1
    s = {n: logits(n) for n in range(ahead)}
    outs = {}
    for n in range(n_batches):
        if n + ahead < n_batches:
            s[n + ahead] = logits(n + ahead)
        ex, stats = softmax(n, s.pop(n))
        outs[n] = values(n, ex, stats)
        if n % 2 == 1:
            c, e = operands(n)[:2]
            merged = [jnp.where(out_half == e, outs[n - 1][u], outs[n][u]) for u in range(cols_per_kv)]
            o_ref[:, pl.ds(c * cols_per_kv * LANES, cols_per_kv * LANES)] = (
                jnp.concatenate(merged, axis=1).astype(o_ref.dtype))


def _swa_attention(proj, pos_col, pos_row, slopes, sinks):
    s = proj.shape[0]
    w = WINDOW
    kvw = SWA_KV_HEADS * SWA_HEAD_DIM
    kblk = (2 * BRANCH) // kvw
    prev = lambda i: jnp.maximum(i - 1, 0)
    return pl.pallas_call(
        _swa_kernel,
        grid=(s // w,),
        in_specs=[
            pl.BlockSpec((w, BRANCH), lambda i: (i, 0)),
            pl.BlockSpec((w, kvw), lambda i: (i, kblk)),
            pl.BlockSpec((w, kvw), lambda i: (prev(i), kblk)),
            pl.BlockSpec((w, kvw), lambda i: (i, kblk + 1)),
            pl.BlockSpec((w, kvw), lambda i: (prev(i), kblk + 1)),
            pl.BlockSpec((w, LANES), lambda i: (i, 0)),
            pl.BlockSpec((1, w), lambda i: (0, i)),
            pl.BlockSpec((1, w), lambda i: (0, prev(i))),
            pl.BlockSpec((SWA_Q_HEADS, 1, LANES), lambda i: (0, 0, 0)),
            pl.BlockSpec((SWA_Q_HEADS, 1, LANES), lambda i: (0, 0, 0)),
        ],
        out_specs=pl.BlockSpec((w, BRANCH), lambda i: (i, 0)),
        out_shape=jax.ShapeDtypeStruct((s, BRANCH), jnp.bfloat16),
        compiler_params=_params("arbitrary"),
        name="swa_attention",
    )(proj, proj, proj, proj, proj, pos_col, pos_row, pos_row, slopes, sinks)


def _silu(g):
    return g / (1.0 + jnp.exp(-g))


def _out_kernel(*refs, gated):
    if gated:
        x_ref, ymix_ref, qm_ref, gmem_ref, kbd_ref, vobd_ref, wo_ref, gpost_ref, o_ref = refs
        y_mix = ymix_ref[...]
    else:
        (x_ref, mix_ref, gmix_ref, qm_ref, gmem_ref, kbd_ref, vobd_ref, wo_ref, gpost_ref,
         o_ref) = refs
        y_mix = mix_ref[...] * _silu(gmix_ref[...])

    s = lax.dot_general(qm_ref[...], kbd_ref[0], (((1,), (1,)), ((), ())),
                        preferred_element_type=jnp.float32)
    ps = []
    for h in range(MEM_HEADS):
        sh = s[:, h * MEM_LEN:(h + 1) * MEM_LEN]
        ps.append(jnp.exp(sh - jnp.max(sh, axis=1, keepdims=True)).astype(jnp.bfloat16))
    nd = jnp.dot(jnp.concatenate(ps, axis=1), vobd_ref[0], preferred_element_type=jnp.float32)
    mem_out = nd[:, :MEM_WIDTH] / nd[:, MEM_WIDTH:]

    y_mem = mem_out.astype(jnp.bfloat16) * _silu(gmem_ref[...])
    y = (jnp.dot(y_mix, wo_ref[:BRANCH, :], preferred_element_type=jnp.float32)
         + jnp.dot(y_mem, wo_ref[BRANCH:, :], preferred_element_type=jnp.float32))
    r = lax.rsqrt(jnp.mean(y * y, axis=-1, keepdims=True) + EPS)
    o_ref[...] = x_ref[...] + y * r * gpost_ref[...]


def _out_layer(x, mix, gate_mix, q_mem, gate_mem, kbd, vobd, layer, w_out, g_post):
    s, d = x.shape
    tm = OUT_TM
    gated = gate_mix is None
    col_blk = lambda width, src: pl.BlockSpec((tm, width), lambda i: (i, src[1] // width))
    in_specs = [pl.BlockSpec((tm, d), lambda i: (i, 0)), pl.BlockSpec((tm, BRANCH), lambda i: (i, 0))]
    args = [x, mix]
    if not gated:
        in_specs.append(col_blk(BRANCH, gate_mix))
        args.append(gate_mix[0])
    in_specs += [
        col_blk(MEM_WIDTH, q_mem), col_blk(MEM_WIDTH, gate_mem),
        pl.BlockSpec((1,) + kbd.shape[1:], lambda i: (layer, 0, 0)),
        pl.BlockSpec((1,) + vobd.shape[1:], lambda i: (layer, 0, 0)),
        pl.BlockSpec((GATE_WIDTH, d), lambda i: (0, 0)),
        pl.BlockSpec((1, d), lambda i: (0, 0)),
    ]
    args += [q_mem[0], gate_mem[0], kbd, vobd, w_out, g_post]
    return pl.pallas_call(
        functools.partial(_out_kernel, gated=gated),
        grid=(s // tm,),
        in_specs=in_specs,
        out_specs=pl.BlockSpec((tm, d), lambda i: (i, 0)),
        out_shape=jax.ShapeDtypeStruct((s, d), jnp.float32),
        compiler_params=_params("arbitrary"),
        name="out_gated" if gated else "out_attn",
    )(*args)


def _relayout_w_in(w, kind):
    if kind == 2:
        q, k, v, q_mem, gate = jnp.split(w, [2048, 2304, 2560, 2816], axis=1)
        parts = [q, gate[:, :BRANCH], k, v, q_mem, gate[:, BRANCH:]]
        cols = dict(a=0, gate_mix=2048, k=4096, v=4352, q_mem=4608, gate_mem=4864)
    elif kind == 1:
        a, b, c, q_mem, gate = jnp.split(w, [2048, 4096, 6144, 6400], axis=1)
        parts = [a, b, c, gate[:, :BRANCH], q_mem, gate[:, BRANCH:]]
        cols = dict(a=0, b=2048, c=4096, gate_mix=6144, q_mem=8192, gate_mem=8448)
    else:
        a, b, c, q_mem, gate = jnp.split(w, [2048, 4096, 6144, 6400], axis=1)
        parts = []
        for j in range(BRANCH // CONV_BLOCK):
            ch = slice(j * CONV_BLOCK, (j + 1) * CONV_BLOCK)
            parts += [a[:, ch], b[:, ch], c[:, ch], gate[:, ch]]
        parts += [q_mem, gate[:, BRANCH:]]
        cols = dict(q_mem=0, gate_mem=MEM_WIDTH)
    return jnp.concatenate([p.astype(jnp.bfloat16) for p in parts], axis=1), cols


def _alibi_slopes(n_heads):
    return 2.0 ** (-ALIBI_MAX_BIAS * jnp.arange(1, n_heads + 1, dtype=jnp.float32) / n_heads)


def kernel(x, mem, positions, norm_pre_0, norm_post_0, norm_mem_0, w_in_0, w_mem_kv_0, conv_w_0, w_out_0, norm_pre_1, norm_post_1, norm_mem_1, w_in_1, w_mem_kv_1, lambda_q1_1, lambda_k1_1, lambda_q2_1, lambda_k2_1, subln_1, w_out_1, norm_pre_2, norm_post_2, norm_mem_2, w_in_2, w_mem_kv_2, sinks_2, w_out_2, norm_pre_3, norm_post_3, norm_mem_3, w_in_3, w_mem_kv_3, conv_w_3, w_out_3):
    b, s, d = x.shape
    assert b == 1 and s == SEQ and d == D_MODEL
    xs = x.reshape(s, d)
    pos_f = positions.reshape(s).astype(jnp.float32)
    pos_row = pos_f.reshape(1, s)
    pos_col = jnp.broadcast_to(pos_f[:, None], (s, LANES))

    pre = [norm_pre_0, norm_pre_1, norm_pre_2, norm_pre_3]
    post = [norm_post_0, norm_post_1, norm_post_2, norm_post_3]
    w_in = [w_in_0, w_in_1, w_in_2, w_in_3]
    w_out = [w_out_0, w_out_1, w_out_2, w_out_3]
    conv_w = {0: conv_w_0, 3: conv_w_3}

    mem_gain = jnp.stack([norm_mem_0, norm_mem_1, norm_mem_2, norm_mem_3]).reshape(DEPTH, 1, d)
    mem_w = jnp.stack([w_mem_kv_0, w_mem_kv_1, w_mem_kv_2, w_mem_kv_3]).astype(jnp.bfloat16)
    kbd, vobd = _mem_kv(mem.reshape(MEM_LEN, d), mem_gain, mem_w)

    for layer in range(DEPTH):
        kind = layer % 3
        w, cols = _relayout_w_in(w_in[layer], kind)
        gain = pre[layer].reshape(1, d)
        if kind == 0:
            mix, proj = _norm_proj_conv(xs, gain, w, conv_w[layer])
            gate_mix = None
        else:
            proj = _norm_proj(xs, gain, w)
            gate_mix = (proj, cols["gate_mix"])
        if kind == 1:
            slopes = jnp.broadcast_to(_alibi_slopes(DIFF_HEADS)[:, None, None], (DIFF_HEADS, 8, LANES))
            lam_init = 0.8 - 0.6 * math.exp(-0.3 * layer)
            mix = _diff_attention(
                proj, pos_col, slopes,
                lambda_q1_1.reshape(1, -1), lambda_k1_1.reshape(1, -1),
                lambda_q2_1.reshape(1, -1), lambda_k2_1.reshape(1, -1),
                subln_1.reshape(1, -1), lam_init)
        elif kind == 2:
            slopes = jnp.broadcast_to(_alibi_slopes(SWA_Q_HEADS)[:, None, None], (SWA_Q_HEADS, 1, LANES))
            sinks = jnp.broadcast_to(sinks_2.astype(jnp.float32)[:, None, None], (SWA_Q_HEADS, 1, LANES))
            mix = _swa_attention(proj, pos_col, pos_row, slopes, sinks)
        xs = _out_layer(xs, mix, gate_mix, (proj, cols["q_mem"]), (proj, cols["gate_mem"]), kbd, vobd,
                        layer, w_out[layer].astype(jnp.bfloat16), post[layer].reshape(1, d))
    return xs.reshape(b, s, d)
```

```python
import functools
import math

import jax
import jax.numpy as jnp
from jax import lax
from jax.experimental import pallas as pl
from jax.experimental.pallas import tpu as pltpu

D_MODEL = 1024
SEQ = 16384
DEPTH = 4
MEM_LEN = 256
BRANCH = 2048
CONV_WIDTH = 3
DIFF_HEAD_DIM = 64
DIFF_HEADS = 16
SWA_HEAD_DIM = 64
SWA_Q_HEADS = 32
SWA_KV_HEADS = 4
SWA_GROUP = SWA_Q_HEADS // SWA_KV_HEADS
WINDOW = 128
MEM_HEADS = 4
MEM_HEAD_DIM = 64
MEM_WIDTH = 256
GATE_WIDTH = BRANCH + MEM_WIDTH
ALIBI_MAX_BIAS = 8.0
EPS = 1e-6
NEG_INF = -1e30

LANES = 128
VMEM_LIMIT_BYTES = 56 * 1024 * 1024

PROJ_TM = 512
CONV_BLOCK = 256
OUT_TM = 512
DIFF_TQ = 512
DIFF_TK = 256
DIFF_GROUP = 1
DIFF_UNROLLS = (16, 8, 4, 2)
FEAT_RADIX = 128.0
FEAT_DUMMY_LANE = 6
LOG2E = math.log2(math.e)


def _params(*sem):
    return pltpu.CompilerParams(dimension_semantics=sem, vmem_limit_bytes=VMEM_LIMIT_BYTES)


def _mem_kv_kernel(mem_ref, g_ref, w_ref, kbd_ref, vobd_ref):
    m = mem_ref[...]
    r = lax.rsqrt(jnp.mean(m * m, axis=-1, keepdims=True) + EPS)
    mn = (m * r * g_ref[0]).astype(jnp.bfloat16)
    kv = jnp.dot(mn, w_ref[0], preferred_element_type=jnp.float32)
    km = kv[:, :MEM_WIDTH] * (MEM_HEAD_DIM ** -0.5)
    vm = kv[:, MEM_WIDTH:]
    head_of_lane = lax.broadcasted_iota(jnp.int32, (MEM_LEN, MEM_WIDTH), 1) // MEM_HEAD_DIM
    for h in range(MEM_HEADS):
        sel = head_of_lane == h
        rows = pl.ds(h * MEM_LEN, MEM_LEN)
        kbd_ref[0, rows, :] = jnp.where(sel, km, 0.0).astype(jnp.bfloat16)
        vobd_ref[0, rows, :MEM_WIDTH] = jnp.where(sel, vm, 0.0).astype(jnp.bfloat16)
        vobd_ref[0, rows, MEM_WIDTH:] = jnp.where(sel, 1.0, 0.0).astype(jnp.bfloat16)


def _mem_kv(mem, gains, weights):
    n_layers = gains.shape[0]
    rows = MEM_HEADS * MEM_LEN
    return pl.pallas_call(
        _mem_kv_kernel,
        grid=(n_layers,),
        in_specs=[
            pl.BlockSpec((MEM_LEN, D_MODEL), lambda l: (0, 0)),
            pl.BlockSpec((1, 1, D_MODEL), lambda l: (l, 0, 0)),
            pl.BlockSpec((1, D_MODEL, 2 * MEM_WIDTH), lambda l: (l, 0, 0)),
        ],
        out_specs=[
            pl.BlockSpec((1, rows, MEM_WIDTH), lambda l: (l, 0, 0)),
            pl.BlockSpec((1, rows, 2 * MEM_WIDTH), lambda l: (l, 0, 0)),
        ],
        out_shape=[
            jax.ShapeDtypeStruct((n_layers, rows, MEM_WIDTH), jnp.bfloat16),
            jax.ShapeDtypeStruct((n_layers, rows, 2 * MEM_WIDTH), jnp.bfloat16),
        ],
        compiler_params=_params("arbitrary"),
        name="mem_kv",
    )(mem, gains, weights)


def _proj_kernel(x_ref, g_ref, w_ref, o_ref):
    x = x_ref[...]
    r = lax.rsqrt(jnp.mean(x * x, axis=-1, keepdims=True) + EPS)
    h = (x * r * g_ref[...]).astype(jnp.bfloat16)
    o_ref[...] = jnp.dot(h, w_ref[...], preferred_element_type=jnp.float32).astype(o_ref.dtype)


def _norm_proj(x, gain, w):
    s, d = x.shape
    c = w.shape[1]
    n_col = 2
    tn = c // n_col
    assert tn * n_col == c and tn % LANES == 0 and s % PROJ_TM == 0
    return pl.pallas_call(
        _proj_kernel,
        grid=(n_col, s // PROJ_TM),
        in_specs=[
            pl.BlockSpec((PROJ_TM, d), lambda j, i: (i, 0)),
            pl.BlockSpec((1, d), lambda j, i: (0, 0)),
            pl.BlockSpec((d, tn), lambda j, i: (0, j)),
        ],
        out_specs=pl.BlockSpec((PROJ_TM, tn), lambda j, i: (i, j)),
        out_shape=jax.ShapeDtypeStruct((s, c), jnp.bfloat16),
        compiler_params=_params("arbitrary", "arbitrary"),
        name="norm_proj",
    )(x, gain, w)


def _proj_conv_kernel(x_ref, g_ref, w_ref, cw_ref, y_ref, qg_ref, zh_ref, h_ref):
    i = pl.program_id(0)
    x = x_ref[...]
    r = lax.rsqrt(jnp.mean(x * x, axis=-1, keepdims=True) + EPS)
    h_ref[...] = (x * r * g_ref[...]).astype(jnp.bfloat16)
    tm = x.shape[0]
    ch = CONV_BLOCK
    row = lax.broadcasted_iota(jnp.int32, (8, ch), 0)
    for j in range(BRANCH // ch):
        pr = jnp.dot(h_ref[...], w_ref[:, 4 * ch * j:4 * ch * (j + 1)],
                     preferred_element_type=jnp.float32)
        bg, cg, u, gate = (pr[:, n * ch:(n + 1) * ch] for n in range(4))
        z = cg * u
        zh = jnp.where(i > 0, zh_ref[j], 0.0)
        z1 = pltpu.roll(z, 1, 0)
        z2 = pltpu.roll(z, 2, 0)
        z1 = jnp.concatenate([jnp.where(row == 0, zh[7:8], z1[:8]), z1[8:]], axis=0)
        z2 = jnp.concatenate(
            [jnp.where(row == 0, zh[6:7], jnp.where(row == 1, zh[7:8], z2[:8])), z2[8:]], axis=0)
        cw = cw_ref[:, j * ch:(j + 1) * ch]
        mix = bg * (cw[0:1] * z2 + cw[1:2] * z1 + cw[2:3] * z)
        y_ref[:, j * ch:(j + 1) * ch] = (mix * _silu(gate)).astype(y_ref.dtype)
        zh_ref[j] = z[tm - 8:]
    qg_ref[...] = jnp.dot(h_ref[...], w_ref[:, 4 * BRANCH:],
                          preferred_element_type=jnp.float32).astype(qg_ref.dtype)


def _norm_proj_conv(x, gain, w, conv_w):
    s, d = x.shape
    c = w.shape[1]
    assert c == 4 * BRANCH + 2 * MEM_WIDTH and BRANCH % CONV_BLOCK == 0
    return pl.pallas_call(
        _proj_conv_kernel,
        grid=(s // PROJ_TM,),
        in_specs=[
            pl.BlockSpec((PROJ_TM, d), lambda i: (i, 0)),
            pl.BlockSpec((1, d), lambda i: (0, 0)),
            pl.BlockSpec((d, c), lambda i: (0, 0), pipeline_mode=pl.Buffered(1)),
            pl.BlockSpec((CONV_WIDTH, BRANCH), lambda i: (0, 0)),
        ],
        out_specs=[
            pl.BlockSpec((PROJ_TM, BRANCH), lambda i: (i, 0)),
            pl.BlockSpec((PROJ_TM, 2 * MEM_WIDTH), lambda i: (i, 0)),
        ],
        out_shape=[
            jax.ShapeDtypeStruct((s, BRANCH), jnp.bfloat16),
            jax.ShapeDtypeStruct((s, 2 * MEM_WIDTH), jnp.bfloat16),
        ],
        scratch_shapes=[pltpu.VMEM((BRANCH // CONV_BLOCK, 8, CONV_BLOCK), jnp.float32),
                        pltpu.VMEM((PROJ_TM, d), jnp.bfloat16)],
        compiler_params=_params("arbitrary"),
        name="norm_proj_conv",
    )(x, gain, w, conv_w)


def _diff_attn_kernel(q_ref, k_ref, v_ref, posk_ref, slope_ref, lq1_ref, lk1_ref, lq2_ref, lk2_ref,
                      subln_ref, o_ref, m_ref, acc_ref, s_ref, p_ref, alpha_ref, kf_ref, *, lam_init):
    tq, tk = DIFF_TQ, DIFF_TK
    group = range(DIFF_GROUP)
    i = pl.program_id(1)
    d = DIFF_HEAD_DIM
    n_blocks = SEQ // tk
    reps = tk // LANES
    n_diag = tq // tk
    n_full = i * n_diag
    head_lanes = [slice(g * LANES, (g + 1) * LANES) for g in group]

    @pl.when(i == 0)
    def _build_key_features():
        lane_t = lax.broadcasted_iota(jnp.int32, (tk, LANES), 1)

        def fill(t, carry):
            pos = posk_ref[pl.ds(pl.multiple_of(t * tk, tk), tk), :]
            hi = jnp.floor(pos * (1.0 / FEAT_RADIX))
            lo = pos - hi * FEAT_RADIX
            kf_ref[t] = jnp.where(lane_t < 3, hi, jnp.where(lane_t < 6, lo, 0.0)).astype(jnp.bfloat16)
            return carry

        lax.fori_loop(0, n_blocks, fill, 0)
        kf_ref[n_blocks] = jnp.where(lane_t == FEAT_DUMMY_LANE, 1.0, 0.0).astype(jnp.bfloat16)

    flane = lax.broadcasted_iota(jnp.int32, (1, LANES), 1)
    qqs = []
    for g in group:
        q = q_ref[:, head_lanes[g]].astype(jnp.float32) * (d ** -0.5 * LOG2E)
        lane = lax.broadcasted_iota(jnp.int32, q.shape, 1)
        qq = jnp.concatenate([jnp.where(lane < d, q, 0.0), jnp.where(lane < d, 0.0, q)],
                             axis=0).astype(jnp.bfloat16)
        slope = slope_ref[g, 0:1, :] * LOG2E
        s1 = slope.astype(jnp.bfloat16).astype(jnp.float32)
        s2 = (slope - s1).astype(jnp.bfloat16).astype(jnp.float32)
        s3 = (slope - s1 - s2).astype(jnp.bfloat16).astype(jnp.float32)
        qfeat = jnp.zeros((1, LANES), jnp.float32)
        for f, val in enumerate([s1 * FEAT_RADIX, s2 * FEAT_RADIX, s3 * FEAT_RADIX, s1, s2, s3]):
            qfeat = jnp.where(flane == f, val, qfeat)
        qfeat = jnp.where(flane == FEAT_DUMMY_LANE, NEG_INF, qfeat)
        qqs.append(jnp.concatenate(
            [qq, jnp.broadcast_to(qfeat, (2 * tq, LANES)).astype(jnp.bfloat16)], axis=1))
        m_ref[g] = jnp.full(m_ref.shape[1:], NEG_INF, dtype=jnp.float32)
        acc_ref[g] = jnp.zeros(acc_ref.shape[1:], dtype=jnp.float32)
        if n_diag == 1:
            p_ref[2 * g + 1] = jnp.zeros(p_ref.shape[1:], dtype=p_ref.dtype)
            alpha_ref[2 * g + 1] = jnp.ones(alpha_ref.shape[1:], dtype=jnp.float32)

    ones = jnp.ones((tk, LANES), dtype=jnp.bfloat16)

    def keys(g, t_key, t_feat):
        rows = pl.ds(pl.multiple_of(t_key * tk, tk), tk)
        return jnp.concatenate([k_ref[rows, head_lanes[g]], kf_ref[t_feat]], axis=1)

    def logits(g, t_key, t_feat):
        return lax.dot_general(qqs[g], keys(g, t_key, t_feat), (((1,), (1,)), ((), ())),
                               preferred_element_type=jnp.float32)

    def softmax_step(g, s):
        m_prev = m_ref[g]
        m_new = jnp.maximum(m_prev, jnp.max(s, axis=1, keepdims=True))
        m_ref[g] = m_new
        p = jnp.exp2(s - jnp.concatenate([m_new] * reps, axis=1)).astype(jnp.bfloat16)
        return p, jnp.exp2(m_prev - m_new)

    def accumulate(g, t, p, alpha):
        vb = v_ref[pl.ds(pl.multiple_of(t * tk, tk), tk), head_lanes[g]]
        pv = jnp.dot(p, jnp.concatenate([vb, ones], axis=1), preferred_element_type=jnp.float32)
        acc_ref[g] = acc_ref[g] * jnp.concatenate([alpha, alpha], axis=1) + pv

    row = lax.broadcasted_iota(jnp.int32, (2 * tq, tk), 0)
    row = jnp.where(row >= tq, row - tq, row)
    col = lax.broadcasted_iota(jnp.int32, (2 * tq, tk), 1)

    def stage_a_boundary(b):
        for g in group:
            s = logits(g, n_full + b, n_full + b)
            s_ref[2 * g + b % 2] = jnp.where(col + b * tk <= row, s, NEG_INF)

    def stage_a(b, buf):
        valid = b < n_full + n_diag
        t_key = jnp.clip(b - n_diag, 0, n_blocks - 1)
        t_feat = jnp.where(valid, b - n_diag, n_blocks)
        for g in group:
            s_ref[2 * g + buf] = logits(g, t_key, t_feat)

    def stage_b(buf):
        for g in group:
            p, alpha = softmax_step(g, s_ref[2 * g + buf])
            p_ref[2 * g + buf] = p
            alpha_ref[2 * g + buf] = alpha

    def stage_c(b, buf):
        t = jnp.clip(jnp.where(b < n_diag, n_full + b, b - n_diag), 0, n_blocks - 1)
        for g in group:
            accumulate(g, t, p_ref[2 * g + buf], alpha_ref[2 * g + buf])

    stage_a_boundary(0)
    for b in range(1, n_diag):
        if b >= 2:
            stage_c(b - 2, b % 2)
        stage_b((b - 1) % 2)
        stage_a_boundary(b)

    def run_steps(first, trips, unroll):
        def body(u, carry):
            for r in range(unroll):
                b = first + unroll * u + r + 1
                par = (n_diag + r) % 2
                stage_c(b - 2, par)
                stage_b(1 - par)
                stage_a(b, par)
            return carry
        lax.fori_loop(0, trips, body, 0)

    last_step = n_full + n_diag + 1
    done = n_diag - 1
    for size in DIFF_UNROLLS[:-1]:
        trips = (last_step - done) // size
        run_steps(done, trips, size)
        done = done + trips * size
    run_steps(done, (last_step - done + DIFF_UNROLLS[-1] - 1) // DIFF_UNROLLS[-1], DIFF_UNROLLS[-1])

    lam = (jnp.exp(jnp.sum(lq1_ref[...] * lk1_ref[...], axis=1, keepdims=True))
           - jnp.exp(jnp.sum(lq2_ref[...] * lk2_ref[...], axis=1, keepdims=True)) + lam_init)
    for g in group:
        acc = acc_ref[g]
        o = acc[:, :LANES] / acc[:, LANES:]
        o = o[:tq] - lam * o[tq:]
        r = lax.rsqrt(jnp.mean(o * o, axis=-1, keepdims=True) + EPS)
        o_ref[:, head_lanes[g]] = (o * r * subln_ref[...] * (1.0 - lam_init)).astype(o_ref.dtype)


def _diff_attention(proj, pos_col, slopes, lq1, lk1, lq2, lk2, subln, lam_init):
    s = proj.shape[0]
    tq, tk, n_group = DIFF_TQ, DIFF_TK, DIFF_GROUP
    assert tq % tk == 0 and s % tq == 0 and all(u % 2 == 0 for u in DIFF_UNROLLS)
    assert DIFF_HEADS % n_group == 0
    gw = n_group * LANES
    gb = BRANCH // gw
    vec = lambda n: pl.BlockSpec((1, n), lambda h, i: (0, 0))
    resident = pl.BlockSpec
    return pl.pallas_call(
        functools.partial(_diff_attn_kernel, lam_init=lam_init),
        grid=(DIFF_HEADS // n_group, s // tq),
        in_specs=[
            pl.BlockSpec((tq, gw), lambda h, i: (i, h)),
            resident((s, gw), lambda h, i: (0, gb + h)),
            resident((s, gw), lambda h, i: (0, 2 * gb + h)),
            resident((s, LANES), lambda h, i: (0, 0)),
            pl.BlockSpec((n_group, 8, LANES), lambda h, i: (h, 0, 0)),
            vec(DIFF_HEAD_DIM), vec(DIFF_HEAD_DIM), vec(DIFF_HEAD_DIM), vec(DIFF_HEAD_DIM),
            vec(2 * DIFF_HEAD_DIM),
        ],
        out_specs=pl.BlockSpec((tq, gw), lambda h, i: (i, h)),
        out_shape=jax.ShapeDtypeStruct((s, BRANCH), jnp.bfloat16),
        scratch_shapes=[
            pltpu.VMEM((n_group, 2 * tq, LANES), jnp.float32),
            pltpu.VMEM((n_group, 2 * tq, 2 * LANES), jnp.float32),
            pltpu.VMEM((2 * n_group, 2 * tq, tk), jnp.float32),
            pltpu.VMEM((2 * n_group, 2 * tq, tk), jnp.bfloat16),
            pltpu.VMEM((2 * n_group, 2 * tq, LANES), jnp.float32),
            pltpu.VMEM((s // tk + 1, tk, LANES), jnp.bfloat16),
        ],
        compiler_params=_params("arbitrary", "arbitrary"),
        name="diff_attention",
    )(proj, proj, proj, pos_col, slopes, lq1, lk1, lq2, lk2, subln)


def _swap_halves(x):
    half = LANES // 2
    return jnp.concatenate([x[:, half:], x[:, :half]], axis=1)


def _swa_block(q_ref, k_ref, v_ref, row0, pq, pk_prev, pk_cur, prev_valid, slope_ref, sink_ref, g_ref,
               y_ref):
    w = WINDOW
    d = SWA_HEAD_DIM
    rows = pl.ds(row0, w)
    cols_per_kv = SWA_GROUP // 2
    rel = jnp.concatenate([pq - pk_prev, pq - pk_cur], axis=1)
    col = lax.broadcasted_iota(jnp.int32, (w, 2 * w), 1)
    valid = (rel >= 0.0) & (rel < float(w)) & ((col >= w) | prev_valid)
    rel_masked = jnp.where(valid, rel, -NEG_INF)
    lane_half = lax.broadcasted_iota(jnp.int32, (2 * w, LANES), 1) // d
    out_half = lax.broadcasted_iota(jnp.int32, (w, LANES), 1) // d
    ones = jnp.ones((2 * w, LANES), dtype=jnp.bfloat16)

    pairs = []
    for pair in range(SWA_KV_HEADS // 2):
        lanes = slice(pair * LANES, (pair + 1) * LANES)
        kk = k_ref[pl.ds(row0, 2 * w), lanes]
        kk = (kk.astype(jnp.float32) * (d ** -0.5 * LOG2E)).astype(jnp.bfloat16)
        vv = v_ref[pl.ds(row0, 2 * w), lanes]
        pairs.append((kk, _swap_halves(kk), vv, _swap_halves(vv)))

    def operands(n):
        c, t = divmod(n, 2)
        pair, e = divmod(c, 2)
        kk, kk_sw, vv, vv_sw = pairs[pair]
        zero = jnp.zeros_like(kk)
        if t == 0:
            rhs, val = jnp.where(lane_half == e, kk, zero), vv
        else:
            rhs, val = jnp.where(lane_half == e, zero, kk_sw), vv_sw
        heads = [c * SWA_GROUP + 2 * u + (e if t == 0 else 1 - e) for u in range(cols_per_kv)]
        return c, e, rhs, val, heads

    def logits(n):
        c, _, rhs, _, _ = operands(n)
        lhs = jnp.concatenate(
            [q_ref[rows, pl.ds((c * cols_per_kv + u) * LANES, LANES)] for u in range(cols_per_kv)], axis=0)
        return lax.dot_general(lhs, rhs, (((1,), (1,)), ((), ())), preferred_element_type=jnp.float32)

    def softmax(n, s):
        heads = operands(n)[4]
        ex, stats = [], []
        for u, head in enumerate(heads):
            slope = slope_ref[head] * LOG2E
            su = s[u * w:(u + 1) * w] - jnp.concatenate([slope, slope], axis=1) * rel_masked
            sink = sink_ref[head] * LOG2E
            m = jnp.maximum(jnp.max(su, axis=1, keepdims=True), sink)
            ex.append(jnp.exp2(su - jnp.concatenate([m, m], axis=1)).astype(jnp.bfloat16))
            stats.append(jnp.exp2(sink - m))
        return jnp.concatenate(ex, axis=0), stats

    def values(n, ex, stats):
        val = operands(n)[3]
        acc = jnp.dot(ex, jnp.concatenate([val, ones], axis=1), preferred_element_type=jnp.float32)
        return [acc[u * w:(u + 1) * w, :LANES] / (acc[u * w:(u + 1) * w, LANES:] + stats[u])
                for u in range(cols_per_kv)]

    n_batches = 2 * SWA_KV_HEADS
    ahead = 1
    s = {n: logits(n) for n in range(ahead)}
    outs = {}
    for n in range(n_batches):
        if n + ahead < n_batches:
            s[n + ahead] = logits(n + ahead)
        ex, stats = softmax(n, s.pop(n))
        outs[n] = values(n, ex, stats)
        if n % 2 == 1:
            c, e = operands(n)[:2]
            merged = [jnp.where(out_half == e, outs[n - 1][u], outs[n][u]) for u in range(cols_per_kv)]
            cols = pl.ds(c * cols_per_kv * LANES, cols_per_kv * LANES)
            y_ref[rows, cols] = jnp.concatenate(merged, axis=1).astype(y_ref.dtype) * g_ref[rows, cols]


def _proj_swa_kernel(x_ref, g_ref, w_ref, pq_ref, pk_ref, pkp_ref, slope_ref, sink_ref, y_ref, qg_ref,
                     h_ref, q_ref, k_ref, v_ref, gate_ref):
    i = pl.program_id(0)
    w = WINDOW
    tm = x_ref.shape[0]
    kvw = SWA_KV_HEADS * SWA_HEAD_DIM

    @pl.when(i == 0)
    def _no_history():
        k_ref[0:w, :] = jnp.zeros((w, kvw), k_ref.dtype)
        v_ref[0:w, :] = jnp.zeros((w, kvw), v_ref.dtype)

    x = x_ref[...]
    r = lax.rsqrt(jnp.mean(x * x, axis=-1, keepdims=True) + EPS)
    h_ref[...] = (x * r * g_ref[...]).astype(jnp.bfloat16)
    dot = lambda lo, hi: jnp.dot(h_ref[...], w_ref[:, lo:hi], preferred_element_type=jnp.float32)
    q_ref[...] = dot(0, BRANCH).astype(q_ref.dtype)
    gate_ref[...] = _silu(dot(BRANCH, 2 * BRANCH)).astype(gate_ref.dtype)
    rest = dot(2 * BRANCH, 2 * BRANCH + 2 * kvw + 2 * MEM_WIDTH)
    k_ref[w:, :] = rest[:, :kvw].astype(k_ref.dtype)
    v_ref[w:, :] = rest[:, kvw:2 * kvw].astype(v_ref.dtype)
    qg_ref[...] = rest[:, 2 * kvw:].astype(qg_ref.dtype)

    for b in range(tm // w):
        pk_prev = pkp_ref[...] if b == 0 else pk_ref[:, (b - 1) * w:b * w]
        _swa_block(q_ref, k_ref, v_ref, b * w, pq_ref[b * w:(b + 1) * w, :], pk_prev,
                   pk_ref[:, b * w:(b + 1) * w], (i > 0) if b == 0 else True,
                   slope_ref, sink_ref, gate_ref, y_ref)

    k_ref[0:w, :] = k_ref[tm:tm + w, :]
    v_ref[0:w, :] = v_ref[tm:tm + w, :]


def _norm_proj_swa(x, gain, w, pos_col, pos_row, slopes, sinks):
    s, d = x.shape
    c = w.shape[1]
    tm = PROJ_TM
    kvw = SWA_KV_HEADS * SWA_HEAD_DIM
    assert c == 2 * BRANCH + 2 * kvw + 2 * MEM_WIDTH and tm % WINDOW == 0
    blocks = tm // WINDOW
    return pl.pallas_call(
        _proj_swa_kernel,
        grid=(s // tm,),
        in_specs=[
            pl.BlockSpec((tm, d), lambda i: (i, 0)),
            pl.BlockSpec((1, d), lambda i: (0, 0)),
            pl.BlockSpec((d, c), lambda i: (0, 0), pipeline_mode=pl.Buffered(1)),
            pl.BlockSpec((tm, LANES), lambda i: (i, 0)),
            pl.BlockSpec((1, tm), lambda i: (0, i)),
            pl.BlockSpec((1, WINDOW), lambda i: (0, jnp.maximum(i * blocks - 1, 0))),
            pl.BlockSpec((SWA_Q_HEADS, 1, LANES), lambda i: (0, 0, 0)),
            pl.BlockSpec((SWA_Q_HEADS, 1, LANES), lambda i: (0, 0, 0)),
        ],
        out_specs=[
            pl.BlockSpec((tm, BRANCH), lambda i: (i, 0)),
            pl.BlockSpec((tm, 2 * MEM_WIDTH), lambda i: (i, 0)),
        ],
        out_shape=[
            jax.ShapeDtypeStruct((s, BRANCH), jnp.bfloat16),
            jax.ShapeDtypeStruct((s, 2 * MEM_WIDTH), jnp.bfloat16),
        ],
        scratch_shapes=[
            pltpu.VMEM((tm, d), jnp.bfloat16),
            pltpu.VMEM((tm, BRANCH), jnp.bfloat16),
            pltpu.VMEM((tm + WINDOW, kvw), jnp.bfloat16),
            pltpu.VMEM((tm + WINDOW, kvw), jnp.bfloat16),
            pltpu.VMEM((tm, BRANCH), jnp.bfloat16),
        ],
        compiler_params=_params("arbitrary"),
        name="norm_proj_swa",
    )(x, gain, w, pos_col, pos_row, pos_row, slopes, sinks)


def _silu(g):
    return g / (1.0 + jnp.exp(-g))


def _out_kernel(*refs, gated):
    if gated:
        x_ref, ymix_ref, qm_ref, gmem_ref, kbd_ref, vobd_ref, wo_ref, gpost_ref, o_ref = refs
        y_mix = ymix_ref[...]
    else:
        (x_ref, mix_ref, gmix_ref, qm_ref, gmem_ref, kbd_ref, vobd_ref, wo_ref, gpost_ref,
         o_ref) = refs
        y_mix = mix_ref[...] * _silu(gmix_ref[...])

    s = lax.dot_general(qm_ref[...], kbd_ref[0], (((1,), (1,)), ((), ())),
                        preferred_element_type=jnp.float32)
    ps = []
    for h in range(MEM_HEADS):
        sh = s[:, h * MEM_LEN:(h + 1) * MEM_LEN]
        ps.append(jnp.exp(sh - jnp.max(sh, axis=1, keepdims=True)).astype(jnp.bfloat16))
    nd = jnp.dot(jnp.concatenate(ps, axis=1), vobd_ref[0], preferred_element_type=jnp.float32)
    mem_out = nd[:, :MEM_WIDTH] / nd[:, MEM_WIDTH:]

    y_mem = mem_out.astype(jnp.bfloat16) * _silu(gmem_ref[...])
    y = (jnp.dot(y_mix, wo_ref[:BRANCH, :], preferred_element_type=jnp.float32)
         + jnp.dot(y_mem, wo_ref[BRANCH:, :], preferred_element_type=jnp.float32))
    r = lax.rsqrt(jnp.mean(y * y, axis=-1, keepdims=True) + EPS)
    o_ref[...] = x_ref[...] + y * r * gpost_ref[...]


def _out_layer(x, mix, gate_mix, q_mem, gate_mem, kbd, vobd, layer, w_out, g_post):
    s, d = x.shape
    tm = OUT_TM
    gated = gate_mix is None
    col_blk = lambda width, src: pl.BlockSpec((tm, width), lambda i: (i, src[1] // width))
    in_specs = [pl.BlockSpec((tm, d), lambda i: (i, 0)), pl.BlockSpec((tm, BRANCH), lambda i: (i, 0))]
    args = [x, mix]
    if not gated:
        in_specs.append(col_blk(BRANCH, gate_mix))
        args.append(gate_mix[0])
    in_specs += [
        col_blk(MEM_WIDTH, q_mem), col_blk(MEM_WIDTH, gate_mem),
        pl.BlockSpec((1,) + kbd.shape[1:], lambda i: (layer, 0, 0)),
        pl.BlockSpec((1,) + vobd.shape[1:], lambda i: (layer, 0, 0)),
        pl.BlockSpec((GATE_WIDTH, d), lambda i: (0, 0)),
        pl.BlockSpec((1, d), lambda i: (0, 0)),
    ]
    args += [q_mem[0], gate_mem[0], kbd, vobd, w_out, g_post]
    return pl.pallas_call(
        functools.partial(_out_kernel, gated=gated),
        grid=(s // tm,),
        in_specs=in_specs,
        out_specs=pl.BlockSpec((tm, d), lambda i: (i, 0)),
        out_shape=jax.ShapeDtypeStruct((s, d), jnp.float32),
        compiler_params=_params("arbitrary"),
        name="out_gated" if gated else "out_attn",
    )(*args)


def _relayout_w_in(w, kind):
    if kind == 2:
        q, k, v, q_mem, gate = jnp.split(w, [2048, 2304, 2560, 2816], axis=1)
        parts = [q, gate[:, :BRANCH], k, v, q_mem, gate[:, BRANCH:]]
        cols = dict(q_mem=0, gate_mem=MEM_WIDTH)
    elif kind == 1:
        a, b, c, q_mem, gate = jnp.split(w, [2048, 4096, 6144, 6400], axis=1)
        parts = [a, b, c, gate[:, :BRANCH], q_mem, gate[:, BRANCH:]]
        cols = dict(a=0, b=2048, c=4096, gate_mix=6144, q_mem=8192, gate_mem=8448)
    else:
        a, b, c, q_mem, gate = jnp.split(w, [2048, 4096, 6144, 6400], axis=1)
        parts = []
        for j in range(BRANCH // CONV_BLOCK):
            ch = slice(j * CONV_BLOCK, (j + 1) * CONV_BLOCK)
            parts += [a[:, ch], b[:, ch], c[:, ch], gate[:, ch]]
        parts += [q_mem, gate[:, BRANCH:]]
        cols = dict(q_mem=0, gate_mem=MEM_WIDTH)
    return jnp.concatenate([p.astype(jnp.bfloat16) for p in parts], axis=1), cols


def _alibi_slopes(n_heads):
    return 2.0 ** (-ALIBI_MAX_BIAS * jnp.arange(1, n_heads + 1, dtype=jnp.float32) / n_heads)


def kernel(x, mem, positions, norm_pre_0, norm_post_0, norm_mem_0, w_in_0, w_mem_kv_0, conv_w_0, w_out_0, norm_pre_1, norm_post_1, norm_mem_1, w_in_1, w_mem_kv_1, lambda_q1_1, lambda_k1_1, lambda_q2_1, lambda_k2_1, subln_1, w_out_1, norm_pre_2, norm_post_2, norm_mem_2, w_in_2, w_mem_kv_2, sinks_2, w_out_2, norm_pre_3, norm_post_3, norm_mem_3, w_in_3, w_mem_kv_3, conv_w_3, w_out_3):
    b, s, d = x.shape
    assert b == 1 and s == SEQ and d == D_MODEL
    xs = x.reshape(s, d)
    pos_f = positions.reshape(s).astype(jnp.float32)
    pos_row = pos_f.reshape(1, s)
    pos_col = jnp.broadcast_to(pos_f[:, None], (s, LANES))

    pre = [norm_pre_0, norm_pre_1, norm_pre_2, norm_pre_3]
    post = [norm_post_0, norm_post_1, norm_post_2, norm_post_3]
    w_in = [w_in_0, w_in_1, w_in_2, w_in_3]
    w_out = [w_out_0, w_out_1, w_out_2, w_out_3]
    conv_w = {0: conv_w_0, 3: conv_w_3}

    mem_gain = jnp.stack([norm_mem_0, norm_mem_1, norm_mem_2, norm_mem_3]).reshape(DEPTH, 1, d)
    mem_w = jnp.stack([w_mem_kv_0, w_mem_kv_1, w_mem_kv_2, w_mem_kv_3]).astype(jnp.bfloat16)
    kbd, vobd = _mem_kv(mem.reshape(MEM_LEN, d), mem_gain, mem_w)

    for layer in range(DEPTH):
        kind = layer % 3
        w, cols = _relayout_w_in(w_in[layer], kind)
        gain = pre[layer].reshape(1, d)
        gate_mix = None
        if kind == 0:
            mix, proj = _norm_proj_conv(xs, gain, w, conv_w[layer])
        elif kind == 2:
            slopes = jnp.broadcast_to(_alibi_slopes(SWA_Q_HEADS)[:, None, None], (SWA_Q_HEADS, 1, LANES))
            sinks = jnp.broadcast_to(sinks_2.astype(jnp.float32)[:, None, None], (SWA_Q_HEADS, 1, LANES))
            mix, proj = _norm_proj_swa(xs, gain, w, pos_col, pos_row, slopes, sinks)
        else:
            proj = _norm_proj(xs, gain, w)
            gate_mix = (proj, cols["gate_mix"])
            slopes = jnp.broadcast_to(_alibi_slopes(DIFF_HEADS)[:, None, None], (DIFF_HEADS, 8, LANES))
            lam_init = 0.8 - 0.6 * math.exp(-0.3 * layer)
            mix = _diff_attention(
                proj, pos_col, slopes,
                lambda_q1_1.reshape(1, -1), lambda_k1_1.reshape(1, -1),
                lambda_q2_1.reshape(1, -1), lambda_k2_1.reshape(1, -1),
                subln_1.reshape(1, -1), lam_init)
        xs = _out_layer(xs, mix, gate_mix, (proj, cols["q_mem"]), (proj, cols["gate_mem"]), kbd, vobd,
                        layer, w_out[layer].astype(jnp.bfloat16), post[layer].reshape(1, d))
    return xs.reshape(b, s, d)
```

```python
import functools
import math

import jax
import jax.numpy as jnp
from jax import lax
from jax.experimental import pallas as pl
from jax.experimental.pallas import tpu as pltpu

D_MODEL = 1024
SEQ = 16384
DEPTH = 4
MEM_LEN = 256
BRANCH = 2048
CONV_WIDTH = 3
DIFF_HEAD_DIM = 64
DIFF_HEADS = 16
SWA_HEAD_DIM = 64
SWA_Q_HEADS = 32
SWA_KV_HEADS = 4
SWA_GROUP = SWA_Q_HEADS // SWA_KV_HEADS
WINDOW = 128
MEM_HEADS = 4
MEM_HEAD_DIM = 64
MEM_WIDTH = 256
GATE_WIDTH = BRANCH + MEM_WIDTH
ALIBI_MAX_BIAS = 8.0
EPS = 1e-6
NEG_INF = -1e30

LANES = 128
VMEM_LIMIT_BYTES = 56 * 1024 * 1024

PROJ_TM = 512
CONV_BLOCK = 256
OUT_TM = 512
DIFF_TQ = 512
DIFF_TK = 256
DIFF_GROUP = 1
DIFF_UNROLLS = (16, 8, 4, 2)
FEAT_RADIX = 128.0
FEAT_DUMMY_LANE = 6
LOG2E = math.log2(math.e)


def _params(*sem):
    return pltpu.CompilerParams(dimension_semantics=sem, vmem_limit_bytes=VMEM_LIMIT_BYTES)


def _mem_kv_kernel(mem_ref, g_ref, w_ref, kbd_ref, vobd_ref):
    m = mem_ref[...]
    r = lax.rsqrt(jnp.mean(m * m, axis=-1, keepdims=True) + EPS)
    mn = (m * r * g_ref[0]).astype(jnp.bfloat16)
    kv = jnp.dot(mn, w_ref[0], preferred_element_type=jnp.float32)
    km = kv[:, :MEM_WIDTH] * (MEM_HEAD_DIM ** -0.5)
    vm = kv[:, MEM_WIDTH:]
    head_of_lane = lax.broadcasted_iota(jnp.int32, (MEM_LEN, MEM_WIDTH), 1) // MEM_HEAD_DIM
    for h in range(MEM_HEADS):
        sel = head_of_lane == h
        rows = pl.ds(h * MEM_LEN, MEM_LEN)
        kbd_ref[0, rows, :] = jnp.where(sel, km, 0.0).astype(jnp.bfloat16)
        vobd_ref[0, rows, :MEM_WIDTH] = jnp.where(sel, vm, 0.0).astype(jnp.bfloat16)
        vobd_ref[0, rows, MEM_WIDTH:] = jnp.where(sel, 1.0, 0.0).astype(jnp.bfloat16)


def _mem_kv(mem, gains, weights):
    n_layers = gains.shape[0]
    rows = MEM_HEADS * MEM_LEN
    return pl.pallas_call(
        _mem_kv_kernel,
        grid=(n_layers,),
        in_specs=[
            pl.BlockSpec((MEM_LEN, D_MODEL), lambda l: (0, 0)),
            pl.BlockSpec((1, 1, D_MODEL), lambda l: (l, 0, 0)),
            pl.BlockSpec((1, D_MODEL, 2 * MEM_WIDTH), lambda l: (l, 0, 0)),
        ],
        out_specs=[
            pl.BlockSpec((1, rows, MEM_WIDTH), lambda l: (l, 0, 0)),
            pl.BlockSpec((1, rows, 2 * MEM_WIDTH), lambda l: (l, 0, 0)),
        ],
        out_shape=[
            jax.ShapeDtypeStruct((n_layers, rows, MEM_WIDTH), jnp.bfloat16),
            jax.ShapeDtypeStruct((n_layers, rows, 2 * MEM_WIDTH), jnp.bfloat16),
        ],
        compiler_params=_params("arbitrary"),
        name="mem_kv",
    )(mem, gains, weights)


def _proj_kernel(x_ref, g_ref, w_ref, o_ref):
    x = x_ref[...]
    r = lax.rsqrt(jnp.mean(x * x, axis=-1, keepdims=True) + EPS)
    h = (x * r * g_ref[...]).astype(jnp.bfloat16)
    o_ref[...] = jnp.dot(h, w_ref[...], preferred_element_type=jnp.float32).astype(o_ref.dtype)


def _norm_proj(x, gain, w):
    s, d = x.shape
    c = w.shape[1]
    n_col = 2
    tn = c // n_col
    assert tn * n_col == c and tn % LANES == 0 and s % PROJ_TM == 0
    return pl.pallas_call(
        _proj_kernel,
        grid=(n_col, s // PROJ_TM),
        in_specs=[
            pl.BlockSpec((PROJ_TM, d), lambda j, i: (i, 0)),
            pl.BlockSpec((1, d), lambda j, i: (0, 0)),
            pl.BlockSpec((d, tn), lambda j, i: (0, j)),
        ],
        out_specs=pl.BlockSpec((PROJ_TM, tn), lambda j, i: (i, j)),
        out_shape=jax.ShapeDtypeStruct((s, c), jnp.bfloat16),
        compiler_params=_params("arbitrary", "arbitrary"),
        name="norm_proj",
    )(x, gain, w)


def _proj_conv_kernel(x_ref, g_ref, w_ref, cw_ref, y_ref, qg_ref, zh_ref, h_ref):
    i = pl.program_id(0)
    x = x_ref[...]
    r = lax.rsqrt(jnp.mean(x * x, axis=-1, keepdims=True) + EPS)
    h_ref[...] = (x * r * g_ref[...]).astype(jnp.bfloat16)
    tm = x.shape[0]
    ch = CONV_BLOCK
    row = lax.broadcasted_iota(jnp.int32, (8, ch), 0)
    for j in range(BRANCH // ch):
        pr = jnp.dot(h_ref[...], w_ref[:, 4 * ch * j:4 * ch * (j + 1)],
                     preferred_element_type=jnp.float32)
        bg, cg, u, gate = (pr[:, n * ch:(n + 1) * ch] for n in range(4))
        z = cg * u
        zh = jnp.where(i > 0, zh_ref[j], 0.0)
        z1 = pltpu.roll(z, 1, 0)
        z2 = pltpu.roll(z, 2, 0)
        z1 = jnp.concatenate([jnp.where(row == 0, zh[7:8], z1[:8]), z1[8:]], axis=0)
        z2 = jnp.concatenate(
            [jnp.where(row == 0, zh[6:7], jnp.where(row == 1, zh[7:8], z2[:8])), z2[8:]], axis=0)
        cw = cw_ref[:, j * ch:(j + 1) * ch]
        mix = bg * (cw[0:1] * z2 + cw[1:2] * z1 + cw[2:3] * z)
        y_ref[:, j * ch:(j + 1) * ch] = (mix * _silu(gate)).astype(y_ref.dtype)
        zh_ref[j] = z[tm - 8:]
    qg_ref[...] = jnp.dot(h_ref[...], w_ref[:, 4 * BRANCH:],
                          preferred_element_type=jnp.float32).astype(qg_ref.dtype)


def _norm_proj_conv(x, gain, w, conv_w):
    s, d = x.shape
    c = w.shape[1]
    assert c == 4 * BRANCH + 2 * MEM_WIDTH and BRANCH % CONV_BLOCK == 0
    return pl.pallas_call(
        _proj_conv_kernel,
        grid=(s // PROJ_TM,),
        in_specs=[
            pl.BlockSpec((PROJ_TM, d), lambda i: (i, 0)),
            pl.BlockSpec((1, d), lambda i: (0, 0)),
            pl.BlockSpec((d, c), lambda i: (0, 0), pipeline_mode=pl.Buffered(1)),
            pl.BlockSpec((CONV_WIDTH, BRANCH), lambda i: (0, 0)),
        ],
        out_specs=[
            pl.BlockSpec((PROJ_TM, BRANCH), lambda i: (i, 0)),
            pl.BlockSpec((PROJ_TM, 2 * MEM_WIDTH), lambda i: (i, 0)),
        ],
        out_shape=[
            jax.ShapeDtypeStruct((s, BRANCH), jnp.bfloat16),
            jax.ShapeDtypeStruct((s, 2 * MEM_WIDTH), jnp.bfloat16),
        ],
        scratch_shapes=[pltpu.VMEM((BRANCH // CONV_BLOCK, 8, CONV_BLOCK), jnp.float32),
                        pltpu.VMEM((PROJ_TM, d), jnp.bfloat16)],
        compiler_params=_params("arbitrary"),
        name="norm_proj_conv",
    )(x, gain, w, conv_w)


def _diff_attn_kernel(q_ref, k_ref, v_ref, posk_ref, slope_ref, lq1_ref, lk1_ref, lq2_ref, lk2_ref,
                      subln_ref, o_ref, m_ref, acc_ref, s_ref, p_ref, alpha_ref, kf_ref, *, lam_init):
    tq, tk = DIFF_TQ, DIFF_TK
    group = range(DIFF_GROUP)
    i = pl.program_id(1)
    d = DIFF_HEAD_DIM
    n_blocks = SEQ // tk
    reps = tk // LANES
    n_diag = tq // tk
    n_full = i * n_diag
    head_lanes = [slice(g * LANES, (g + 1) * LANES) for g in group]

    @pl.when(i == 0)
    def _build_key_features():
        lane_t = lax.broadcasted_iota(jnp.int32, (tk, LANES), 1)

        def fill(t, carry):
            pos = posk_ref[pl.ds(pl.multiple_of(t * tk, tk), tk), :]
            hi = jnp.floor(pos * (1.0 / FEAT_RADIX))
            lo = pos - hi * FEAT_RADIX
            kf_ref[t] = jnp.where(lane_t < 3, hi, jnp.where(lane_t < 6, lo, 0.0)).astype(jnp.bfloat16)
            return carry

        lax.fori_loop(0, n_blocks, fill, 0)
        kf_ref[n_blocks] = jnp.where(lane_t == FEAT_DUMMY_LANE, 1.0, 0.0).astype(jnp.bfloat16)

    flane = lax.broadcasted_iota(jnp.int32, (1, LANES), 1)
    qqs = []
    for g in group:
        q = q_ref[:, head_lanes[g]].astype(jnp.float32) * (d ** -0.5 * LOG2E)
        lane = lax.broadcasted_iota(jnp.int32, q.shape, 1)
        qq = jnp.concatenate([jnp.where(lane < d, q, 0.0), jnp.where(lane < d, 0.0, q)],
                             axis=0).astype(jnp.bfloat16)
        slope = slope_ref[g, 0:1, :] * LOG2E
        s1 = slope.astype(jnp.bfloat16).astype(jnp.float32)
        s2 = (slope - s1).astype(jnp.bfloat16).astype(jnp.float32)
        s3 = (slope - s1 - s2).astype(jnp.bfloat16).astype(jnp.float32)
        qfeat = jnp.zeros((1, LANES), jnp.float32)
        for f, val in enumerate([s1 * FEAT_RADIX, s2 * FEAT_RADIX, s3 * FEAT_RADIX, s1, s2, s3]):
            qfeat = jnp.where(flane == f, val, qfeat)
        qfeat = jnp.where(flane == FEAT_DUMMY_LANE, NEG_INF, qfeat)
        qqs.append(jnp.concatenate(
            [qq, jnp.broadcast_to(qfeat, (2 * tq, LANES)).astype(jnp.bfloat16)], axis=1))
        m_ref[g] = jnp.full(m_ref.shape[1:], NEG_INF, dtype=jnp.float32)
        acc_ref[g] = jnp.zeros(acc_ref.shape[1:], dtype=jnp.float32)
        if n_diag == 1:
            p_ref[2 * g + 1] = jnp.zeros(p_ref.shape[1:], dtype=p_ref.dtype)
            alpha_ref[2 * g + 1] = jnp.ones(alpha_ref.shape[1:], dtype=jnp.float32)

    ones = jnp.ones((tk, LANES), dtype=jnp.bfloat16)

    def keys(g, t_key, t_feat):
        rows = pl.ds(pl.multiple_of(t_key * tk, tk), tk)
        return jnp.concatenate([k_ref[rows, head_lanes[g]], kf_ref[t_feat]], axis=1)

    def logits(g, t_key, t_feat):
        return lax.dot_general(qqs[g], keys(g, t_key, t_feat), (((1,), (1,)), ((), ())),
                               preferred_element_type=jnp.float32)

    def softmax_step(g, s):
        m_prev = m_ref[g]
        m_new = jnp.maximum(m_prev, jnp.max(s, axis=1, keepdims=True))
        m_ref[g] = m_new
        p = jnp.exp2(s - jnp.concatenate([m_new] * reps, axis=1)).astype(jnp.bfloat16)
        return p, jnp.exp2(m_prev - m_new)

    def accumulate(g, t, p, alpha):
        vb = v_ref[pl.ds(pl.multiple_of(t * tk, tk), tk), head_lanes[g]]
        pv = jnp.dot(p, jnp.concatenate([vb, ones], axis=1), preferred_element_type=jnp.float32)
        acc_ref[g] = acc_ref[g] * jnp.concatenate([alpha, alpha], axis=1) + pv

    row = lax.broadcasted_iota(jnp.int32, (2 * tq, tk), 0)
    row = jnp.where(row >= tq, row - tq, row)
    col = lax.broadcasted_iota(jnp.int32, (2 * tq, tk), 1)

    def stage_a_boundary(b):
        for g in group:
            s = logits(g, n_full + b, n_full + b)
            s_ref[2 * g + b % 2] = jnp.where(col + b * tk <= row, s, NEG_INF)

    def stage_a(b, buf):
        valid = b < n_full + n_diag
        t_key = jnp.clip(b - n_diag, 0, n_blocks - 1)
        t_feat = jnp.where(valid, b - n_diag, n_blocks)
        for g in group:
            s_ref[2 * g + buf] = logits(g, t_key, t_feat)

    def stage_b(buf):
        for g in group:
            p, alpha = softmax_step(g, s_ref[2 * g + buf])
            p_ref[2 * g + buf] = p
            alpha_ref[2 * g + buf] = alpha

    def stage_c(b, buf):
        t = jnp.clip(jnp.where(b < n_diag, n_full + b, b - n_diag), 0, n_blocks - 1)
        for g in group:
            accumulate(g, t, p_ref[2 * g + buf], alpha_ref[2 * g + buf])

    stage_a_boundary(0)
    for b in range(1, n_diag):
        if b >= 2:
            stage_c(b - 2, b % 2)
        stage_b((b - 1) % 2)
        stage_a_boundary(b)

    def run_steps(first, trips, unroll):
        def body(u, carry):
            for r in range(unroll):
                b = first + unroll * u + r + 1
                par = (n_diag + r) % 2
                stage_c(b - 2, par)
                stage_b(1 - par)
                stage_a(b, par)
            return carry
        lax.fori_loop(0, trips, body, 0)

    last_step = n_full + n_diag + 1
    done = n_diag - 1
    for size in DIFF_UNROLLS[:-1]:
        trips = (last_step - done) // size
        run_steps(done, trips, size)
        done = done + trips * size
    run_steps(done, (last_step - done + DIFF_UNROLLS[-1] - 1) // DIFF_UNROLLS[-1], DIFF_UNROLLS[-1])

    lam = (jnp.exp(jnp.sum(lq1_ref[...] * lk1_ref[...], axis=1, keepdims=True))
           - jnp.exp(jnp.sum(lq2_ref[...] * lk2_ref[...], axis=1, keepdims=True)) + lam_init)
    for g in group:
        acc = acc_ref[g]
        o = acc[:, :LANES] / acc[:, LANES:]
        o = o[:tq] - lam * o[tq:]
        r = lax.rsqrt(jnp.mean(o * o, axis=-1, keepdims=True) + EPS)
        o_ref[:, head_lanes[g]] = (o * r * subln_ref[...] * (1.0 - lam_init)).astype(o_ref.dtype)


def _diff_attention(proj, pos_col, slopes, lq1, lk1, lq2, lk2, subln, lam_init):
    s = proj.shape[0]
    tq, tk, n_group = DIFF_TQ, DIFF_TK, DIFF_GROUP
    assert tq % tk == 0 and s % tq == 0 and all(u % 2 == 0 for u in DIFF_UNROLLS)
    assert DIFF_HEADS % n_group == 0
    gw = n_group * LANES
    gb = BRANCH // gw
    vec = lambda n: pl.BlockSpec((1, n), lambda h, i: (0, 0))
    resident = pl.BlockSpec
    return pl.pallas_call(
        functools.partial(_diff_attn_kernel, lam_init=lam_init),
        grid=(DIFF_HEADS // n_group, s // tq),
        in_specs=[
            pl.BlockSpec((tq, gw), lambda h, i: (i, h)),
            resident((s, gw), lambda h, i: (0, gb + h)),
            resident((s, gw), lambda h, i: (0, 2 * gb + h)),
            resident((s, LANES), lambda h, i: (0, 0)),
            pl.BlockSpec((n_group, 8, LANES), lambda h, i: (h, 0, 0)),
            vec(DIFF_HEAD_DIM), vec(DIFF_HEAD_DIM), vec(DIFF_HEAD_DIM), vec(DIFF_HEAD_DIM),
            vec(2 * DIFF_HEAD_DIM),
        ],
        out_specs=pl.BlockSpec((tq, gw), lambda h, i: (i, h)),
        out_shape=jax.ShapeDtypeStruct((s, BRANCH), jnp.bfloat16),
        scratch_shapes=[
            pltpu.VMEM((n_group, 2 * tq, LANES), jnp.float32),
            pltpu.VMEM((n_group, 2 * tq, 2 * LANES), jnp.float32),
            pltpu.VMEM((2 * n_group, 2 * tq, tk), jnp.float32),
            pltpu.VMEM((2 * n_group, 2 * tq, tk), jnp.bfloat16),
            pltpu.VMEM((2 * n_group, 2 * tq, LANES), jnp.float32),
            pltpu.VMEM((s // tk + 1, tk, LANES), jnp.bfloat16),
        ],
        compiler_params=_params("arbitrary", "arbitrary"),
        name="diff_attention",
    )(proj, proj, proj, pos_col, slopes, lq1, lk1, lq2, lk2, subln)


def _swap_halves(x):
    half = LANES // 2
    return jnp.concatenate([x[:, half:], x[:, :half]], axis=1)


def _swa_block(q_ref, k_ref, v_ref, row0, pq, pk_prev, pk_cur, prev_valid, slope_ref, sink_ref, g_ref,
               y_ref):
    w = WINDOW
    d = SWA_HEAD_DIM
    rows = pl.ds(row0, w)
    cols_per_kv = SWA_GROUP // 2
    rel = jnp.concatenate([pq - pk_prev, pq - pk_cur], axis=1)
    col = lax.broadcasted_iota(jnp.int32, (w, 2 * w), 1)
    valid = (rel >= 0.0) & (rel < float(w)) & ((col >= w) | prev_valid)
    rel_masked = jnp.where(valid, rel, -NEG_INF)
    lane_half = lax.broadcasted_iota(jnp.int32, (2 * w, LANES), 1) // d
    out_half = lax.broadcasted_iota(jnp.int32, (w, LANES), 1) // d
    ones = jnp.ones((2 * w, LANES), dtype=jnp.bfloat16)

    pairs = []
    for pair in range(SWA_KV_HEADS // 2):
        lanes = slice(pair * LANES, (pair + 1) * LANES)
        kk = k_ref[pl.ds(row0, 2 * w), lanes]
        kk = (kk.astype(jnp.float32) * (d ** -0.5 * LOG2E)).astype(jnp.bfloat16)
        vv = v_ref[pl.ds(row0, 2 * w), lanes]
        pairs.append((kk, _swap_halves(kk), vv, _swap_halves(vv)))

    def operands(n):
        c, t = divmod(n, 2)
        pair, e = divmod(c, 2)
        kk, kk_sw, vv, vv_sw = pairs[pair]
        zero = jnp.zeros_like(kk)
        if t == 0:
            rhs, val = jnp.where(lane_half == e, kk, zero), vv
        else:
            rhs, val = jnp.where(lane_half == e, zero, kk_sw), vv_sw
        heads = [c * SWA_GROUP + 2 * u + (e if t == 0 else 1 - e) for u in range(cols_per_kv)]
        return c, e, rhs, val, heads

    def logits(n):
        c, _, rhs, _, _ = operands(n)
        lhs = jnp.concatenate(
            [q_ref[rows, pl.ds((c * cols_per_kv + u) * LANES, LANES)] for u in range(cols_per_kv)], axis=0)
        return lax.dot_general(lhs, rhs, (((1,), (1,)), ((), ())), preferred_element_type=jnp.float32)

    def softmax(n, s):
        heads = operands(n)[4]
        ex, stats = [], []
        for u, head in enumerate(heads):
            slope = slope_ref[head] * LOG2E
            su = s[u * w:(u + 1) * w] - jnp.concatenate([slope, slope], axis=1) * rel_masked
            sink = sink_ref[head] * LOG2E
            m = jnp.maximum(jnp.max(su, axis=1, keepdims=True), sink)
            ex.append(jnp.exp2(su - jnp.concatenate([m, m], axis=1)).astype(jnp.bfloat16))
            stats.append(jnp.exp2(sink - m))
        return jnp.concatenate(ex, axis=0), stats

    def values(n, ex, stats):
        val = operands(n)[3]
        acc = jnp.dot(ex, jnp.concatenate([val, ones], axis=1), preferred_element_type=jnp.float32)
        return [acc[u * w:(u + 1) * w, :LANES] / (acc[u * w:(u + 1) * w, LANES:] + stats[u])
                for u in range(cols_per_kv)]

    n_batches = 2 * SWA_KV_HEADS
    ahead = 1
    s = {n: logits(n) for n in range(ahead)}
    outs = {}
    for n in range(n_batches):
        if n + ahead < n_batches:
            s[n + ahead] = logits(n + ahead)
        ex, stats = softmax(n, s.pop(n))
        outs[n] = values(n, ex, stats)
        if n % 2 == 1:
            c, e = operands(n)[:2]
            merged = [jnp.where(out_half == e, outs[n - 1][u], outs[n][u]) for u in range(cols_per_kv)]
            cols = pl.ds(c * cols_per_kv * LANES, cols_per_kv * LANES)
            y_ref[rows, cols] = jnp.concatenate(merged, axis=1).astype(y_ref.dtype) * g_ref[rows, cols]


def _proj_swa_kernel(x_ref, g_ref, w_ref, pq_ref, pk_ref, pkp_ref, slope_ref, sink_ref, y_ref, qg_ref,
                     h_ref, q_ref, k_ref, v_ref, gate_ref):
    i = pl.program_id(0)
    w = WINDOW
    tm = x_ref.shape[0]
    kvw = SWA_KV_HEADS * SWA_HEAD_DIM

    @pl.when(i == 0)
    def _no_history():
        k_ref[0:w, :] = jnp.zeros((w, kvw), k_ref.dtype)
        v_ref[0:w, :] = jnp.zeros((w, kvw), v_ref.dtype)

    x = x_ref[...]
    r = lax.rsqrt(jnp.mean(x * x, axis=-1, keepdims=True) + EPS)
    h_ref[...] = (x * r * g_ref[...]).astype(jnp.bfloat16)
    dot = lambda lo, hi: jnp.dot(h_ref[...], w_ref[:, lo:hi], preferred_element_type=jnp.float32)
    q_ref[...] = dot(0, BRANCH).astype(q_ref.dtype)
    gate_ref[...] = _silu(dot(BRANCH, 2 * BRANCH)).astype(gate_ref.dtype)
    rest = dot(2 * BRANCH, 2 * BRANCH + 2 * kvw + 2 * MEM_WIDTH)
    k_ref[w:, :] = rest[:, :kvw].astype(k_ref.dtype)
    v_ref[w:, :] = rest[:, kvw:2 * kvw].astype(v_ref.dtype)
    qg_ref[...] = rest[:, 2 * kvw:].astype(qg_ref.dtype)

    for b in range(tm // w):
        pk_prev = pkp_ref[...] if b == 0 else pk_ref[:, (b - 1) * w:b * w]
        _swa_block(q_ref, k_ref, v_ref, b * w, pq_ref[b * w:(b + 1) * w, :], pk_prev,
                   pk_ref[:, b * w:(b + 1) * w], (i > 0) if b == 0 else True,
                   slope_ref, sink_ref, gate_ref, y_ref)

    k_ref[0:w, :] = k_ref[tm:tm + w, :]
    v_ref[0:w, :] = v_ref[tm:tm + w, :]


def _norm_proj_swa(x, gain, w, pos_col, pos_row, slopes, sinks):
    s, d = x.shape
    c = w.shape[1]
    tm = PROJ_TM
    kvw = SWA_KV_HEADS * SWA_HEAD_DIM
    assert c == 2 * BRANCH + 2 * kvw + 2 * MEM_WIDTH and tm % WINDOW == 0
    blocks = tm // WINDOW
    return pl.pallas_call(
        _proj_swa_kernel,
        grid=(s // tm,),
        in_specs=[
            pl.BlockSpec((tm, d), lambda i: (i, 0)),
            pl.BlockSpec((1, d), lambda i: (0, 0)),
            pl.BlockSpec((d, c), lambda i: (0, 0), pipeline_mode=pl.Buffered(1)),
            pl.BlockSpec((tm, LANES), lambda i: (i, 0)),
            pl.BlockSpec((1, tm), lambda i: (0, i)),
            pl.BlockSpec((1, WINDOW), lambda i: (0, jnp.maximum(i * blocks - 1, 0))),
            pl.BlockSpec((SWA_Q_HEADS, 1, LANES), lambda i: (0, 0, 0)),
            pl.BlockSpec((SWA_Q_HEADS, 1, LANES), lambda i: (0, 0, 0)),
        ],
        out_specs=[
            pl.BlockSpec((tm, BRANCH), lambda i: (i, 0)),
            pl.BlockSpec((tm, 2 * MEM_WIDTH), lambda i: (i, 0)),
        ],
        out_shape=[
            jax.ShapeDtypeStruct((s, BRANCH), jnp.bfloat16),
            jax.ShapeDtypeStruct((s, 2 * MEM_WIDTH), jnp.bfloat16),
        ],
        scratch_shapes=[
            pltpu.VMEM((tm, d), jnp.bfloat16),
            pltpu.VMEM((tm, BRANCH), jnp.bfloat16),
            pltpu.VMEM((tm + WINDOW, kvw), jnp.bfloat16),
            pltpu.VMEM((tm + WINDOW, kvw), jnp.bfloat16),
            pltpu.VMEM((tm, BRANCH), jnp.bfloat16),
        ],
        compiler_params=_params("arbitrary"),
        name="norm_proj_swa",
    )(x, gain, w, pos_col, pos_row, pos_row, slopes, sinks)


def _silu(g):
    return g / (1.0 + jnp.exp(-g))


def _out_kernel(*refs, gated):
    if gated:
        x_ref, ymix_ref, qm_ref, gmem_ref, kbd_ref, vobd_ref, wo_ref, gpost_ref, o_ref = refs
        y_mix = ymix_ref[...]
    else:
        (x_ref, mix_ref, gmix_ref, qm_ref, gmem_ref, kbd_ref, vobd_ref, wo_ref, gpost_ref,
         o_ref) = refs
        y_mix = mix_ref[...] * _silu(gmix_ref[...])

    s = lax.dot_general(qm_ref[...], kbd_ref[0], (((1,), (1,)), ((), ())),
                        preferred_element_type=jnp.float32)
    ps = []
    for h in range(MEM_HEADS):
        sh = s[:, h * MEM_LEN:(h + 1) * MEM_LEN]
        ps.append(jnp.exp(sh - jnp.max(sh, axis=1, keepdims=True)).astype(jnp.bfloat16))
    nd = jnp.dot(jnp.concatenate(ps, axis=1), vobd_ref[0], preferred_element_type=jnp.float32)
    mem_out = nd[:, :MEM_WIDTH] / nd[:, MEM_WIDTH:]

    y_mem = mem_out.astype(jnp.bfloat16) * _silu(gmem_ref[...])
    y = (jnp.dot(y_mix, wo_ref[:BRANCH, :], preferred_element_type=jnp.float32)
         + jnp.dot(y_mem, wo_ref[BRANCH:, :], preferred_element_type=jnp.float32))
    r = lax.rsqrt(jnp.mean(y * y, axis=-1, keepdims=True) + EPS)
    o_ref[...] = x_ref[...] + y * r * gpost_ref[...]


def _out_layer(x, mix, gate_mix, q_mem, gate_mem, kbd, vobd, layer, w_out, g_post):
    s, d = x.shape
    tm = OUT_TM
    gated = gate_mix is None
    col_blk = lambda width, src: pl.BlockSpec((tm, width), lambda i: (i, src[1] // width))
    in_specs = [pl.BlockSpec((tm, d), lambda i: (i, 0)), pl.BlockSpec((tm, BRANCH), lambda i: (i, 0))]
    args = [x, mix]
    if not gated:
        in_specs.append(col_blk(BRANCH, gate_mix))
        args.append(gate_mix[0])
    in_specs += [
        col_blk(MEM_WIDTH, q_mem), col_blk(MEM_WIDTH, gate_mem),
        pl.BlockSpec((1,) + kbd.shape[1:], lambda i: (layer, 0, 0)),
        pl.BlockSpec((1,) + vobd.shape[1:], lambda i: (layer, 0, 0)),
        pl.BlockSpec((GATE_WIDTH, d), lambda i: (0, 0)),
        pl.BlockSpec((1, d), lambda i: (0, 0)),
    ]
    args += [q_mem[0], gate_mem[0], kbd, vobd, w_out, g_post]
    return pl.pallas_call(
        functools.partial(_out_kernel, gated=gated),
        grid=(s // tm,),
        in_specs=in_specs,
        out_specs=pl.BlockSpec((tm, d), lambda i: (i, 0)),
        out_shape=jax.ShapeDtypeStruct((s, d), jnp.float32),
        compiler_params=_params("arbitrary"),
        name="out_gated" if gated else "out_attn",
    )(*args)


def _relayout_w_in(w, kind):
    kvw = SWA_KV_HEADS * SWA_HEAD_DIM
    if kind == 2:
        q, k, v, q_mem, gate = jnp.split(
            w, [BRANCH, BRANCH + kvw, BRANCH + 2 * kvw, BRANCH + 2 * kvw + MEM_WIDTH], axis=1)
        parts = [q, gate[:, :BRANCH], k, v, q_mem, gate[:, BRANCH:]]
        cols = dict(q_mem=0, gate_mem=MEM_WIDTH)
    elif kind == 1:
        a, b, c, q_mem, gate = jnp.split(
            w, [BRANCH, 2 * BRANCH, 3 * BRANCH, 3 * BRANCH + MEM_WIDTH], axis=1)
        parts = [a, b, c, gate[:, :BRANCH], q_mem, gate[:, BRANCH:]]
        cols = dict(gate_mix=3 * BRANCH, q_mem=4 * BRANCH, gate_mem=4 * BRANCH + MEM_WIDTH)
    else:
        a, b, c, q_mem, gate = jnp.split(
            w, [BRANCH, 2 * BRANCH, 3 * BRANCH, 3 * BRANCH + MEM_WIDTH], axis=1)
        parts = []
        for j in range(BRANCH // CONV_BLOCK):
            ch = slice(j * CONV_BLOCK, (j + 1) * CONV_BLOCK)
            parts += [a[:, ch], b[:, ch], c[:, ch], gate[:, ch]]
        parts += [q_mem, gate[:, BRANCH:]]
        cols = dict(q_mem=0, gate_mem=MEM_WIDTH)
    return jnp.concatenate([p.astype(jnp.bfloat16) for p in parts], axis=1), cols


def _alibi_slopes(n_heads):
    return 2.0 ** (-ALIBI_MAX_BIAS * jnp.arange(1, n_heads + 1, dtype=jnp.float32) / n_heads)


def kernel(x, mem, positions, norm_pre_0, norm_post_0, norm_mem_0, w_in_0, w_mem_kv_0, conv_w_0, w_out_0, norm_pre_1, norm_post_1, norm_mem_1, w_in_1, w_mem_kv_1, lambda_q1_1, lambda_k1_1, lambda_q2_1, lambda_k2_1, subln_1, w_out_1, norm_pre_2, norm_post_2, norm_mem_2, w_in_2, w_mem_kv_2, sinks_2, w_out_2, norm_pre_3, norm_post_3, norm_mem_3, w_in_3, w_mem_kv_3, conv_w_3, w_out_3):
    b, s, d = x.shape
    assert b == 1 and s == SEQ and d == D_MODEL
    xs = x.reshape(s, d)
    pos_f = positions.reshape(s).astype(jnp.float32)
    pos_row = pos_f.reshape(1, s)
    pos_col = jnp.broadcast_to(pos_f[:, None], (s, LANES))

    pre = [norm_pre_0, norm_pre_1, norm_pre_2, norm_pre_3]
    post = [norm_post_0, norm_post_1, norm_post_2, norm_post_3]
    w_in = [w_in_0, w_in_1, w_in_2, w_in_3]
    w_out = [w_out_0, w_out_1, w_out_2, w_out_3]
    conv_w = {0: conv_w_0, 3: conv_w_3}

    mem_gain = jnp.stack([norm_mem_0, norm_mem_1, norm_mem_2, norm_mem_3]).reshape(DEPTH, 1, d)
    mem_w = jnp.stack([w_mem_kv_0, w_mem_kv_1, w_mem_kv_2, w_mem_kv_3]).astype(jnp.bfloat16)
    kbd, vobd = _mem_kv(mem.reshape(MEM_LEN, d), mem_gain, mem_w)

    for layer in range(DEPTH):
        kind = layer % 3
        w, cols = _relayout_w_in(w_in[layer], kind)
        gain = pre[layer].reshape(1, d)
        gate_mix = None
        if kind == 0:
            mix, proj = _norm_proj_conv(xs, gain, w, conv_w[layer])
        elif kind == 2:
            slopes = jnp.broadcast_to(_alibi_slopes(SWA_Q_HEADS)[:, None, None], (SWA_Q_HEADS, 1, LANES))
            sinks = jnp.broadcast_to(sinks_2.astype(jnp.float32)[:, None, None], (SWA_Q_HEADS, 1, LANES))
            mix, proj = _norm_proj_swa(xs, gain, w, pos_col, pos_row, slopes, sinks)
        else:
            proj = _norm_proj(xs, gain, w)
            gate_mix = (proj, cols["gate_mix"])
            slopes = jnp.broadcast_to(_alibi_slopes(DIFF_HEADS)[:, None, None], (DIFF_HEADS, 8, LANES))
            lam_init = 0.8 - 0.6 * math.exp(-0.3 * layer)
            mix = _diff_attention(
                proj, pos_col, slopes,
                lambda_q1_1.reshape(1, -1), lambda_k1_1.reshape(1, -1),
                lambda_q2_1.reshape(1, -1), lambda_k2_1.reshape(1, -1),
                subln_1.reshape(1, -1), lam_init)
        xs = _out_layer(xs, mix, gate_mix, (proj, cols["q_mem"]), (proj, cols["gate_mem"]), kbd, vobd,
                        layer, w_out[layer].astype(jnp.bfloat16), post[layer].reshape(1, d))
    return xs.reshape(b, s, d)
```

```python
import functools
import math

import jax
import jax.numpy as jnp
from jax import lax
from jax.experimental import pallas as pl
from jax.experimental.pallas import tpu as pltpu

D_MODEL = 1024
SEQ = 16384
DEPTH = 4
MEM_LEN = 256
BRANCH = 2048
CONV_WIDTH = 3
DIFF_HEAD_DIM = 64
DIFF_HEADS = 16
SWA_HEAD_DIM = 64
SWA_Q_HEADS = 32
SWA_KV_HEADS = 4
SWA_GROUP = SWA_Q_HEADS // SWA_KV_HEADS
WINDOW = 128
MEM_HEADS = 4
MEM_HEAD_DIM = 64
MEM_WIDTH = 256
GATE_WIDTH = BRANCH + MEM_WIDTH
ALIBI_MAX_BIAS = 8.0
EPS = 1e-6
NEG_INF = -1e30

LANES = 128
VMEM_LIMIT_BYTES = 56 * 1024 * 1024

PROJ_TM = 512
CONV_BLOCK = 256
OUT_TM = 512
DIFF_TQ = 512
DIFF_QSUB = 2
DIFF_TK = 256
DIFF_GROUP = 1
DIFF_UNROLLS = (16, 8, 4, 2)
FEAT_RADIX = 128.0
FEAT_DUMMY_LANE = 6
LOG2E = math.log2(math.e)


def _params(*sem):
    return pltpu.CompilerParams(dimension_semantics=sem, vmem_limit_bytes=VMEM_LIMIT_BYTES)


def _mem_kv_kernel(mem_ref, g_ref, w_ref, kbd_ref, vobd_ref):
    m = mem_ref[...]
    r = lax.rsqrt(jnp.mean(m * m, axis=-1, keepdims=True) + EPS)
    mn = (m * r * g_ref[0]).astype(jnp.bfloat16)
    kv = jnp.dot(mn, w_ref[0], preferred_element_type=jnp.float32)
    km = kv[:, :MEM_WIDTH] * (MEM_HEAD_DIM ** -0.5)
    vm = kv[:, MEM_WIDTH:]
    head_of_lane = lax.broadcasted_iota(jnp.int32, (MEM_LEN, MEM_WIDTH), 1) // MEM_HEAD_DIM
    for h in range(MEM_HEADS):
        sel = head_of_lane == h
        rows = pl.ds(h * MEM_LEN, MEM_LEN)
        kbd_ref[0, rows, :] = jnp.where(sel, km, 0.0).astype(jnp.bfloat16)
        vobd_ref[0, rows, :MEM_WIDTH] = jnp.where(sel, vm, 0.0).astype(jnp.bfloat16)
        vobd_ref[0, rows, MEM_WIDTH:] = jnp.where(sel, 1.0, 0.0).astype(jnp.bfloat16)


def _mem_kv(mem, gains, weights):
    n_layers = gains.shape[0]
    rows = MEM_HEADS * MEM_LEN
    return pl.pallas_call(
        _mem_kv_kernel,
        grid=(n_layers,),
        in_specs=[
            pl.BlockSpec((MEM_LEN, D_MODEL), lambda l: (0, 0)),
            pl.BlockSpec((1, 1, D_MODEL), lambda l: (l, 0, 0)),
            pl.BlockSpec((1, D_MODEL, 2 * MEM_WIDTH), lambda l: (l, 0, 0)),
        ],
        out_specs=[
            pl.BlockSpec((1, rows, MEM_WIDTH), lambda l: (l, 0, 0)),
            pl.BlockSpec((1, rows, 2 * MEM_WIDTH), lambda l: (l, 0, 0)),
        ],
        out_shape=[
            jax.ShapeDtypeStruct((n_layers, rows, MEM_WIDTH), jnp.bfloat16),
            jax.ShapeDtypeStruct((n_layers, rows, 2 * MEM_WIDTH), jnp.bfloat16),
        ],
        compiler_params=_params("arbitrary"),
        name="mem_kv",
    )(mem, gains, weights)


def _proj_kernel(x_ref, g_ref, w_ref, o_ref):
    x = x_ref[...]
    r = lax.rsqrt(jnp.mean(x * x, axis=-1, keepdims=True) + EPS)
    h = (x * r * g_ref[...]).astype(jnp.bfloat16)
    o_ref[...] = jnp.dot(h, w_ref[...], preferred_element_type=jnp.float32).astype(o_ref.dtype)


def _norm_proj(x, gain, w):
    s, d = x.shape
    c = w.shape[1]
    n_col = 2
    tn = c // n_col
    assert tn * n_col == c and tn % LANES == 0 and s % PROJ_TM == 0
    return pl.pallas_call(
        _proj_kernel,
        grid=(n_col, s // PROJ_TM),
        in_specs=[
            pl.BlockSpec((PROJ_TM, d), lambda j, i: (i, 0)),
            pl.BlockSpec((1, d), lambda j, i: (0, 0)),
            pl.BlockSpec((d, tn), lambda j, i: (0, j)),
        ],
        out_specs=pl.BlockSpec((PROJ_TM, tn), lambda j, i: (i, j)),
        out_shape=jax.ShapeDtypeStruct((s, c), jnp.bfloat16),
        compiler_params=_params("arbitrary", "arbitrary"),
        name="norm_proj",
    )(x, gain, w)


def _proj_conv_kernel(x_ref, g_ref, w_ref, cw_ref, y_ref, qg_ref, zh_ref, h_ref):
    i = pl.program_id(0)
    x = x_ref[...]
    r = lax.rsqrt(jnp.mean(x * x, axis=-1, keepdims=True) + EPS)
    h_ref[...] = (x * r * g_ref[...]).astype(jnp.bfloat16)
    tm = x.shape[0]
    ch = CONV_BLOCK
    row = lax.broadcasted_iota(jnp.int32, (8, ch), 0)
    for j in range(BRANCH // ch):
        pr = jnp.dot(h_ref[...], w_ref[:, 4 * ch * j:4 * ch * (j + 1)],
                     preferred_element_type=jnp.float32)
        bg, cg, u, gate = (pr[:, n * ch:(n + 1) * ch] for n in range(4))
        z = cg * u
        zh = jnp.where(i > 0, zh_ref[j], 0.0)
        z1 = pltpu.roll(z, 1, 0)
        z2 = pltpu.roll(z, 2, 0)
        z1 = jnp.concatenate([jnp.where(row == 0, zh[7:8], z1[:8]), z1[8:]], axis=0)
        z2 = jnp.concatenate(
            [jnp.where(row == 0, zh[6:7], jnp.where(row == 1, zh[7:8], z2[:8])), z2[8:]], axis=0)
        cw = cw_ref[:, j * ch:(j + 1) * ch]
        mix = bg * (cw[0:1] * z2 + cw[1:2] * z1 + cw[2:3] * z)
        y_ref[:, j * ch:(j + 1) * ch] = (mix * _silu(gate)).astype(y_ref.dtype)
        zh_ref[j] = z[tm - 8:]
    qg_ref[...] = jnp.dot(h_ref[...], w_ref[:, 4 * BRANCH:],
                          preferred_element_type=jnp.float32).astype(qg_ref.dtype)


def _norm_proj_conv(x, gain, w, conv_w):
    s, d = x.shape
    c = w.shape[1]
    assert c == 4 * BRANCH + 2 * MEM_WIDTH and BRANCH % CONV_BLOCK == 0
    return pl.pallas_call(
        _proj_conv_kernel,
        grid=(s // PROJ_TM,),
        in_specs=[
            pl.BlockSpec((PROJ_TM, d), lambda i: (i, 0)),
            pl.BlockSpec((1, d), lambda i: (0, 0)),
            pl.BlockSpec((d, c), lambda i: (0, 0), pipeline_mode=pl.Buffered(1)),
            pl.BlockSpec((CONV_WIDTH, BRANCH), lambda i: (0, 0)),
        ],
        out_specs=[
            pl.BlockSpec((PROJ_TM, BRANCH), lambda i: (i, 0)),
            pl.BlockSpec((PROJ_TM, 2 * MEM_WIDTH), lambda i: (i, 0)),
        ],
        out_shape=[
            jax.ShapeDtypeStruct((s, BRANCH), jnp.bfloat16),
            jax.ShapeDtypeStruct((s, 2 * MEM_WIDTH), jnp.bfloat16),
        ],
        scratch_shapes=[pltpu.VMEM((BRANCH // CONV_BLOCK, 8, CONV_BLOCK), jnp.float32),
                        pltpu.VMEM((PROJ_TM, d), jnp.bfloat16)],
        compiler_params=_params("arbitrary"),
        name="norm_proj_conv",
    )(x, gain, w, conv_w)


def _diff_key_features(posk_ref, kf_ref):
    tk = DIFF_TK
    n_blocks = SEQ // tk
    lane_t = lax.broadcasted_iota(jnp.int32, (tk, LANES), 1)

    def fill(t, carry):
        pos = posk_ref[pl.ds(pl.multiple_of(t * tk, tk), tk), :]
        hi = jnp.floor(pos * (1.0 / FEAT_RADIX))
        lo = pos - hi * FEAT_RADIX
        kf_ref[t] = jnp.where(lane_t < 3, hi, jnp.where(lane_t < 6, lo, 0.0)).astype(jnp.bfloat16)
        return carry

    lax.fori_loop(0, n_blocks, fill, 0)
    kf_ref[n_blocks] = jnp.where(lane_t == FEAT_DUMMY_LANE, 1.0, 0.0).astype(jnp.bfloat16)


def _diff_attn_kernel(*refs, lam_init):
    @pl.when(pl.program_id(1) == 0)
    def _once_per_head_group():
        _diff_key_features(refs[3], refs[-1])

    for sub in range(DIFF_QSUB):
        _diff_query_block(pl.program_id(1) * DIFF_QSUB + sub, sub * DIFF_TQ, *refs, lam_init=lam_init)


def _diff_query_block(i, row0, q_ref, k_ref, v_ref, posk_ref, slope_ref, lq1_ref, lk1_ref, lq2_ref,
                      lk2_ref, subln_ref, o_ref, m_ref, acc_ref, s_ref, p_ref, alpha_ref, kf_ref, *,
                      lam_init):
    tq, tk = DIFF_TQ, DIFF_TK
    group = range(DIFF_GROUP)
    q_rows = pl.ds(row0, tq)
    d = DIFF_HEAD_DIM
    n_blocks = SEQ // tk
    reps = tk // LANES
    n_diag = tq // tk
    n_full = i * n_diag
    head_lanes = [slice(g * LANES, (g + 1) * LANES) for g in group]

    flane =lax.broadcasted_iota(jnp.int32, (1, LANES), 1)
    qqs = []
    for g in group:
        q = q_ref[q_rows, head_lanes[g]].astype(jnp.float32) * (d ** -0.5 * LOG2E)
        lane = lax.broadcasted_iota(jnp.int32, q.shape, 1)
        qq = jnp.concatenate([jnp.where(lane < d, q, 0.0), jnp.where(lane < d, 0.0, q)],
                             axis=0).astype(jnp.bfloat16)
        slope = slope_ref[g, 0:1, :] * LOG2E
        s1 = slope.astype(jnp.bfloat16).astype(jnp.float32)
        s2 = (slope - s1).astype(jnp.bfloat16).astype(jnp.float32)
        s3 = (slope - s1 - s2).astype(jnp.bfloat16).astype(jnp.float32)
        qfeat = jnp.zeros((1, LANES), jnp.float32)
        for f, val in enumerate([s1 * FEAT_RADIX, s2 * FEAT_RADIX, s3 * FEAT_RADIX, s1, s2, s3]):
            qfeat = jnp.where(flane == f, val, qfeat)
        qfeat = jnp.where(flane == FEAT_DUMMY_LANE, NEG_INF, qfeat)
        qqs.append(jnp.concatenate(
            [qq, jnp.broadcast_to(qfeat, (2 * tq, LANES)).astype(jnp.bfloat16)], axis=1))
        m_ref[g] = jnp.full(m_ref.shape[1:], NEG_INF, dtype=jnp.float32)
        acc_ref[g] = jnp.zeros(acc_ref.shape[1:], dtype=jnp.float32)
        if n_diag == 1:
            p_ref[2 * g + 1] = jnp.zeros(p_ref.shape[1:], dtype=p_ref.dtype)
            alpha_ref[2 * g + 1] = jnp.ones(alpha_ref.shape[1:], dtype=jnp.float32)

    ones = jnp.ones((tk, LANES), dtype=jnp.bfloat16)

    def keys(g, t_key, t_feat):
        rows = pl.ds(pl.multiple_of(t_key * tk, tk), tk)
        return jnp.concatenate([k_ref[rows, head_lanes[g]], kf_ref[t_feat]], axis=1)

    def logits(g, t_key, t_feat):
        return lax.dot_general(qqs[g], keys(g, t_key, t_feat), (((1,), (1,)), ((), ())),
                               preferred_element_type=jnp.float32)

    def softmax_step(g, s):
        m_prev = m_ref[g]
        m_new = jnp.maximum(m_prev, jnp.max(s, axis=1, keepdims=True))
        m_ref[g] = m_new
        p = jnp.exp2(s - jnp.concatenate([m_new] * reps, axis=1)).astype(jnp.bfloat16)
        return p, jnp.exp2(m_prev - m_new)

    def accumulate(g, t, p, alpha):
        vb = v_ref[pl.ds(pl.multiple_of(t * tk, tk), tk), head_lanes[g]]
        pv = jnp.dot(p, jnp.concatenate([vb, ones], axis=1), preferred_element_type=jnp.float32)
        acc_ref[g] = acc_ref[g] * jnp.concatenate([alpha, alpha], axis=1) + pv

    row = lax.broadcasted_iota(jnp.int32, (2 * tq, tk), 0)
    row = jnp.where(row >= tq, row - tq, row)
    col = lax.broadcasted_iota(jnp.int32, (2 * tq, tk), 1)

    def stage_a_boundary(b):
        for g in group:
            s = logits(g, n_full + b, n_full + b)
            s_ref[2 * g + b % 2] = jnp.where(col + b * tk <= row, s, NEG_INF)

    def stage_a(b, buf):
        valid = b < n_full + n_diag
        t_key = jnp.clip(b - n_diag, 0, n_blocks - 1)
        t_feat = jnp.where(valid, b - n_diag, n_blocks)
        for g in group:
            s_ref[2 * g + buf] = logits(g, t_key, t_feat)

    def stage_b(buf):
        for g in group:
            p, alpha = softmax_step(g, s_ref[2 * g + buf])
            p_ref[2 * g + buf] = p
            alpha_ref[2 * g + buf] = alpha

    def stage_c(b, buf):
        t = jnp.clip(jnp.where(b < n_diag, n_full + b, b - n_diag), 0, n_blocks - 1)
        for g in group:
            accumulate(g, t, p_ref[2 * g + buf], alpha_ref[2 * g + buf])

    stage_a_boundary(0)
    for b in range(1, n_diag):
        if b >= 2:
            stage_c(b - 2, b % 2)
        stage_b((b - 1) % 2)
        stage_a_boundary(b)

    def run_steps(first, trips, unroll):
        def body(u, carry):
            for r in range(unroll):
                b = first + unroll * u + r + 1
                par = (n_diag + r) % 2
                stage_c(b - 2, par)
                stage_b(1 - par)
                stage_a(b, par)
            return carry
        lax.fori_loop(0, trips, body, 0)

    last_step = n_full + n_diag + 1
    done = n_diag - 1
    for size in DIFF_UNROLLS[:-1]:
        trips = (last_step - done) // size
        run_steps(done, trips, size)
        done = done + trips * size
    run_steps(done, (last_step - done + DIFF_UNROLLS[-1] - 1) // DIFF_UNROLLS[-1], DIFF_UNROLLS[-1])

    lam = (jnp.exp(jnp.sum(lq1_ref[...] * lk1_ref[...], axis=1, keepdims=True))
           - jnp.exp(jnp.sum(lq2_ref[...] * lk2_ref[...], axis=1, keepdims=True)) + lam_init)
    for g in group:
        acc = acc_ref[g]
        o = acc[:, :LANES] / acc[:, LANES:]
        o = o[:tq] - lam * o[tq:]
        r = lax.rsqrt(jnp.mean(o * o, axis=-1, keepdims=True) + EPS)
        o_ref[q_rows, head_lanes[g]] = (o * r * subln_ref[...] * (1.0 - lam_init)).astype(o_ref.dtype)


def _diff_attention(proj, pos_col, slopes, lq1, lk1, lq2, lk2, subln, lam_init):
    s = proj.shape[0]
    tq, tk, n_group = DIFF_TQ, DIFF_TK, DIFF_GROUP
    assert tq % tk == 0 and s % tq == 0 and all(u % 2 == 0 for u in DIFF_UNROLLS)
    assert DIFF_HEADS % n_group == 0
    gw = n_group * LANES
    gb = BRANCH // gw
    vec = lambda n: pl.BlockSpec((1, n), lambda h, i: (0, 0))
    resident = pl.BlockSpec
    return pl.pallas_call(
        functools.partial(_diff_attn_kernel, lam_init=lam_init),
        grid=(DIFF_HEADS // n_group, s // (DIFF_QSUB * tq)),
        in_specs=[
            pl.BlockSpec((DIFF_QSUB * tq, gw), lambda h, i: (i, h)),
            resident((s, gw), lambda h, i: (0, gb + h)),
            resident((s, gw), lambda h, i: (0, 2 * gb + h)),
            resident((s, LANES), lambda h, i: (0, 0)),
            pl.BlockSpec((n_group, 8, LANES), lambda h, i: (h, 0, 0)),
            vec(DIFF_HEAD_DIM), vec(DIFF_HEAD_DIM), vec(DIFF_HEAD_DIM), vec(DIFF_HEAD_DIM),
            vec(2 * DIFF_HEAD_DIM),
        ],
        out_specs=pl.BlockSpec((DIFF_QSUB * tq, gw), lambda h, i: (i, h)),
        out_shape=jax.ShapeDtypeStruct((s, BRANCH), jnp.bfloat16),
        scratch_shapes=[
            pltpu.VMEM((n_group, 2 * tq, LANES), jnp.float32),
            pltpu.VMEM((n_group, 2 * tq, 2 * LANES), jnp.float32),
            pltpu.VMEM((2 * n_group, 2 * tq, tk), jnp.float32),
            pltpu.VMEM((2 * n_group, 2 * tq, tk), jnp.bfloat16),
            pltpu.VMEM((2 * n_group, 2 * tq, LANES), jnp.float32),
            pltpu.VMEM((s // tk + 1, tk, LANES), jnp.bfloat16),
        ],
        compiler_params=_params("arbitrary", "arbitrary"),
        name="diff_attention",
    )(proj, proj, proj, pos_col, slopes, lq1, lk1, lq2, lk2, subln)


def _swap_halves(x):
    half = LANES // 2
    return jnp.concatenate([x[:, half:], x[:, :half]], axis=1)


def _swa_block(q_ref, k_ref, v_ref, row0, pq, pk_prev, pk_cur, prev_valid, slope_ref, sink_ref, g_ref,
               y_ref):
    w = WINDOW
    d = SWA_HEAD_DIM
    rows = pl.ds(row0, w)
    cols_per_kv = SWA_GROUP // 2
    rel = jnp.concatenate([pq - pk_prev, pq - pk_cur], axis=1)
    col = lax.broadcasted_iota(jnp.int32, (w, 2 * w), 1)
    valid = (rel >= 0.0) & (rel < float(w)) & ((col >= w) | prev_valid)
    rel_masked = jnp.where(valid, rel, -NEG_INF)
    lane_half = lax.broadcasted_iota(jnp.int32, (2 * w, LANES), 1) // d
    out_half = lax.broadcasted_iota(jnp.int32, (w, LANES), 1) // d
    ones = jnp.ones((2 * w, LANES), dtype=jnp.bfloat16)

    pairs = []
    for pair in range(SWA_KV_HEADS // 2):
        lanes = slice(pair * LANES, (pair + 1) * LANES)
        kk = k_ref[pl.ds(row0, 2 * w), lanes]
        kk = (kk.astype(jnp.float32) * (d ** -0.5 * LOG2E)).astype(jnp.bfloat16)
        vv = v_ref[pl.ds(row0, 2 * w), lanes]
        pairs.append((kk, _swap_halves(kk), vv, _swap_halves(vv)))

    def operands(n):
        c, t = divmod(n, 2)
        pair, e = divmod(c, 2)
        kk, kk_sw, vv, vv_sw = pairs[pair]
        zero = jnp.zeros_like(kk)
        if t == 0:
            rhs, val = jnp.where(lane_half == e, kk, zero), vv
        else:
            rhs, val = jnp.where(lane_half == e, zero, kk_sw), vv_sw
        heads = [c * SWA_GROUP + 2 * u + (e if t == 0 else 1 - e) for u in range(cols_per_kv)]
        return c, e, rhs, val, heads

    def logits(n):
        c, _, rhs, _, _ = operands(n)
        lhs = jnp.concatenate(
            [q_ref[rows, pl.ds((c * cols_per_kv + u) * LANES, LANES)] for u in range(cols_per_kv)], axis=0)
        return lax.dot_general(lhs, rhs, (((1,), (1,)), ((), ())), preferred_element_type=jnp.float32)

    def softmax(n, s):
        heads = operands(n)[4]
        ex, stats = [], []
        for u, head in enumerate(heads):
            slope = slope_ref[head] * LOG2E
            su = s[u * w:(u + 1) * w] - jnp.concatenate([slope, slope], axis=1) * rel_masked
            sink = sink_ref[head] * LOG2E
            m = jnp.maximum(jnp.max(su, axis=1, keepdims=True), sink)
            ex.append(jnp.exp2(su - jnp.concatenate([m, m], axis=1)).astype(jnp.bfloat16))
            stats.append(jnp.exp2(sink - m))
        return jnp.concatenate(ex, axis=0), stats

    def values(n, ex, stats):
        val = operands(n)[3]
        acc = jnp.dot(ex, jnp.concatenate([val, ones], axis=1), preferred_element_type=jnp.float32)
        return [acc[u * w:(u + 1) * w, :LANES] / (acc[u * w:(u + 1) * w, LANES:] + stats[u])
                for u in range(cols_per_kv)]

    n_batches = 2 * SWA_KV_HEADS
    ahead = 1
    s = {n: logits(n) for n in range(ahead)}
    outs = {}
    for n in range(n_batches):
        if n + ahead < n_batches:
            s[n + ahead] = logits(n + ahead)
        ex, stats = softmax(n, s.pop(n))
        outs[n] = values(n, ex, stats)
        if n % 2 == 1:
            c, e = operands(n)[:2]
            merged = [jnp.where(out_half == e, outs[n - 1][u], outs[n][u]) for u in range(cols_per_kv)]
            cols = pl.ds(c * cols_per_kv * LANES, cols_per_kv * LANES)
            y_ref[rows, cols] = jnp.concatenate(merged, axis=1).astype(y_ref.dtype) * g_ref[rows, cols]


def _proj_swa_kernel(x_ref, g_ref, w_ref, pq_ref, pk_ref, pkp_ref, slope_ref, sink_ref, y_ref, qg_ref,
                     h_ref, q_ref, k_ref, v_ref, gate_ref):
    i = pl.program_id(0)
    w = WINDOW
    tm = x_ref.shape[0]
    kvw = SWA_KV_HEADS * SWA_HEAD_DIM

    @pl.when(i == 0)
    def _no_history():
        k_ref[0:w, :] = jnp.zeros((w, kvw), k_ref.dtype)
        v_ref[0:w, :] = jnp.zeros((w, kvw), v_ref.dtype)

    x = x_ref[...]
    r = lax.rsqrt(jnp.mean(x * x, axis=-1, keepdims=True) + EPS)
    h_ref[...] = (x * r * g_ref[...]).astype(jnp.bfloat16)
    dot = lambda lo, hi: jnp.dot(h_ref[...], w_ref[:, lo:hi], preferred_element_type=jnp.float32)
    q_ref[...] = dot(0, BRANCH).astype(q_ref.dtype)
    gate_ref[...] = _silu(dot(BRANCH, 2 * BRANCH)).astype(gate_ref.dtype)
    rest = dot(2 * BRANCH, 2 * BRANCH + 2 * kvw + 2 * MEM_WIDTH)
    k_ref[w:, :] = rest[:, :kvw].astype(k_ref.dtype)
    v_ref[w:, :] = rest[:, kvw:2 * kvw].astype(v_ref.dtype)
    qg_ref[...] = rest[:, 2 * kvw:].astype(qg_ref.dtype)

    for b in range(tm // w):
        pk_prev = pkp_ref[...] if b == 0 else pk_ref[:, (b - 1) * w:b * w]
        _swa_block(q_ref, k_ref, v_ref, b * w, pq_ref[b * w:(b + 1) * w, :], pk_prev,
                   pk_ref[:, b * w:(b + 1) * w], (i > 0) if b == 0 else True,
                   slope_ref, sink_ref, gate_ref, y_ref)

    k_ref[0:w, :] = k_ref[tm:tm + w, :]
    v_ref[0:w, :] = v_ref[tm:tm + w, :]


def _norm_proj_swa(x, gain, w, pos_col, pos_row, slopes, sinks):
    s, d = x.shape
    c = w.shape[1]
    tm = PROJ_TM
    kvw = SWA_KV_HEADS * SWA_HEAD_DIM
    assert c == 2 * BRANCH + 2 * kvw + 2 * MEM_WIDTH and tm % WINDOW == 0
    blocks = tm // WINDOW
    return pl.pallas_call(
        _proj_swa_kernel,
        grid=(s // tm,),
        in_specs=[
            pl.BlockSpec((tm, d), lambda i: (i, 0)),
            pl.BlockSpec((1, d), lambda i: (0, 0)),
            pl.BlockSpec((d, c), lambda i: (0, 0), pipeline_mode=pl.Buffered(1)),
            pl.BlockSpec((tm, LANES), lambda i: (i, 0)),
            pl.BlockSpec((1, tm), lambda i: (0, i)),
            pl.BlockSpec((1, WINDOW), lambda i: (0, jnp.maximum(i * blocks - 1, 0))),
            pl.BlockSpec((SWA_Q_HEADS, 1, LANES), lambda i: (0, 0, 0)),
            pl.BlockSpec((SWA_Q_HEADS, 1, LANES), lambda i: (0, 0, 0)),
        ],
        out_specs=[
            pl.BlockSpec((tm, BRANCH), lambda i: (i, 0)),
            pl.BlockSpec((tm, 2 * MEM_WIDTH), lambda i: (i, 0)),
        ],
        out_shape=[
            jax.ShapeDtypeStruct((s, BRANCH), jnp.bfloat16),
            jax.ShapeDtypeStruct((s, 2 * MEM_WIDTH), jnp.bfloat16),
        ],
        scratch_shapes=[
            pltpu.VMEM((tm, d), jnp.bfloat16),
            pltpu.VMEM((tm, BRANCH), jnp.bfloat16),
            pltpu.VMEM((tm + WINDOW, kvw), jnp.bfloat16),
            pltpu.VMEM((tm + WINDOW, kvw), jnp.bfloat16),
            pltpu.VMEM((tm, BRANCH), jnp.bfloat16),
        ],
        compiler_params=_params("arbitrary"),
        name="norm_proj_swa",
    )(x, gain, w, pos_col, pos_row, pos_row, slopes, sinks)


def _silu(g):
    return g / (1.0 + jnp.exp(-g))


def _out_kernel(*refs, gated):
    if gated:
        x_ref, ymix_ref, qm_ref, gmem_ref, kbd_ref, vobd_ref, wo_ref, gpost_ref, o_ref = refs
        y_mix = ymix_ref[...]
    else:
        (x_ref, mix_ref, gmix_ref, qm_ref, gmem_ref, kbd_ref, vobd_ref, wo_ref, gpost_ref,
         o_ref) = refs
        y_mix = mix_ref[...] * _silu(gmix_ref[...])

    s = lax.dot_general(qm_ref[...], kbd_ref[0], (((1,), (1,)), ((), ())),
                        preferred_element_type=jnp.float32)
    ps = []
    for h in range(MEM_HEADS):
        sh = s[:, h * MEM_LEN:(h + 1) * MEM_LEN]
        ps.append(jnp.exp(sh - jnp.max(sh, axis=1, keepdims=True)).astype(jnp.bfloat16))
    nd = jnp.dot(jnp.concatenate(ps, axis=1), vobd_ref[0], preferred_element_type=jnp.float32)
    mem_out = nd[:, :MEM_WIDTH] / nd[:, MEM_WIDTH:]

    y_mem = mem_out.astype(jnp.bfloat16) * _silu(gmem_ref[...])
    y = (jnp.dot(y_mix, wo_ref[:BRANCH, :], preferred_element_type=jnp.float32)
         + jnp.dot(y_mem, wo_ref[BRANCH:, :], preferred_element_type=jnp.float32))
    r = lax.rsqrt(jnp.mean(y * y, axis=-1, keepdims=True) + EPS)
    o_ref[...] = x_ref[...] + y * r * gpost_ref[...]


def _out_layer(x, mix, gate_mix, q_mem, gate_mem, kbd, vobd, layer, w_out, g_post):
    s, d = x.shape
    tm = OUT_TM
    gated = gate_mix is None
    col_blk = lambda width, src: pl.BlockSpec((tm, width), lambda i: (i, src[1] // width))
    in_specs = [pl.BlockSpec((tm, d), lambda i: (i, 0)), pl.BlockSpec((tm, BRANCH), lambda i: (i, 0))]
    args = [x, mix]
    if not gated:
        in_specs.append(col_blk(BRANCH, gate_mix))
        args.append(gate_mix[0])
    in_specs += [
        col_blk(MEM_WIDTH, q_mem), col_blk(MEM_WIDTH, gate_mem),
        pl.BlockSpec((1,) + kbd.shape[1:], lambda i: (layer, 0, 0)),
        pl.BlockSpec((1,) + vobd.shape[1:], lambda i: (layer, 0, 0)),
        pl.BlockSpec((GATE_WIDTH, d), lambda i: (0, 0)),
        pl.BlockSpec((1, d), lambda i: (0, 0)),
    ]
    args += [q_mem[0], gate_mem[0], kbd, vobd, w_out, g_post]
    return pl.pallas_call(
        functools.partial(_out_kernel, gated=gated),
        grid=(s // tm,),
        in_specs=in_specs,
        out_specs=pl.BlockSpec((tm, d), lambda i: (i, 0)),
        out_shape=jax.ShapeDtypeStruct((s, d), jnp.float32),
        compiler_params=_params("arbitrary"),
        name="out_gated" if gated else "out_attn",
    )(*args)


def _relayout_w_in(w, kind):
    kvw = SWA_KV_HEADS * SWA_HEAD_DIM
    if kind == 2:
        q, k, v, q_mem, gate = jnp.split(
            w, [BRANCH, BRANCH + kvw, BRANCH + 2 * kvw, BRANCH + 2 * kvw + MEM_WIDTH], axis=1)
        parts = [q, gate[:, :BRANCH], k, v, q_mem, gate[:, BRANCH:]]
        cols = dict(q_mem=0, gate_mem=MEM_WIDTH)
    elif kind == 1:
        a, b, c, q_mem, gate = jnp.split(
            w, [BRANCH, 2 * BRANCH, 3 * BRANCH, 3 * BRANCH + MEM_WIDTH], axis=1)
        parts = [a, b, c, gate[:, :BRANCH], q_mem, gate[:, BRANCH:]]
        cols = dict(gate_mix=3 * BRANCH, q_mem=4 * BRANCH, gate_mem=4 * BRANCH + MEM_WIDTH)
    else:
        a, b, c, q_mem, gate = jnp.split(
            w, [BRANCH, 2 * BRANCH, 3 * BRANCH, 3 * BRANCH + MEM_WIDTH], axis=1)
        parts = []
        for j in range(BRANCH // CONV_BLOCK):
            ch = slice(j * CONV_BLOCK, (j + 1) * CONV_BLOCK)
            parts += [a[:, ch], b[:, ch], c[:, ch], gate[:, ch]]
        parts += [q_mem, gate[:, BRANCH:]]
        cols = dict(q_mem=0, gate_mem=MEM_WIDTH)
    return jnp.concatenate([p.astype(jnp.bfloat16) for p in parts], axis=1), cols


def _alibi_slopes(n_heads):
    return 2.0 ** (-ALIBI_MAX_BIAS * jnp.arange(1, n_heads + 1, dtype=jnp.float32) / n_heads)


def kernel(x, mem, positions, norm_pre_0, norm_post_0, norm_mem_0, w_in_0, w_mem_kv_0, conv_w_0, w_out_0, norm_pre_1, norm_post_1, norm_mem_1, w_in_1, w_mem_kv_1, lambda_q1_1, lambda_k1_1, lambda_q2_1, lambda_k2_1, subln_1, w_out_1, norm_pre_2, norm_post_2, norm_mem_2, w_in_2, w_mem_kv_2, sinks_2, w_out_2, norm_pre_3, norm_post_3, norm_mem_3, w_in_3, w_mem_kv_3, conv_w_3, w_out_3):
    b, s, d = x.shape
    assert b == 1 and s == SEQ and d == D_MODEL
    xs = x.reshape(s, d)
    pos_f = positions.reshape(s).astype(jnp.float32)
    pos_row = pos_f.reshape(1, s)
    pos_col = jnp.broadcast_to(pos_f[:, None], (s, LANES))

    pre = [norm_pre_0, norm_pre_1, norm_pre_2, norm_pre_3]
    post = [norm_post_0, norm_post_1, norm_post_2, norm_post_3]
    w_in = [w_in_0, w_in_1, w_in_2, w_in_3]
    w_out = [w_out_0, w_out_1, w_out_2, w_out_3]
    conv_w = {0: conv_w_0, 3: conv_w_3}

    mem_gain = jnp.stack([norm_mem_0, norm_mem_1, norm_mem_2, norm_mem_3]).reshape(DEPTH, 1, d)
    mem_w = jnp.stack([w_mem_kv_0, w_mem_kv_1, w_mem_kv_2, w_mem_kv_3]).astype(jnp.bfloat16)
    kbd, vobd = _mem_kv(mem.reshape(MEM_LEN, d), mem_gain, mem_w)

    for layer in range(DEPTH):
        kind = layer % 3
        w, cols = _relayout_w_in(w_in[layer], kind)
        gain = pre[layer].reshape(1, d)
        gate_mix = None
        if kind == 0:
            mix, proj = _norm_proj_conv(xs, gain, w, conv_w[layer])
        elif kind == 2:
            slopes = jnp.broadcast_to(_alibi_slopes(SWA_Q_HEADS)[:, None, None], (SWA_Q_HEADS, 1, LANES))
            sinks = jnp.broadcast_to(sinks_2.astype(jnp.float32)[:, None, None], (SWA_Q_HEADS, 1, LANES))
            mix, proj = _norm_proj_swa(xs, gain, w, pos_col, pos_row, slopes, sinks)
        else:
            proj = _norm_proj(xs, gain, w)
            gate_mix = (proj, cols["gate_mix"])
            slopes = jnp.broadcast_to(_alibi_slopes(DIFF_HEADS)[:, None, None], (DIFF_HEADS, 8, LANES))
            lam_init = 0.8 - 0.6 * math.exp(-0.3 * layer)
            mix = _diff_attention(
                proj, pos_col, slopes,
                lambda_q1_1.reshape(1, -1), lambda_k1_1.reshape(1, -1),
                lambda_q2_1.reshape(1, -1), lambda_k2_1.reshape(1, -1),
                subln_1.reshape(1, -1), lam_init)
        xs = _out_layer(xs, mix, gate_mix, (proj, cols["q_mem"]), (proj, cols["gate_mem"]), kbd, vobd,
                        layer, w_out[layer].astype(jnp.bfloat16), post[layer].reshape(1, d))
    return xs.reshape(b, s, d)
```

```python
import functools
import math

import jax
import jax.numpy as jnp
from jax import lax
from jax.experimental import pallas as pl
from jax.experimental.pallas import tpu as pltpu

D_MODEL = 1024
SEQ = 16384
DEPTH = 4
MEM_LEN = 256
BRANCH = 2048
CONV_WIDTH = 3
DIFF_HEAD_DIM = 64
DIFF_HEADS = 16
SWA_HEAD_DIM = 64
SWA_Q_HEADS = 32
SWA_KV_HEADS = 4
SWA_GROUP = SWA_Q_HEADS // SWA_KV_HEADS
WINDOW = 128
MEM_HEADS = 4
MEM_HEAD_DIM = 64
MEM_WIDTH = 256
GATE_WIDTH = BRANCH + MEM_WIDTH
ALIBI_MAX_BIAS = 8.0
EPS = 1e-6
NEG_INF = -1e30

LANES = 128
VMEM_LIMIT_BYTES = 56 * 1024 * 1024

PROJ_TM = 512
CONV_BLOCK = 256
OUT_TM = 1024
DIFF_TQ = 512
DIFF_QSUB = 4
DIFF_TK = 256
DIFF_GROUP = 1
DIFF_UNROLLS = (16, 8, 4, 2)
FEAT_RADIX = 128.0
FEAT_DUMMY_LANE = 6
LOG2E = math.log2(math.e)


def _params(*sem):
    return pltpu.CompilerParams(dimension_semantics=sem, vmem_limit_bytes=VMEM_LIMIT_BYTES)


def _mem_kv_kernel(mem_ref, g_ref, w_ref, kbd_ref, vobd_ref):
    m = mem_ref[...]
    r = lax.rsqrt(jnp.mean(m * m, axis=-1, keepdims=True) + EPS)
    mn = (m * r * g_ref[0]).astype(jnp.bfloat16)
    kv = jnp.dot(mn, w_ref[0], preferred_element_type=jnp.float32)
    km = kv[:, :MEM_WIDTH] * (MEM_HEAD_DIM ** -0.5)
    vm = kv[:, MEM_WIDTH:]
    head_of_lane = lax.broadcasted_iota(jnp.int32, (MEM_LEN, MEM_WIDTH), 1) // MEM_HEAD_DIM
    for h in range(MEM_HEADS):
        sel = head_of_lane == h
        rows = pl.ds(h * MEM_LEN, MEM_LEN)
        kbd_ref[0, rows, :] = jnp.where(sel, km, 0.0).astype(jnp.bfloat16)
        vobd_ref[0, rows, :MEM_WIDTH] = jnp.where(sel, vm, 0.0).astype(jnp.bfloat16)
        vobd_ref[0, rows, MEM_WIDTH:] = jnp.where(sel, 1.0, 0.0).astype(jnp.bfloat16)


def _mem_kv(mem, gains, weights):
    n_layers = gains.shape[0]
    rows = MEM_HEADS * MEM_LEN
    return pl.pallas_call(
        _mem_kv_kernel,
        grid=(n_layers,),
        in_specs=[
            pl.BlockSpec((MEM_LEN, D_MODEL), lambda l: (0, 0)),
            pl.BlockSpec((1, 1, D_MODEL), lambda l: (l, 0, 0)),
            pl.BlockSpec((1, D_MODEL, 2 * MEM_WIDTH), lambda l: (l, 0, 0)),
        ],
        out_specs=[
            pl.BlockSpec((1, rows, MEM_WIDTH), lambda l: (l, 0, 0)),
            pl.BlockSpec((1, rows, 2 * MEM_WIDTH), lambda l: (l, 0, 0)),
        ],
        out_shape=[
            jax.ShapeDtypeStruct((n_layers, rows, MEM_WIDTH), jnp.bfloat16),
            jax.ShapeDtypeStruct((n_layers, rows, 2 * MEM_WIDTH), jnp.bfloat16),
        ],
        compiler_params=_params("arbitrary"),
        name="mem_kv",
    )(mem, gains, weights)


def _proj_kernel(x_ref, g_ref, w_ref, o_ref):
    x = x_ref[...]
    r = lax.rsqrt(jnp.mean(x * x, axis=-1, keepdims=True) + EPS)
    h = (x * r * g_ref[...]).astype(jnp.bfloat16)
    o_ref[...] = jnp.dot(h, w_ref[...], preferred_element_type=jnp.float32).astype(o_ref.dtype)


def _norm_proj(x, gain, w):
    s, d = x.shape
    c = w.shape[1]
    n_col = 2
    tn = c // n_col
    assert tn * n_col == c and tn % LANES == 0 and s % PROJ_TM == 0
    return pl.pallas_call(
        _proj_kernel,
        grid=(n_col, s // PROJ_TM),
        in_specs=[
            pl.BlockSpec((PROJ_TM, d), lambda j, i: (i, 0)),
            pl.BlockSpec((1, d), lambda j, i: (0, 0)),
            pl.BlockSpec((d, tn), lambda j, i: (0, j)),
        ],
        out_specs=pl.BlockSpec((PROJ_TM, tn), lambda j, i: (i, j)),
        out_shape=jax.ShapeDtypeStruct((s, c), jnp.bfloat16),
        compiler_params=_params("arbitrary", "arbitrary"),
        name="norm_proj",
    )(x, gain, w)


def _proj_conv_kernel(x_ref, g_ref, w_ref, cw_ref, y_ref, qg_ref, zh_ref, h_ref):
    i = pl.program_id(0)
    x = x_ref[...]
    r = lax.rsqrt(jnp.mean(x * x, axis=-1, keepdims=True) + EPS)
    h_ref[...] = (x * r * g_ref[...]).astype(jnp.bfloat16)
    tm = x.shape[0]
    ch = CONV_BLOCK
    row = lax.broadcasted_iota(jnp.int32, (8, ch), 0)
    for j in range(BRANCH // ch):
        pr = jnp.dot(h_ref[...], w_ref[:, 4 * ch * j:4 * ch * (j + 1)],
                     preferred_element_type=jnp.float32)
        bg, cg, u, gate = (pr[:, n * ch:(n + 1) * ch] for n in range(4))
        z = cg * u
        zh = jnp.where(i > 0, zh_ref[j], 0.0)
        z1 = pltpu.roll(z, 1, 0)
        z2 = pltpu.roll(z, 2, 0)
        z1 = jnp.concatenate([jnp.where(row == 0, zh[7:8], z1[:8]), z1[8:]], axis=0)
        z2 = jnp.concatenate(
            [jnp.where(row == 0, zh[6:7], jnp.where(row == 1, zh[7:8], z2[:8])), z2[8:]], axis=0)
        cw = cw_ref[:, j * ch:(j + 1) * ch]
        mix = bg * (cw[0:1] * z2 + cw[1:2] * z1 + cw[2:3] * z)
        y_ref[:, j * ch:(j + 1) * ch] = (mix * _silu(gate)).astype(y_ref.dtype)
        zh_ref[j] = z[tm - 8:]
    qg_ref[...] = jnp.dot(h_ref[...], w_ref[:, 4 * BRANCH:],
                          preferred_element_type=jnp.float32).astype(qg_ref.dtype)


def _norm_proj_conv(x, gain, w, conv_w):
    s, d = x.shape
    c = w.shape[1]
    assert c == 4 * BRANCH + 2 * MEM_WIDTH and BRANCH % CONV_BLOCK == 0
    return pl.pallas_call(
        _proj_conv_kernel,
        grid=(s // PROJ_TM,),
        in_specs=[
            pl.BlockSpec((PROJ_TM, d), lambda i: (i, 0)),
            pl.BlockSpec((1, d), lambda i: (0, 0)),
            pl.BlockSpec((d, c), lambda i: (0, 0), pipeline_mode=pl.Buffered(1)),
            pl.BlockSpec((CONV_WIDTH, BRANCH), lambda i: (0, 0)),
        ],
        out_specs=[
            pl.BlockSpec((PROJ_TM, BRANCH), lambda i: (i, 0)),
            pl.BlockSpec((PROJ_TM, 2 * MEM_WIDTH), lambda i: (i, 0)),
        ],
        out_shape=[
            jax.ShapeDtypeStruct((s, BRANCH), jnp.bfloat16),
            jax.ShapeDtypeStruct((s, 2 * MEM_WIDTH), jnp.bfloat16),
        ],
        scratch_shapes=[pltpu.VMEM((BRANCH // CONV_BLOCK, 8, CONV_BLOCK), jnp.float32),
                        pltpu.VMEM((PROJ_TM, d), jnp.bfloat16)],
        compiler_params=_params("arbitrary"),
        name="norm_proj_conv",
    )(x, gain, w, conv_w)


def _diff_key_features(posk_ref, kf_ref):
    tk = DIFF_TK
    n_blocks = SEQ // tk
    lane_t = lax.broadcasted_iota(jnp.int32, (tk, LANES), 1)

    def fill(t, carry):
        pos = posk_ref[pl.ds(pl.multiple_of(t * tk, tk), tk), :]
        hi = jnp.floor(pos * (1.0 / FEAT_RADIX))
        lo = pos - hi * FEAT_RADIX
        kf_ref[t] = jnp.where(lane_t < 3, hi, jnp.where(lane_t < 6, lo, 0.0)).astype(jnp.bfloat16)
        return carry

    lax.fori_loop(0, n_blocks, fill, 0)
    kf_ref[n_blocks] = jnp.where(lane_t == FEAT_DUMMY_LANE, 1.0, 0.0).astype(jnp.bfloat16)


def _diff_attn_kernel(*refs, lam_init):
    @pl.when(pl.program_id(1) == 0)
    def _once_per_head_group():
        _diff_key_features(refs[3], refs[-1])

    for sub in range(DIFF_QSUB):
        _diff_query_block(pl.program_id(1) * DIFF_QSUB + sub, sub * DIFF_TQ, *refs, lam_init=lam_init)


def _diff_query_block(i, row0, q_ref, k_ref, v_ref, posk_ref, slope_ref, lq1_ref, lk1_ref, lq2_ref,
                      lk2_ref, subln_ref, o_ref, m_ref, acc_ref, s_ref, p_ref, alpha_ref, kf_ref, *,
                      lam_init):
    tq, tk = DIFF_TQ, DIFF_TK
    group = range(DIFF_GROUP)
    q_rows = pl.ds(row0, tq)
    d = DIFF_HEAD_DIM
    n_blocks = SEQ // tk
    reps = tk // LANES
    n_diag = tq // tk
    n_full = i * n_diag
    head_lanes = [slice(g * LANES, (g + 1) * LANES) for g in group]

    flane =lax.broadcasted_iota(jnp.int32, (1, LANES), 1)
    qqs = []
    for g in group:
        q = q_ref[q_rows, head_lanes[g]].astype(jnp.float32) * (d ** -0.5 * LOG2E)
        lane = lax.broadcasted_iota(jnp.int32, q.shape, 1)
        qq = jnp.concatenate([jnp.where(lane < d, q, 0.0), jnp.where(lane < d, 0.0, q)],
                             axis=0).astype(jnp.bfloat16)
        slope = slope_ref[g, 0:1, :] * LOG2E
        s1 = slope.astype(jnp.bfloat16).astype(jnp.float32)
        s2 = (slope - s1).astype(jnp.bfloat16).astype(jnp.float32)
        s3 = (slope - s1 - s2).astype(jnp.bfloat16).astype(jnp.float32)
        qfeat = jnp.zeros((1, LANES), jnp.float32)
        for f, val in enumerate([s1 * FEAT_RADIX, s2 * FEAT_RADIX, s3 * FEAT_RADIX, s1, s2, s3]):
            qfeat = jnp.where(flane == f, val, qfeat)
        qfeat = jnp.where(flane == FEAT_DUMMY_LANE, NEG_INF, qfeat)
        qqs.append(jnp.concatenate(
            [qq, jnp.broadcast_to(qfeat, (2 * tq, LANES)).astype(jnp.bfloat16)], axis=1))
        m_ref[g] = jnp.full(m_ref.shape[1:], NEG_INF, dtype=jnp.float32)
        acc_ref[g] = jnp.zeros(acc_ref.shape[1:], dtype=jnp.float32)
        if n_diag == 1:
            p_ref[2 * g + 1] = jnp.zeros(p_ref.shape[1:], dtype=p_ref.dtype)
            alpha_ref[2 * g + 1] = jnp.ones(alpha_ref.shape[1:], dtype=jnp.float32)

    ones = jnp.ones((tk, LANES), dtype=jnp.bfloat16)

    def keys(g, t_key, t_feat):
        rows = pl.ds(pl.multiple_of(t_key * tk, tk), tk)
        return jnp.concatenate([k_ref[rows, head_lanes[g]], kf_ref[t_feat]], axis=1)

    def logits(g, t_key, t_feat):
        return lax.dot_general(qqs[g], keys(g, t_key, t_feat), (((1,), (1,)), ((), ())),
                               preferred_element_type=jnp.float32)

    def softmax_step(g, s):
        m_prev = m_ref[g]
        m_new = jnp.maximum(m_prev, jnp.max(s, axis=1, keepdims=True))
        m_ref[g] = m_new
        p = jnp.exp2(s - jnp.concatenate([m_new] * reps, axis=1)).astype(jnp.bfloat16)
        return p, jnp.exp2(m_prev - m_new)

    def accumulate(g, t, p, alpha):
        vb = v_ref[pl.ds(pl.multiple_of(t * tk, tk), tk), head_lanes[g]]
        pv = jnp.dot(p, jnp.concatenate([vb, ones], axis=1), preferred_element_type=jnp.float32)
        acc_ref[g] = acc_ref[g] * jnp.concatenate([alpha, alpha], axis=1) + pv

    row = lax.broadcasted_iota(jnp.int32, (2 * tq, tk), 0)
    row = jnp.where(row >= tq, row - tq, row)
    col = lax.broadcasted_iota(jnp.int32, (2 * tq, tk), 1)

    def stage_a_boundary(b):
        for g in group:
            s = logits(g, n_full + b, n_full + b)
            s_ref[2 * g + b % 2] = jnp.where(col + b * tk <= row, s, NEG_INF)

    def stage_a(b, buf):
        valid = b < n_full + n_diag
        t_key = jnp.clip(b - n_diag, 0, n_blocks - 1)
        t_feat = jnp.where(valid, b - n_diag, n_blocks)
        for g in group:
            s_ref[2 * g + buf] = logits(g, t_key, t_feat)

    def stage_b(buf):
        for g in group:
            p, alpha = softmax_step(g, s_ref[2 * g + buf])
            p_ref[2 * g + buf] = p
            alpha_ref[2 * g + buf] = alpha

    def stage_c(b, buf):
        t = jnp.clip(jnp.where(b < n_diag, n_full + b, b - n_diag), 0, n_blocks - 1)
        for g in group:
            accumulate(g, t, p_ref[2 * g + buf], alpha_ref[2 * g + buf])

    stage_a_boundary(0)
    for b in range(1, n_diag):
        if b >= 2:
            stage_c(b - 2, b % 2)
        stage_b((b - 1) % 2)
        stage_a_boundary(b)

    def run_steps(first, trips, unroll):
        def body(u, carry):
            for r in range(unroll):
                b = first + unroll * u + r + 1
                par = (n_diag + r) % 2
                stage_c(b - 2, par)
                stage_b(1 - par)
                stage_a(b, par)
            return carry
        lax.fori_loop(0, trips, body, 0)

    last_step = n_full + n_diag + 1
    done = n_diag - 1
    for size in DIFF_UNROLLS[:-1]:
        trips = (last_step - done) // size
        run_steps(done, trips, size)
        done = done + trips * size
    run_steps(done, (last_step - done + DIFF_UNROLLS[-1] - 1) // DIFF_UNROLLS[-1], DIFF_UNROLLS[-1])

    lam = (jnp.exp(jnp.sum(lq1_ref[...] * lk1_ref[...], axis=1, keepdims=True))
           - jnp.exp(jnp.sum(lq2_ref[...] * lk2_ref[...], axis=1, keepdims=True)) + lam_init)
    for g in group:
        acc = acc_ref[g]
        o = acc[:, :LANES] / acc[:, LANES:]
        o = o[:tq] - lam * o[tq:]
        r = lax.rsqrt(jnp.mean(o * o, axis=-1, keepdims=True) + EPS)
        o_ref[q_rows, head_lanes[g]] = (o * r * subln_ref[...] * (1.0 - lam_init)).astype(o_ref.dtype)


def _diff_attention(proj, pos_col, slopes, lq1, lk1, lq2, lk2, subln, lam_init):
    s = proj.shape[0]
    tq, tk, n_group = DIFF_TQ, DIFF_TK, DIFF_GROUP
    assert tq % tk == 0 and s % tq == 0 and all(u % 2 == 0 for u in DIFF_UNROLLS)
    assert DIFF_HEADS % n_group == 0
    gw = n_group * LANES
    gb = BRANCH // gw
    vec = lambda n: pl.BlockSpec((1, n), lambda h, i: (0, 0))
    resident = pl.BlockSpec
    return pl.pallas_call(
        functools.partial(_diff_attn_kernel, lam_init=lam_init),
        grid=(DIFF_HEADS // n_group, s // (DIFF_QSUB * tq)),
        in_specs=[
            pl.BlockSpec((DIFF_QSUB * tq, gw), lambda h, i: (i, h)),
            resident((s, gw), lambda h, i: (0, gb + h)),
            resident((s, gw), lambda h, i: (0, 2 * gb + h)),
            resident((s, LANES), lambda h, i: (0, 0)),
            pl.BlockSpec((n_group, 8, LANES), lambda h, i: (h, 0, 0)),
            vec(DIFF_HEAD_DIM), vec(DIFF_HEAD_DIM), vec(DIFF_HEAD_DIM), vec(DIFF_HEAD_DIM),
            vec(2 * DIFF_HEAD_DIM),
        ],
        out_specs=pl.BlockSpec((DIFF_QSUB * tq, gw), lambda h, i: (i, h)),
        out_shape=jax.ShapeDtypeStruct((s, BRANCH), jnp.bfloat16),
        scratch_shapes=[
            pltpu.VMEM((n_group, 2 * tq, LANES), jnp.float32),
            pltpu.VMEM((n_group, 2 * tq, 2 * LANES), jnp.float32),
            pltpu.VMEM((2 * n_group, 2 * tq, tk), jnp.float32),
            pltpu.VMEM((2 * n_group, 2 * tq, tk), jnp.bfloat16),
            pltpu.VMEM((2 * n_group, 2 * tq, LANES), jnp.float32),
            pltpu.VMEM((s // tk + 1, tk, LANES), jnp.bfloat16),
        ],
        compiler_params=_params("arbitrary", "arbitrary"),
        name="diff_attention",
    )(proj, proj, proj, pos_col, slopes, lq1, lk1, lq2, lk2, subln)


def _swap_halves(x):
    half = LANES // 2
    return jnp.concatenate([x[:, half:], x[:, :half]], axis=1)


def _swa_block(q_ref, k_ref, v_ref, row0, pq, pk_prev, pk_cur, prev_valid, slope_ref, sink_ref, g_ref,
               y_ref):
    w = WINDOW
    d = SWA_HEAD_DIM
    rows = pl.ds(row0, w)
    cols_per_kv = SWA_GROUP // 2
    rel = jnp.concatenate([pq - pk_prev, pq - pk_cur], axis=1)
    col = lax.broadcasted_iota(jnp.int32, (w, 2 * w), 1)
    valid = (rel >= 0.0) & (rel < float(w)) & ((col >= w) | prev_valid)
    rel_masked = jnp.where(valid, rel, -NEG_INF)
    lane_half = lax.broadcasted_iota(jnp.int32, (2 * w, LANES), 1) // d
    out_half = lax.broadcasted_iota(jnp.int32, (w, LANES), 1) // d
    ones = jnp.ones((2 * w, LANES), dtype=jnp.bfloat16)

    pairs = []
    for pair in range(SWA_KV_HEADS // 2):
        lanes = slice(pair * LANES, (pair + 1) * LANES)
        kk = k_ref[pl.ds(row0, 2 * w), lanes]
        kk = (kk.astype(jnp.float32) * (d ** -0.5 * LOG2E)).astype(jnp.bfloat16)
        vv = v_ref[pl.ds(row0, 2 * w), lanes]
        pairs.append((kk, _swap_halves(kk), vv, _swap_halves(vv)))

    def operands(n):
        c, t = divmod(n, 2)
        pair, e = divmod(c, 2)
        kk, kk_sw, vv, vv_sw = pairs[pair]
        zero = jnp.zeros_like(kk)
        if t == 0:
            rhs, val = jnp.where(lane_half == e, kk, zero), vv
        else:
            rhs, val = jnp.where(lane_half == e, zero, kk_sw), vv_sw
        heads = [c * SWA_GROUP + 2 * u + (e if t == 0 else 1 - e) for u in range(cols_per_kv)]
        return c, e, rhs, val, heads

    def logits(n):
        c, _, rhs, _, _ = operands(n)
        lhs = jnp.concatenate(
            [q_ref[rows, pl.ds((c * cols_per_kv + u) * LANES, LANES)] for u in range(cols_per_kv)], axis=0)
        return lax.dot_general(lhs, rhs, (((1,), (1,)), ((), ())), preferred_element_type=jnp.float32)

    def softmax(n, s):
        heads = operands(n)[4]
        ex, stats = [], []
        for u, head in enumerate(heads):
            slope = slope_ref[head] * LOG2E
            su = s[u * w:(u + 1) * w] - jnp.concatenate([slope, slope], axis=1) * rel_masked
            sink = sink_ref[head] * LOG2E
            m = jnp.maximum(jnp.max(su, axis=1, keepdims=True), sink)
            ex.append(jnp.exp2(su - jnp.concatenate([m, m], axis=1)).astype(jnp.bfloat16))
            stats.append(jnp.exp2(sink - m))
        return jnp.concatenate(ex, axis=0), stats

    def values(n, ex, stats):
        val = operands(n)[3]
        acc = jnp.dot(ex, jnp.concatenate([val, ones], axis=1), preferred_element_type=jnp.float32)
        return [acc[u * w:(u + 1) * w, :LANES] / (acc[u * w:(u + 1) * w, LANES:] + stats[u])
                for u in range(cols_per_kv)]

    n_batches = 2 * SWA_KV_HEADS
    ahead = 1
    s = {n: logits(n) for n in range(ahead)}
    outs = {}
    for n in range(n_batches):
        if n + ahead < n_batches:
            s[n + ahead] = logits(n + ahead)
        ex, stats = softmax(n, s.pop(n))
        outs[n] = values(n, ex, stats)
        if n % 2 == 1:
            c, e = operands(n)[:2]
            merged = [jnp.where(out_half == e, outs[n - 1][u], outs[n][u]) for u in range(cols_per_kv)]
            cols = pl.ds(c * cols_per_kv * LANES, cols_per_kv * LANES)
            y_ref[rows, cols] = jnp.concatenate(merged, axis=1).astype(y_ref.dtype) * g_ref[rows, cols]


def _proj_swa_kernel(x_ref, g_ref, w_ref, pq_ref, pk_ref, pkp_ref, slope_ref, sink_ref, y_ref, qg_ref,
                     h_ref, q_ref, k_ref, v_ref, gate_ref):
    i = pl.program_id(0)
    w = WINDOW
    tm = x_ref.shape[0]
    kvw = SWA_KV_HEADS * SWA_HEAD_DIM

    @pl.when(i == 0)
    def _no_history():
        k_ref[0:w, :] = jnp.zeros((w, kvw), k_ref.dtype)
        v_ref[0:w, :] = jnp.zeros((w, kvw), v_ref.dtype)

    x = x_ref[...]
    r = lax.rsqrt(jnp.mean(x * x, axis=-1, keepdims=True) + EPS)
    h_ref[...] = (x * r * g_ref[...]).astype(jnp.bfloat16)
    dot = lambda lo, hi: jnp.dot(h_ref[...], w_ref[:, lo:hi], preferred_element_type=jnp.float32)
    q_ref[...] = dot(0, BRANCH).astype(q_ref.dtype)
    gate_ref[...] = _silu(dot(BRANCH, 2 * BRANCH)).astype(gate_ref.dtype)
    rest = dot(2 * BRANCH, 2 * BRANCH + 2 * kvw + 2 * MEM_WIDTH)
    k_ref[w:, :] = rest[:, :kvw].astype(k_ref.dtype)
    v_ref[w:, :] = rest[:, kvw:2 * kvw].astype(v_ref.dtype)
    qg_ref[...] = rest[:, 2 * kvw:].astype(qg_ref.dtype)

    for b in range(tm // w):
        pk_prev = pkp_ref[...] if b == 0 else pk_ref[:, (b - 1) * w:b * w]
        _swa_block(q_ref, k_ref, v_ref, b * w, pq_ref[b * w:(b + 1) * w, :], pk_prev,
                   pk_ref[:, b * w:(b + 1) * w], (i > 0) if b == 0 else True,
                   slope_ref, sink_ref, gate_ref, y_ref)

    k_ref[0:w, :] = k_ref[tm:tm + w, :]
    v_ref[0:w, :] = v_ref[tm:tm + w, :]


def _norm_proj_swa(x, gain, w, pos_col, pos_row, slopes, sinks):
    s, d = x.shape
    c = w.shape[1]
    tm = PROJ_TM
    kvw = SWA_KV_HEADS * SWA_HEAD_DIM
    assert c == 2 * BRANCH + 2 * kvw + 2 * MEM_WIDTH and tm % WINDOW == 0
    blocks = tm // WINDOW
    return pl.pallas_call(
        _proj_swa_kernel,
        grid=(s // tm,),
        in_specs=[
            pl.BlockSpec((tm, d), lambda i: (i, 0)),
            pl.BlockSpec((1, d), lambda i: (0, 0)),
            pl.BlockSpec((d, c), lambda i: (0, 0), pipeline_mode=pl.Buffered(1)),
            pl.BlockSpec((tm, LANES), lambda i: (i, 0)),
            pl.BlockSpec((1, tm), lambda i: (0, i)),
            pl.BlockSpec((1, WINDOW), lambda i: (0, jnp.maximum(i * blocks - 1, 0))),
            pl.BlockSpec((SWA_Q_HEADS, 1, LANES), lambda i: (0, 0, 0)),
            pl.BlockSpec((SWA_Q_HEADS, 1, LANES), lambda i: (0, 0, 0)),
        ],
        out_specs=[
            pl.BlockSpec((tm, BRANCH), lambda i: (i, 0)),
            pl.BlockSpec((tm, 2 * MEM_WIDTH), lambda i: (i, 0)),
        ],
        out_shape=[
            jax.ShapeDtypeStruct((s, BRANCH), jnp.bfloat16),
            jax.ShapeDtypeStruct((s, 2 * MEM_WIDTH), jnp.bfloat16),
        ],
        scratch_shapes=[
            pltpu.VMEM((tm, d), jnp.bfloat16),
            pltpu.VMEM((tm, BRANCH), jnp.bfloat16),
            pltpu.VMEM((tm + WINDOW, kvw), jnp.bfloat16),
            pltpu.VMEM((tm + WINDOW, kvw), jnp.bfloat16),
            pltpu.VMEM((tm, BRANCH), jnp.bfloat16),
        ],
        compiler_params=_params("arbitrary"),
        name="norm_proj_swa",
    )(x, gain, w, pos_col, pos_row, pos_row, slopes, sinks)


def _silu(g):
    return g / (1.0 + jnp.exp(-g))


def _out_kernel(*refs, gated):
    if gated:
        x_ref, ymix_ref, qm_ref, gmem_ref, kbd_ref, vobd_ref, wo_ref, gpost_ref, o_ref = refs
        y_mix = ymix_ref[...]
    else:
        (x_ref, mix_ref, gmix_ref, qm_ref, gmem_ref, kbd_ref, vobd_ref, wo_ref, gpost_ref,
         o_ref) = refs
        y_mix = mix_ref[...] * _silu(gmix_ref[...])

    s = lax.dot_general(qm_ref[...], kbd_ref[0], (((1,), (1,)), ((), ())),
                        preferred_element_type=jnp.float32)
    ps = []
    for h in range(MEM_HEADS):
        sh = s[:, h * MEM_LEN:(h + 1) * MEM_LEN]
        ps.append(jnp.exp(sh - jnp.max(sh, axis=1, keepdims=True)).astype(jnp.bfloat16))
    nd = jnp.dot(jnp.concatenate(ps, axis=1), vobd_ref[0], preferred_element_type=jnp.float32)
    mem_out = nd[:, :MEM_WIDTH] / nd[:, MEM_WIDTH:]

    y_mem = mem_out.astype(jnp.bfloat16) * _silu(gmem_ref[...])
    y = (jnp.dot(y_mix, wo_ref[:BRANCH, :], preferred_element_type=jnp.float32)
         + jnp.dot(y_mem, wo_ref[BRANCH:, :], preferred_element_type=jnp.float32))
    r = lax.rsqrt(jnp.mean(y * y, axis=-1, keepdims=True) + EPS)
    o_ref[...] = x_ref[...] + y * r * gpost_ref[...]


def _out_layer(x, mix, gate_mix, q_mem, gate_mem, kbd, vobd, layer, w_out, g_post):
    s, d = x.shape
    tm = OUT_TM
    gated = gate_mix is None
    col_blk = lambda width, src: pl.BlockSpec((tm, width), lambda i: (i, src[1] // width))
    in_specs = [pl.BlockSpec((tm, d), lambda i: (i, 0)), pl.BlockSpec((tm, BRANCH), lambda i: (i, 0))]
    args = [x, mix]
    if not gated:
        in_specs.append(col_blk(BRANCH, gate_mix))
        args.append(gate_mix[0])
    in_specs += [
        col_blk(MEM_WIDTH, q_mem), col_blk(MEM_WIDTH, gate_mem),
        pl.BlockSpec((1,) + kbd.shape[1:], lambda i: (layer, 0, 0)),
        pl.BlockSpec((1,) + vobd.shape[1:], lambda i: (layer, 0, 0)),
        pl.BlockSpec((GATE_WIDTH, d), lambda i: (0, 0)),
        pl.BlockSpec((1, d), lambda i: (0, 0)),
    ]
    args += [q_mem[0], gate_mem[0], kbd, vobd, w_out, g_post]
    return pl.pallas_call(
        functools.partial(_out_kernel, gated=gated),
        grid=(s // tm,),
        in_specs=in_specs,
        out_specs=pl.BlockSpec((tm, d), lambda i: (i, 0)),
        out_shape=jax.ShapeDtypeStruct((s, d), jnp.float32),
        compiler_params=_params("arbitrary"),
        name="out_gated" if gated else "out_attn",
    )(*args)


def _relayout_w_in(w, kind):
    kvw = SWA_KV_HEADS * SWA_HEAD_DIM
    if kind == 2:
        q, k, v, q_mem, gate = jnp.split(
            w, [BRANCH, BRANCH + kvw, BRANCH + 2 * kvw, BRANCH + 2 * kvw + MEM_WIDTH], axis=1)
        parts = [q, gate[:, :BRANCH], k, v, q_mem, gate[:, BRANCH:]]
        cols = dict(q_mem=0, gate_mem=MEM_WIDTH)
    elif kind == 1:
        a, b, c, q_mem, gate = jnp.split(
            w, [BRANCH, 2 * BRANCH, 3 * BRANCH, 3 * BRANCH + MEM_WIDTH], axis=1)
        parts = [a, b, c, gate[:, :BRANCH], q_mem, gate[:, BRANCH:]]
        cols = dict(gate_mix=3 * BRANCH, q_mem=4 * BRANCH, gate_mem=4 * BRANCH + MEM_WIDTH)
    else:
        a, b, c, q_mem, gate = jnp.split(
            w, [BRANCH, 2 * BRANCH, 3 * BRANCH, 3 * BRANCH + MEM_WIDTH], axis=1)
        parts = []
        for j in range(BRANCH // CONV_BLOCK):
            ch = slice(j * CONV_BLOCK, (j + 1) * CONV_BLOCK)
            parts += [a[:, ch], b[:, ch], c[:, ch], gate[:, ch]]
        parts += [q_mem, gate[:, BRANCH:]]
        cols = dict(q_mem=0, gate_mem=MEM_WIDTH)
    return jnp.concatenate([p.astype(jnp.bfloat16) for p in parts], axis=1), cols


def _alibi_slopes(n_heads):
    return 2.0 ** (-ALIBI_MAX_BIAS * jnp.arange(1, n_heads + 1, dtype=jnp.float32) / n_heads)


def kernel(x, mem, positions, norm_pre_0, norm_post_0, norm_mem_0, w_in_0, w_mem_kv_0, conv_w_0, w_out_0, norm_pre_1, norm_post_1, norm_mem_1, w_in_1, w_mem_kv_1, lambda_q1_1, lambda_k1_1, lambda_q2_1, lambda_k2_1, subln_1, w_out_1, norm_pre_2, norm_post_2, norm_mem_2, w_in_2, w_mem_kv_2, sinks_2, w_out_2, norm_pre_3, norm_post_3, norm_mem_3, w_in_3, w_mem_kv_3, conv_w_3, w_out_3):
    b, s, d = x.shape
    assert b == 1 and s == SEQ and d == D_MODEL
    xs = x.reshape(s, d)
    pos_f = positions.reshape(s).astype(jnp.float32)
    pos_row = pos_f.reshape(1, s)
    pos_col = jnp.broadcast_to(pos_f[:, None], (s, LANES))

    pre = [norm_pre_0, norm_pre_1, norm_pre_2, norm_pre_3]
    post = [norm_post_0, norm_post_1, norm_post_2, norm_post_3]
    w_in = [w_in_0, w_in_1, w_in_2, w_in_3]
    w_out = [w_out_0, w_out_1, w_out_2, w_out_3]
    conv_w = {0: conv_w_0, 3: conv_w_3}

    mem_gain = jnp.stack([norm_mem_0, norm_mem_1, norm_mem_2, norm_mem_3]).reshape(DEPTH, 1, d)
    mem_w = jnp.stack([w_mem_kv_0, w_mem_kv_1, w_mem_kv_2, w_mem_kv_3]).astype(jnp.bfloat16)
    kbd, vobd = _mem_kv(mem.reshape(MEM_LEN, d), mem_gain, mem_w)

    for layer in range(DEPTH):
        kind = layer % 3
        w, cols = _relayout_w_in(w_in[layer], kind)
        gain = pre[layer].reshape(1, d)
        gate_mix = None
        if kind == 0:
            mix, proj = _norm_proj_conv(xs, gain, w, conv_w[layer])
        elif kind == 2:
            slopes = jnp.broadcast_to(_alibi_slopes(SWA_Q_HEADS)[:, None, None], (SWA_Q_HEADS, 1, LANES))
            sinks = jnp.broadcast_to(sinks_2.astype(jnp.float32)[:, None, None], (SWA_Q_HEADS, 1, LANES))
            mix, proj = _norm_proj_swa(xs, gain, w, pos_col, pos_row, slopes, sinks)
        else:
            proj = _norm_proj(xs, gain, w)
            gate_mix = (proj, cols["gate_mix"])
            slopes = jnp.broadcast_to(_alibi_slopes(DIFF_HEADS)[:, None, None], (DIFF_HEADS, 8, LANES))
            lam_init = 0.8 - 0.6 * math.exp(-0.3 * layer)
            mix = _diff_attention(
                proj, pos_col, slopes,
                lambda_q1_1.reshape(1, -1), lambda_k1_1.reshape(1, -1),
                lambda_q2_1.reshape(1, -1), lambda_k2_1.reshape(1, -1),
                subln_1.reshape(1, -1), lam_init)
        xs = _out_layer(xs, mix, gate_mix, (proj, cols["q_mem"]), (proj, cols["gate_mem"]), kbd, vobd,
                        layer, w_out[layer].astype(jnp.bfloat16), post[layer].reshape(1, d))
    return xs.reshape(b, s, d)
```

```python
import functools
import math

import jax
import jax.numpy as jnp
from jax import lax
from jax.experimental import pallas as pl
from jax.experimental.pallas import tpu as pltpu

D_MODEL = 1024
SEQ = 16384
DEPTH = 4
MEM_LEN = 256
BRANCH = 2048
CONV_WIDTH = 3
DIFF_HEAD_DIM = 64
DIFF_HEADS = 16
SWA_HEAD_DIM = 64
SWA_Q_HEADS = 32
SWA_KV_HEADS = 4
SWA_GROUP = SWA_Q_HEADS // SWA_KV_HEADS
WINDOW = 128
MEM_HEADS = 4
MEM_HEAD_DIM = 64
MEM_WIDTH = 256
GATE_WIDTH = BRANCH + MEM_WIDTH
ALIBI_MAX_BIAS = 8.0
EPS = 1e-6
NEG_INF = -1e30

LANES = 128
VMEM_LIMIT_BYTES = 56 * 1024 * 1024

PROJ_TM = 512
CONV_BLOCK = 256
OUT_TM = 512
DIFF_TQ = 512
DIFF_QSUB = 2
DIFF_TK = 256
DIFF_GROUP = 1
DIFF_UNROLLS = (16, 4, 2)
FEAT_RADIX = 128.0
FEAT_DUMMY_LANE = 6
LOG2E = math.log2(math.e)


def _params(*sem):
    return pltpu.CompilerParams(dimension_semantics=sem, vmem_limit_bytes=VMEM_LIMIT_BYTES)


def _mem_kv_kernel(mem_ref, g_ref, w_ref, kbd_ref, vobd_ref):
    m = mem_ref[...]
    r = lax.rsqrt(jnp.mean(m * m, axis=-1, keepdims=True) + EPS)
    mn = (m * r * g_ref[0]).astype(jnp.bfloat16)
    kv = jnp.dot(mn, w_ref[0], preferred_element_type=jnp.float32)
    km = kv[:, :MEM_WIDTH] * (MEM_HEAD_DIM ** -0.5)
    vm = kv[:, MEM_WIDTH:]
    head_of_lane = lax.broadcasted_iota(jnp.int32, (MEM_LEN, MEM_WIDTH), 1) // MEM_HEAD_DIM
    for h in range(MEM_HEADS):
        sel = head_of_lane == h
        rows = pl.ds(h * MEM_LEN, MEM_LEN)
        kbd_ref[0, rows, :] = jnp.where(sel, km, 0.0).astype(jnp.bfloat16)
        vobd_ref[0, rows, :MEM_WIDTH] = jnp.where(sel, vm, 0.0).astype(jnp.bfloat16)
        vobd_ref[0, rows, MEM_WIDTH:] = jnp.where(sel, 1.0, 0.0).astype(jnp.bfloat16)


def _mem_kv(mem, gains, weights):
    n_layers = gains.shape[0]
    rows = MEM_HEADS * MEM_LEN
    return pl.pallas_call(
        _mem_kv_kernel,
        grid=(n_layers,),
        in_specs=[
            pl.BlockSpec((MEM_LEN, D_MODEL), lambda l: (0, 0)),
            pl.BlockSpec((1, 1, D_MODEL), lambda l: (l, 0, 0)),
            pl.BlockSpec((1, D_MODEL, 2 * MEM_WIDTH), lambda l: (l, 0, 0)),
        ],
        out_specs=[
            pl.BlockSpec((1, rows, MEM_WIDTH), lambda l: (l, 0, 0)),
            pl.BlockSpec((1, rows, 2 * MEM_WIDTH), lambda l: (l, 0, 0)),
        ],
        out_shape=[
            jax.ShapeDtypeStruct((n_layers, rows, MEM_WIDTH), jnp.bfloat16),
            jax.ShapeDtypeStruct((n_layers, rows, 2 * MEM_WIDTH), jnp.bfloat16),
        ],
        compiler_params=_params("arbitrary"),
        name="mem_kv",
    )(mem, gains, weights)


def _proj_kernel(x_ref, g_ref, w_ref, o_ref):
    x = x_ref[...]
    r = lax.rsqrt(jnp.mean(x * x, axis=-1, keepdims=True) + EPS)
    h = (x * r * g_ref[...]).astype(jnp.bfloat16)
    o_ref[...] = jnp.dot(h, w_ref[...], preferred_element_type=jnp.float32).astype(o_ref.dtype)


def _norm_proj(x, gain, w):
    s, d = x.shape
    c = w.shape[1]
    n_col = 2
    tn = c // n_col
    assert tn * n_col == c and tn % LANES == 0 and s % PROJ_TM == 0
    return pl.pallas_call(
        _proj_kernel,
        grid=(n_col, s // PROJ_TM),
        in_specs=[
            pl.BlockSpec((PROJ_TM, d), lambda j, i: (i, 0)),
            pl.BlockSpec((1, d), lambda j, i: (0, 0)),
            pl.BlockSpec((d, tn), lambda j, i: (0, j)),
        ],
        out_specs=pl.BlockSpec((PROJ_TM, tn), lambda j, i: (i, j)),
        out_shape=jax.ShapeDtypeStruct((s, c), jnp.bfloat16),
        compiler_params=_params("arbitrary", "arbitrary"),
        name="norm_proj",
    )(x, gain, w)


def _proj_conv_kernel(x_ref, g_ref, w_ref, cw_ref, y_ref, qg_ref, zh_ref, h_ref):
    i = pl.program_id(0)
    x = x_ref[...]
    r = lax.rsqrt(jnp.mean(x * x, axis=-1, keepdims=True) + EPS)
    h_ref[...] = (x * r * g_ref[...]).astype(jnp.bfloat16)
    tm = x.shape[0]
    ch = CONV_BLOCK
    row = lax.broadcasted_iota(jnp.int32, (8, ch), 0)
    for j in range(BRANCH // ch):
        pr = jnp.dot(h_ref[...], w_ref[:, 4 * ch * j:4 * ch * (j + 1)],
                     preferred_element_type=jnp.float32)
        bg, cg, u, gate = (pr[:, n * ch:(n + 1) * ch] for n in range(4))
        z = cg * u
        zh = jnp.where(i > 0, zh_ref[j], 0.0)
        z1 = pltpu.roll(z, 1, 0)
        z2 = pltpu.roll(z, 2, 0)
        z1 = jnp.concatenate([jnp.where(row == 0, zh[7:8], z1[:8]), z1[8:]], axis=0)
        z2 = jnp.concatenate(
            [jnp.where(row == 0, zh[6:7], jnp.where(row == 1, zh[7:8], z2[:8])), z2[8:]], axis=0)
        cw = cw_ref[:, j * ch:(j + 1) * ch]
        mix = bg * (cw[0:1] * z2 + cw[1:2] * z1 + cw[2:3] * z)
        y_ref[:, j * ch:(j + 1) * ch] = (mix * _silu(gate)).astype(y_ref.dtype)
        zh_ref[j] = z[tm - 8:]
    qg_ref[...] = jnp.dot(h_ref[...], w_ref[:, 4 * BRANCH:],
                          preferred_element_type=jnp.float32).astype(qg_ref.dtype)


def _norm_proj_conv(x, gain, w, conv_w):
    s, d = x.shape
    c = w.shape[1]
    assert c == 4 * BRANCH + 2 * MEM_WIDTH and BRANCH % CONV_BLOCK == 0
    return pl.pallas_call(
        _proj_conv_kernel,
        grid=(s // PROJ_TM,),
        in_specs=[
            pl.BlockSpec((PROJ_TM, d), lambda i: (i, 0)),
            pl.BlockSpec((1, d), lambda i: (0, 0)),
            pl.BlockSpec((d, c), lambda i: (0, 0), pipeline_mode=pl.Buffered(1)),
            pl.BlockSpec((CONV_WIDTH, BRANCH), lambda i: (0, 0)),
        ],
        out_specs=[
            pl.BlockSpec((PROJ_TM, BRANCH), lambda i: (i, 0)),
            pl.BlockSpec((PROJ_TM, 2 * MEM_WIDTH), lambda i: (i, 0)),
        ],
        out_shape=[
            jax.ShapeDtypeStruct((s, BRANCH), jnp.bfloat16),
            jax.ShapeDtypeStruct((s, 2 * MEM_WIDTH), jnp.bfloat16),
        ],
        scratch_shapes=[pltpu.VMEM((BRANCH // CONV_BLOCK, 8, CONV_BLOCK), jnp.float32),
                        pltpu.VMEM((PROJ_TM, d), jnp.bfloat16)],
        compiler_params=_params("arbitrary"),
        name="norm_proj_conv",
    )(x, gain, w, conv_w)


def _diff_key_features(posk_ref, kf_ref):
    tk = DIFF_TK
    n_blocks = SEQ // tk
    lane_t = lax.broadcasted_iota(jnp.int32, (tk, LANES), 1)

    def fill(t, carry):
        pos = posk_ref[pl.ds(pl.multiple_of(t * tk, tk), tk), :]
        hi = jnp.floor(pos * (1.0 / FEAT_RADIX))
        lo = pos - hi * FEAT_RADIX
        kf_ref[t] = jnp.where(lane_t < 3, hi, jnp.where(lane_t < 6, lo, 0.0)).astype(jnp.bfloat16)
        return carry

    lax.fori_loop(0, n_blocks, fill, 0)
    kf_ref[n_blocks] = jnp.where(lane_t == FEAT_DUMMY_LANE, 1.0, 0.0).astype(jnp.bfloat16)


def _diff_attn_kernel(*refs, lam_init):
    @pl.when(pl.program_id(1) == 0)
    def _once_per_head_group():
        _diff_key_features(refs[3], refs[-1])

    for sub in range(DIFF_QSUB):
        _diff_query_block(pl.program_id(1) * DIFF_QSUB + sub, sub * DIFF_TQ, *refs, lam_init=lam_init)


def _diff_query_block(i, row0, q_ref, k_ref, v_ref, posk_ref, slope_ref, lq1_ref, lk1_ref, lq2_ref,
                      lk2_ref, subln_ref, o_ref, m_ref, acc_ref, s_ref, p_ref, alpha_ref, kf_ref, *,
                      lam_init):
    tq, tk = DIFF_TQ, DIFF_TK
    group = range(DIFF_GROUP)
    q_rows = pl.ds(row0, tq)
    d = DIFF_HEAD_DIM
    n_blocks = SEQ // tk
    reps = tk // LANES
    n_diag = tq // tk
    n_full = i * n_diag
    head_lanes = [slice(g * LANES, (g + 1) * LANES) for g in group]

    flane =lax.broadcasted_iota(jnp.int32, (1, LANES), 1)
    qqs = []
    for g in group:
        q = q_ref[q_rows, head_lanes[g]].astype(jnp.float32) * (d ** -0.5 * LOG2E)
        lane = lax.broadcasted_iota(jnp.int32, q.shape, 1)
        qq = jnp.concatenate([jnp.where(lane < d, q, 0.0), jnp.where(lane < d, 0.0, q)],
                             axis=0).astype(jnp.bfloat16)
        slope = slope_ref[g, 0:1, :] * LOG2E
        s1 = slope.astype(jnp.bfloat16).astype(jnp.float32)
        s2 = (slope - s1).astype(jnp.bfloat16).astype(jnp.float32)
        s3 = (slope - s1 - s2).astype(jnp.bfloat16).astype(jnp.float32)
        qfeat = jnp.zeros((1, LANES), jnp.float32)
        for f, val in enumerate([s1 * FEAT_RADIX, s2 * FEAT_RADIX, s3 * FEAT_RADIX, s1, s2, s3]):
            qfeat = jnp.where(flane == f, val, qfeat)
        qfeat = jnp.where(flane == FEAT_DUMMY_LANE, NEG_INF, qfeat)
        qqs.append(jnp.concatenate(
            [qq, jnp.broadcast_to(qfeat, (2 * tq, LANES)).astype(jnp.bfloat16)], axis=1))
        m_ref[g] = jnp.full(m_ref.shape[1:], NEG_INF, dtype=jnp.float32)
        acc_ref[g] = jnp.zeros(acc_ref.shape[1:], dtype=jnp.float32)
        if n_diag == 1:
            p_ref[2 * g + 1] = jnp.zeros(p_ref.shape[1:], dtype=p_ref.dtype)
            alpha_ref[2 * g + 1] = jnp.ones(alpha_ref.shape[1:], dtype=jnp.float32)

    ones = jnp.ones((tk, LANES), dtype=jnp.bfloat16)

    def keys(g, t_key, t_feat):
        rows = pl.ds(pl.multiple_of(t_key * tk, tk), tk)
        return jnp.concatenate([k_ref[rows, head_lanes[g]], kf_ref[t_feat]], axis=1)

    def logits(g, t_key, t_feat):
        return lax.dot_general(qqs[g], keys(g, t_key, t_feat), (((1,), (1,)), ((), ())),
                               preferred_element_type=jnp.float32)

    def softmax_step(g, s):
        m_prev = m_ref[g]
        m_new = jnp.maximum(m_prev, jnp.max(s, axis=1, keepdims=True))
        m_ref[g] = m_new
        p = jnp.exp2(s - jnp.concatenate([m_new] * reps, axis=1)).astype(jnp.bfloat16)
        return p, jnp.exp2(m_prev - m_new)

    def accumulate(g, t, p, alpha):
        vb = v_ref[pl.ds(pl.multiple_of(t * tk, tk), tk), head_lanes[g]]
        pv = jnp.dot(p, jnp.concatenate([vb, ones], axis=1), preferred_element_type=jnp.float32)
        acc_ref[g] = acc_ref[g] * jnp.concatenate([alpha, alpha], axis=1) + pv

    row = lax.broadcasted_iota(jnp.int32, (2 * tq, tk), 0)
    row = jnp.where(row >= tq, row - tq, row)
    col = lax.broadcasted_iota(jnp.int32, (2 * tq, tk), 1)

    def stage_a_boundary(b):
        for g in group:
            s = logits(g, n_full + b, n_full + b)
            s_ref[2 * g + b % 2] = jnp.where(col + b * tk <= row, s, NEG_INF)

    def stage_a(b, buf):
        valid = b < n_full + n_diag
        t_key = jnp.clip(b - n_diag, 0, n_blocks - 1)
        t_feat = jnp.where(valid, b - n_diag, n_blocks)
        for g in group:
            s_ref[2 * g + buf] = logits(g, t_key, t_feat)

    def stage_b(buf):
        for g in group:
            p, alpha = softmax_step(g, s_ref[2 * g + buf])
            p_ref[2 * g + buf] = p
            alpha_ref[2 * g + buf] = alpha

    def stage_c(b, buf):
        t = jnp.clip(jnp.where(b < n_diag, n_full + b, b - n_diag), 0, n_blocks - 1)
        for g in group:
            accumulate(g, t, p_ref[2 * g + buf], alpha_ref[2 * g + buf])

    stage_a_boundary(0)
    for b in range(1, n_diag):
        if b >= 2:
            stage_c(b - 2, b % 2)
        stage_b((b - 1) % 2)
        stage_a_boundary(b)

    def run_steps(first, trips, unroll):
        def body(u, carry):
            for r in range(unroll):
                b = first + unroll * u + r + 1
                par = (n_diag + r) % 2
                stage_c(b - 2, par)
                stage_b(1 - par)
                stage_a(b, par)
            return carry
        lax.fori_loop(0, trips, body, 0)

    last_step = n_full + n_diag + 1
    done = n_diag - 1
    for size in DIFF_UNROLLS[:-1]:
        trips = (last_step - done) // size
        run_steps(done, trips, size)
        done = done + trips * size
    run_steps(done, (last_step - done + DIFF_UNROLLS[-1] - 1) // DIFF_UNROLLS[-1], DIFF_UNROLLS[-1])

    lam = (jnp.exp(jnp.sum(lq1_ref[...] * lk1_ref[...], axis=1, keepdims=True))
           - jnp.exp(jnp.sum(lq2_ref[...] * lk2_ref[...], axis=1, keepdims=True)) + lam_init)
    for g in group:
        acc = acc_ref[g]
        o = acc[:, :LANES] / acc[:, LANES:]
        o = o[:tq] - lam * o[tq:]
        r = lax.rsqrt(jnp.mean(o * o, axis=-1, keepdims=True) + EPS)
        o_ref[q_rows, head_lanes[g]] = (o * r * subln_ref[...] * (1.0 - lam_init)).astype(o_ref.dtype)


def _diff_attention(proj, pos_col, slopes, lq1, lk1, lq2, lk2, subln, lam_init):
    s = proj.shape[0]
    tq, tk, n_group = DIFF_TQ, DIFF_TK, DIFF_GROUP
    assert tq % tk == 0 and s % tq == 0 and all(u % 2 == 0 for u in DIFF_UNROLLS)
    assert DIFF_HEADS % n_group == 0
    gw = n_group * LANES
    gb = BRANCH // gw
    vec = lambda n: pl.BlockSpec((1, n), lambda h, i: (0, 0))
    resident = pl.BlockSpec
    return pl.pallas_call(
        functools.partial(_diff_attn_kernel, lam_init=lam_init),
        grid=(DIFF_HEADS // n_group, s // (DIFF_QSUB * tq)),
        in_specs=[
            pl.BlockSpec((DIFF_QSUB * tq, gw), lambda h, i: (i, h)),
            resident((s, gw), lambda h, i: (0, gb + h)),
            resident((s, gw), lambda h, i: (0, 2 * gb + h)),
            resident((s, LANES), lambda h, i: (0, 0)),
            pl.BlockSpec((n_group, 8, LANES), lambda h, i: (h, 0, 0)),
            vec(DIFF_HEAD_DIM), vec(DIFF_HEAD_DIM), vec(DIFF_HEAD_DIM), vec(DIFF_HEAD_DIM),
            vec(2 * DIFF_HEAD_DIM),
        ],
        out_specs=pl.BlockSpec((DIFF_QSUB * tq, gw), lambda h, i: (i, h)),
        out_shape=jax.ShapeDtypeStruct((s, BRANCH), jnp.bfloat16),
        scratch_shapes=[
            pltpu.VMEM((n_group, 2 * tq, LANES), jnp.float32),
            pltpu.VMEM((n_group, 2 * tq, 2 * LANES), jnp.float32),
            pltpu.VMEM((2 * n_group, 2 * tq, tk), jnp.float32),
            pltpu.VMEM((2 * n_group, 2 * tq, tk), jnp.bfloat16),
            pltpu.VMEM((2 * n_group, 2 * tq, LANES), jnp.float32),
            pltpu.VMEM((s // tk + 1, tk, LANES), jnp.bfloat16),
        ],
        compiler_params=_params("arbitrary", "arbitrary"),
        name="diff_attention",
    )(proj, proj, proj, pos_col, slopes, lq1, lk1, lq2, lk2, subln)


def _swap_halves(x):
    half = LANES // 2
    return jnp.concatenate([x[:, half:], x[:, :half]], axis=1)


def _swa_block(q_ref, k_ref, v_ref, row0, pq, pk_prev, pk_cur, prev_valid, slope_ref, sink_ref, g_ref,
               y_ref):
    w = WINDOW
    d = SWA_HEAD_DIM
    rows = pl.ds(row0, w)
    cols_per_kv = SWA_GROUP // 2
    rel = jnp.concatenate([pq - pk_prev, pq - pk_cur], axis=1)
    col = lax.broadcasted_iota(jnp.int32, (w, 2 * w), 1)
    valid = (rel >= 0.0) & (rel < float(w)) & ((col >= w) | prev_valid)
    rel_masked = jnp.where(valid, rel, -NEG_INF)
    lane_half = lax.broadcasted_iota(jnp.int32, (2 * w, LANES), 1) // d
    out_half = lax.broadcasted_iota(jnp.int32, (w, LANES), 1) // d
    ones = jnp.ones((2 * w, LANES), dtype=jnp.bfloat16)

    pairs = []
    for pair in range(SWA_KV_HEADS // 2):
        lanes = slice(pair * LANES, (pair + 1) * LANES)
        kk = k_ref[pl.ds(row0, 2 * w), lanes]
        kk = (kk.astype(jnp.float32) * (d ** -0.5 * LOG2E)).astype(jnp.bfloat16)
        vv = v_ref[pl.ds(row0, 2 * w), lanes]
        pairs.append((kk, _swap_halves(kk), vv, _swap_halves(vv)))

    def operands(n):
        c, t = divmod(n, 2)
        pair, e = divmod(c, 2)
        kk, kk_sw, vv, vv_sw = pairs[pair]
        zero = jnp.zeros_like(kk)
        if t == 0:
            rhs, val = jnp.where(lane_half == e, kk, zero), vv
        else:
            rhs, val = jnp.where(lane_half == e, zero, kk_sw), vv_sw
        heads = [c * SWA_GROUP + 2 * u + (e if t == 0 else 1 - e) for u in range(cols_per_kv)]
        return c, e, rhs, val, heads

    def logits(n):
        c, _, rhs, _, _ = operands(n)
        lhs = jnp.concatenate(
            [q_ref[rows, pl.ds((c * cols_per_kv + u) * LANES, LANES)] for u in range(cols_per_kv)], axis=0)
        return lax.dot_general(lhs, rhs, (((1,), (1,)), ((), ())), preferred_element_type=jnp.float32)

    def softmax(n, s):
        heads = operands(n)[4]
        ex, stats = [], []
        for u, head in enumerate(heads):
            slope = slope_ref[head] * LOG2E
            su = s[u * w:(u + 1) * w] - jnp.concatenate([slope, slope], axis=1) * rel_masked
            sink = sink_ref[head] * LOG2E
            m = jnp.maximum(jnp.max(su, axis=1, keepdims=True), sink)
            ex.append(jnp.exp2(su - jnp.concatenate([m, m], axis=1)).astype(jnp.bfloat16))
            stats.append(jnp.exp2(sink - m))
        return jnp.concatenate(ex, axis=0), stats

    def values(n, ex, stats):
        val = operands(n)[3]
        acc = jnp.dot(ex, jnp.concatenate([val, ones], axis=1), preferred_element_type=jnp.float32)
        return [acc[u * w:(u + 1) * w, :LANES] / (acc[u * w:(u + 1) * w, LANES:] + stats[u])
                for u in range(cols_per_kv)]

    n_batches = 2 * SWA_KV_HEADS
    ahead = 1
    s = {n: logits(n) for n in range(ahead)}
    outs = {}
    for n in range(n_batches):
        if n + ahead < n_batches:
            s[n + ahead] = logits(n + ahead)
        ex, stats = softmax(n, s.pop(n))
        outs[n] = values(n, ex, stats)
        if n % 2 == 1:
            c, e = operands(n)[:2]
            merged = [jnp.where(out_half == e, outs[n - 1][u], outs[n][u]) for u in range(cols_per_kv)]
            cols = pl.ds(c * cols_per_kv * LANES, cols_per_kv * LANES)
            y_ref[rows, cols] = jnp.concatenate(merged, axis=1).astype(y_ref.dtype) * g_ref[rows, cols]


def _proj_swa_kernel(x_ref, g_ref, w_ref, pq_ref, pk_ref, pkp_ref, slope_ref, sink_ref, y_ref, qg_ref,
                     h_ref, q_ref, k_ref, v_ref, gate_ref):
    i = pl.program_id(0)
    w = WINDOW
    tm = x_ref.shape[0]
    kvw = SWA_KV_HEADS * SWA_HEAD_DIM

    @pl.when(i == 0)
    def _no_history():
        k_ref[0:w, :] = jnp.zeros((w, kvw), k_ref.dtype)
        v_ref[0:w, :] = jnp.zeros((w, kvw), v_ref.dtype)

    x = x_ref[...]
    r = lax.rsqrt(jnp.mean(x * x, axis=-1, keepdims=True) + EPS)
    h_ref[...] = (x * r * g_ref[...]).astype(jnp.bfloat16)
    dot = lambda lo, hi: jnp.dot(h_ref[...], w_ref[:, lo:hi], preferred_element_type=jnp.float32)
    q_ref[...] = dot(0, BRANCH).astype(q_ref.dtype)
    gate_ref[...] = _silu(dot(BRANCH, 2 * BRANCH)).astype(gate_ref.dtype)
    rest = dot(2 * BRANCH, 2 * BRANCH + 2 * kvw + 2 * MEM_WIDTH)
    k_ref[w:, :] = rest[:, :kvw].astype(k_ref.dtype)
    v_ref[w:, :] = rest[:, kvw:2 * kvw].astype(v_ref.dtype)
    qg_ref[...] = rest[:, 2 * kvw:].astype(qg_ref.dtype)

    for b in range(tm // w):
        pk_prev = pkp_ref[...] if b == 0 else pk_ref[:, (b - 1) * w:b * w]
        _swa_block(q_ref, k_ref, v_ref, b * w, pq_ref[b * w:(b + 1) * w, :], pk_prev,
                   pk_ref[:, b * w:(b + 1) * w], (i > 0) if b == 0 else True,
                   slope_ref, sink_ref, gate_ref, y_ref)

    k_ref[0:w, :] = k_ref[tm:tm + w, :]
    v_ref[0:w, :] = v_ref[tm:tm + w, :]


def _norm_proj_swa(x, gain, w, pos_col, pos_row, slopes, sinks):
    s, d = x.shape
    c = w.shape[1]
    tm = PROJ_TM
    kvw = SWA_KV_HEADS * SWA_HEAD_DIM
    assert c == 2 * BRANCH + 2 * kvw + 2 * MEM_WIDTH and tm % WINDOW == 0
    blocks = tm // WINDOW
    return pl.pallas_call(
        _proj_swa_kernel,
        grid=(s // tm,),
        in_specs=[
            pl.BlockSpec((tm, d), lambda i: (i, 0)),
            pl.BlockSpec((1, d), lambda i: (0, 0)),
            pl.BlockSpec((d, c), lambda i: (0, 0), pipeline_mode=pl.Buffered(1)),
            pl.BlockSpec((tm, LANES), lambda i: (i, 0)),
            pl.BlockSpec((1, tm), lambda i: (0, i)),
            pl.BlockSpec((1, WINDOW), lambda i: (0, jnp.maximum(i * blocks - 1, 0))),
            pl.BlockSpec((SWA_Q_HEADS, 1, LANES), lambda i: (0, 0, 0)),
            pl.BlockSpec((SWA_Q_HEADS, 1, LANES), lambda i: (0, 0, 0)),
        ],
        out_specs=[
            pl.BlockSpec((tm, BRANCH), lambda i: (i, 0)),
            pl.BlockSpec((tm, 2 * MEM_WIDTH), lambda i: (i, 0)),
        ],
        out_shape=[
            jax.ShapeDtypeStruct((s, BRANCH), jnp.bfloat16),
            jax.ShapeDtypeStruct((s, 2 * MEM_WIDTH), jnp.bfloat16),
        ],
        scratch_shapes=[
            pltpu.VMEM((tm, d), jnp.bfloat16),
            pltpu.VMEM((tm, BRANCH), jnp.bfloat16),
            pltpu.VMEM((tm + WINDOW, kvw), jnp.bfloat16),
            pltpu.VMEM((tm + WINDOW, kvw), jnp.bfloat16),
            pltpu.VMEM((tm, BRANCH), jnp.bfloat16),
        ],
        compiler_params=_params("arbitrary"),
        name="norm_proj_swa",
    )(x, gain, w, pos_col, pos_row, pos_row, slopes, sinks)


def _silu(g):
    return g / (1.0 + jnp.exp(-g))


def _out_kernel(*refs, gated):
    if gated:
        x_ref, ymix_ref, qm_ref, gmem_ref, kbd_ref, vobd_ref, wo_ref, gpost_ref, o_ref = refs
        y_mix = ymix_ref[...]
    else:
        (x_ref, mix_ref, gmix_ref, qm_ref, gmem_ref, kbd_ref, vobd_ref, wo_ref, gpost_ref,
         o_ref) = refs
        y_mix = mix_ref[...] * _silu(gmix_ref[...])

    s = lax.dot_general(qm_ref[...], kbd_ref[0], (((1,), (1,)), ((), ())),
                        preferred_element_type=jnp.float32)
    ps = []
    for h in range(MEM_HEADS):
        sh = s[:, h * MEM_LEN:(h + 1) * MEM_LEN]
        ps.append(jnp.exp(sh - jnp.max(sh, axis=1, keepdims=True)).astype(jnp.bfloat16))
    nd = jnp.dot(jnp.concatenate(ps, axis=1), vobd_ref[0], preferred_element_type=jnp.float32)
    mem_out = nd[:, :MEM_WIDTH] / nd[:, MEM_WIDTH:]

    y_mem = mem_out.astype(jnp.bfloat16) * _silu(gmem_ref[...])
    y = (jnp.dot(y_mix, wo_ref[:BRANCH, :], preferred_element_type=jnp.float32)
         + jnp.dot(y_mem, wo_ref[BRANCH:, :], preferred_element_type=jnp.float32))
    r = lax.rsqrt(jnp.mean(y * y, axis=-1, keepdims=True) + EPS)
    o_ref[...] = x_ref[...] + y * r * gpost_ref[...]


def _out_layer(x, mix, gate_mix, q_mem, gate_mem, kbd, vobd, layer, w_out, g_post):
    s, d = x.shape
    tm = OUT_TM
    gated = gate_mix is None
    col_blk = lambda width, src: pl.BlockSpec((tm, width), lambda i: (i, src[1] // width))
    in_specs = [pl.BlockSpec((tm, d), lambda i: (i, 0)), pl.BlockSpec((tm, BRANCH), lambda i: (i, 0))]
    args = [x, mix]
    if not gated:
        in_specs.append(col_blk(BRANCH, gate_mix))
        args.append(gate_mix[0])
    in_specs += [
        col_blk(MEM_WIDTH, q_mem), col_blk(MEM_WIDTH, gate_mem),
        pl.BlockSpec((1,) + kbd.shape[1:], lambda i: (layer, 0, 0)),
        pl.BlockSpec((1,) + vobd.shape[1:], lambda i: (layer, 0, 0)),
        pl.BlockSpec((GATE_WIDTH, d), lambda i: (0, 0)),
        pl.BlockSpec((1, d), lambda i: (0, 0)),
    ]
    args += [q_mem[0], gate_mem[0], kbd, vobd, w_out, g_post]
    return pl.pallas_call(
        functools.partial(_out_kernel, gated=gated),
        grid=(s // tm,),
        in_specs=in_specs,
        out_specs=pl.BlockSpec((tm, d), lambda i: (i, 0)),
        out_shape=jax.ShapeDtypeStruct((s, d), jnp.float32),
        compiler_params=_params("arbitrary"),
        name="out_gated" if gated else "out_attn",
    )(*args)


def _relayout_w_in(w, kind):
    kvw = SWA_KV_HEADS * SWA_HEAD_DIM
    if kind == 2:
        q, k, v, q_mem, gate = jnp.split(
            w, [BRANCH, BRANCH + kvw, BRANCH + 2 * kvw, BRANCH + 2 * kvw + MEM_WIDTH], axis=1)
        parts = [q, gate[:, :BRANCH], k, v, q_mem, gate[:, BRANCH:]]
        cols = dict(q_mem=0, gate_mem=MEM_WIDTH)
    elif kind == 1:
        a, b, c, q_mem, gate = jnp.split(
            w, [BRANCH, 2 * BRANCH, 3 * BRANCH, 3 * BRANCH + MEM_WIDTH], axis=1)
        parts = [a, b, c, gate[:, :BRANCH], q_mem, gate[:, BRANCH:]]
        cols = dict(gate_mix=3 * BRANCH, q_mem=4 * BRANCH, gate_mem=4 * BRANCH + MEM_WIDTH)
    else:
        a, b, c, q_mem, gate = jnp.split(
            w, [BRANCH, 2 * BRANCH, 3 * BRANCH, 3 * BRANCH + MEM_WIDTH], axis=1)
        parts = []
        for j in range(BRANCH // CONV_BLOCK):
            ch = slice(j * CONV_BLOCK, (j + 1) * CONV_BLOCK)
            parts += [a[:, ch], b[:, ch], c[:, ch], gate[:, ch]]
        parts += [q_mem, gate[:, BRANCH:]]
        cols = dict(q_mem=0, gate_mem=MEM_WIDTH)
    return jnp.concatenate([p.astype(jnp.bfloat16) for p in parts], axis=1), cols


def _alibi_slopes(n_heads):
    return 2.0 ** (-ALIBI_MAX_BIAS * jnp.arange(1, n_heads + 1, dtype=jnp.float32) / n_heads)


def kernel(x, mem, positions, norm_pre_0, norm_post_0, norm_mem_0, w_in_0, w_mem_kv_0, conv_w_0, w_out_0, norm_pre_1, norm_post_1, norm_mem_1, w_in_1, w_mem_kv_1, lambda_q1_1, lambda_k1_1, lambda_q2_1, lambda_k2_1, subln_1, w_out_1, norm_pre_2, norm_post_2, norm_mem_2, w_in_2, w_mem_kv_2, sinks_2, w_out_2, norm_pre_3, norm_post_3, norm_mem_3, w_in_3, w_mem_kv_3, conv_w_3, w_out_3):
    b, s, d = x.shape
    assert b == 1 and s == SEQ and d == D_MODEL
    xs = x.reshape(s, d)
    pos_f = positions.reshape(s).astype(jnp.float32)
    pos_row = pos_f.reshape(1, s)
    pos_col = jnp.broadcast_to(pos_f[:, None], (s, LANES))

    pre = [norm_pre_0, norm_pre_1, norm_pre_2, norm_pre_3]
    post = [norm_post_0, norm_post_1, norm_post_2, norm_post_3]
    w_in = [w_in_0, w_in_1, w_in_2, w_in_3]
    w_out = [w_out_0, w_out_1, w_out_2, w_out_3]
    conv_w = {0: conv_w_0, 3: conv_w_3}

    mem_gain = jnp.stack([norm_mem_0, norm_mem_1, norm_mem_2, norm_mem_3]).reshape(DEPTH, 1, d)
    mem_w = jnp.stack([w_mem_kv_0, w_mem_kv_1, w_mem_kv_2, w_mem_kv_3]).astype(jnp.bfloat16)
    kbd, vobd = _mem_kv(mem.reshape(MEM_LEN, d), mem_gain, mem_w)

    for layer in range(DEPTH):
        kind = layer % 3
        w, cols = _relayout_w_in(w_in[layer], kind)
        gain = pre[layer].reshape(1, d)
        gate_mix = None
        if kind == 0:
            mix, proj = _norm_proj_conv(xs, gain, w, conv_w[layer])
        elif kind == 2:
            slopes = jnp.broadcast_to(_alibi_slopes(SWA_Q_HEADS)[:, None, None], (SWA_Q_HEADS, 1, LANES))
            sinks = jnp.broadcast_to(sinks_2.astype(jnp.float32)[:, None, None], (SWA_Q_HEADS, 1, LANES))
            mix, proj = _norm_proj_swa(xs, gain, w, pos_col, pos_row, slopes, sinks)
        else:
            proj = _norm_proj(xs, gain, w)
            gate_mix = (proj, cols["gate_mix"])
            slopes = jnp.broadcast_to(_alibi_slopes(DIFF_HEADS)[:, None, None], (DIFF_HEADS, 8, LANES))
            lam_init = 0.8 - 0.6 * math.exp(-0.3 * layer)
            mix = _diff_attention(
                proj, pos_col, slopes,
                lambda_q1_1.reshape(1, -1), lambda_k1_1.reshape(1, -1),
                lambda_q2_1.reshape(1, -1), lambda_k2_1.reshape(1, -1),
                subln_1.reshape(1, -1), lam_init)
        xs = _out_layer(xs, mix, gate_mix, (proj, cols["q_mem"]), (proj, cols["gate_mem"]), kbd, vobd,
                        layer, w_out[layer].astype(jnp.bfloat16), post[layer].reshape(1, d))
    return xs.reshape(b, s, d)
```
